```python
import jax
import jax.numpy as jnp
from jax import lax
import numpy as np

D_MODEL = 2048
BATCH = 2
SEQ = 8192
DEPTH = 4

MIX_WIDTH = D_MODEL
RET_WIDTH = MIX_WIDTH // 2
RET_HEADS = 8
RET_HEAD_DIM = RET_WIDTH // RET_HEADS
RET_CHUNK = 128
RET_GN_EPS = 1e-5
ROPE_BASE = 10000.0
RWKV_WIDTH = MIX_WIDTH // 2
RWKV_HEAD_DIM = 64
RWKV_HEADS = RWKV_WIDTH // RWKV_HEAD_DIM
LORA_W = 64
LORA_A = 64
LORA_G = 128
RWKV_LN_EPS = 64e-5
D_FF = 4 * D_MODEL
NORM_EPS = 1e-6

RET_COLS = (RET_WIDTH, RET_WIDTH, RET_WIDTH, RET_WIDTH)
RWKV_COLS = (RWKV_WIDTH, RWKV_WIDTH, RWKV_WIDTH, LORA_W, LORA_A, LORA_G)
RWKV_SHIFT_WIDTH = sum(RWKV_COLS)
IN_WIDTH = sum(RET_COLS) + RWKV_SHIFT_WIDTH + 2 * D_MODEL

kernel_name = "retention_rwkv7_gated_hybrid"


def _split_cols(z, widths):
    offs = np.cumsum([0] + list(widths))
    return [z[..., int(offs[i]):int(offs[i + 1])] for i in range(len(widths))]


def rms_norm(x, g):
    xf = x.astype(jnp.float32)
    y = xf * lax.rsqrt(jnp.mean(xf * xf, axis=-1, keepdims=True) + NORM_EPS)
    return (y * g.astype(jnp.float32)).astype(x.dtype)


def head_norm(y, g, b, eps):
    H, d = y.shape[-2:]
    mean = jnp.mean(y, axis=-1, keepdims=True)
    var = jnp.mean(jnp.square(y - mean), axis=-1, keepdims=True)
    yn = (y - mean) * lax.rsqrt(var + eps)
    return yn * g.reshape(H, d).astype(jnp.float32) + b.reshape(H, d).astype(jnp.float32)


def rotary(t, positions):
    half = t.shape[-1] // 2
    inv_freq = ROPE_BASE ** (-jnp.arange(half, dtype=jnp.float32) / half)
    ang = positions.astype(jnp.float32)[..., None] * inv_freq
    cos = jnp.cos(ang)[:, :, None, :]
    sin = jnp.sin(ang)[:, :, None, :]
    t1, t2 = t[..., :half], t[..., half:]
    return jnp.concatenate([t1 * cos - t2 * sin, t2 * cos + t1 * sin], axis=-1)


def retention(q, k, v):
    B, S, H, d = q.shape
    C = RET_CHUNK
    N = S // C
    log_gamma = jnp.log1p(-(2.0 ** (-5.0 - jnp.arange(H, dtype=jnp.float32))))
    idx = jnp.arange(C, dtype=jnp.float32)
    diff = idx[:, None] - idx[None, :]
    inner_decay = jnp.where(diff[None] >= 0,
                            jnp.exp(jnp.maximum(diff, 0.0)[None] * log_gamma[:, None, None]), 0.0)
    query_decay = jnp.exp((idx + 1.0)[None, :] * log_gamma[:, None])[None, :, :, None]
    key_decay = jnp.exp((C - 1.0 - idx)[None, :] * log_gamma[:, None])[None, :, :, None]
    chunk_decay = jnp.exp(C * log_gamma)[None, :, None, None]

    def to_chunks(t):
        return t.reshape(B, N, C, H, d).transpose(1, 0, 3, 2, 4)

    def step(R, qkv):
        qn, kn, vn = qkv
        scores = jnp.einsum('bhcd,bhmd->bhcm', qn, kn) * inner_decay
        inner = jnp.einsum('bhcm,bhmd->bhcd', scores, vn)
        cross = jnp.einsum('bhcd,bhde->bhce', qn, R) * query_decay
        R_new = chunk_decay * R + jnp.einsum('bhcd,bhce->bhde', kn * key_decay, vn)
        return R_new, inner + cross

    R0 = jnp.zeros((B, H, d, d), jnp.float32)
    _, out = lax.scan(step, R0, (to_chunks(q), to_chunks(k), to_chunks(v)))
    return out.transpose(1, 0, 3, 2, 4).reshape(B, S, H, d)


def retention_branch(z, positions, gn_g, gn_b):
    B, S, _ = z.shape
    q, k, v, gr = _split_cols(z.astype(jnp.float32), RET_COLS)
    shp = (B, S, RET_HEADS, RET_HEAD_DIM)
    q = rotary(q.reshape(shp), positions)
    k = rotary(k.reshape(shp), positions) * (RET_HEAD_DIM ** -0.5)
    y = retention(q, k, v.reshape(shp))
    y = head_norm(y, gn_g, gn_b, RET_GN_EPS).reshape(B, S, RET_WIDTH)
    return jax.nn.silu(gr) * y


def rwkv7_scan(r, decay, k, v, kk, a):
    B, S, H, d = r.shape

    def step(state, inp):
        r_t, w_t, k_t, v_t, kk_t, a_t = inp
        sa = jnp.einsum('bhvk,bhk->bhv', state, -kk_t)
        state = (state * w_t[:, :, None, :]
                 + sa[..., None] * (kk_t * a_t)[:, :, None, :]
                 + v_t[..., None] * k_t[:, :, None, :])
        y = jnp.einsum('bhvk,bhk->bhv', state, r_t)
        return state, y

    seq_first = lambda t: jnp.moveaxis(t, 1, 0)
    s0 = jnp.zeros((B, H, d, d), jnp.float32)
    _, y = lax.scan(step, s0, tuple(seq_first(t) for t in (r, decay, k, v, kk, a)))
    return jnp.moveaxis(y, 0, 1)


def rwkv7_branch(z, mu, w0, w_up, a0, a_up, g_up, k_k, k_a, r_k, ln_g, ln_b):
    B, S, _ = z.shape
    f32 = jnp.float32
    z = z.astype(f32)
    z_prev = jnp.pad(z, ((0, 0), (1, 0), (0, 0)))[:, :-1]
    z = z + mu.astype(f32) * (z_prev - z)
    r, k, v, zw, za, zg = _split_cols(z, RWKV_COLS)
    w = -jax.nn.softplus(-(w0.astype(f32) + jnp.tanh(zw) @ w_up.astype(f32))) - 0.5
    decay = jnp.exp(-jnp.exp(w))
    a = jax.nn.sigmoid(a0.astype(f32) + za @ a_up.astype(f32))
    g = jax.nn.sigmoid(zg) @ g_up.astype(f32)
    heads = lambda t: t.reshape(B, S, RWKV_HEADS, RWKV_HEAD_DIM)
    kk = heads(k * k_k.astype(f32))
    kk = kk / jnp.maximum(jnp.sqrt(jnp.sum(kk * kk, axis=-1, keepdims=True)), 1e-12)
    k = k * (1.0 + (a - 1.0) * k_a.astype(f32))
    rh, kh, vh = heads(r), heads(k), heads(v)
    y = rwkv7_scan(rh, heads(decay), kh, vh, kk, heads(a))
    y = head_norm(y, ln_g, ln_b, RWKV_LN_EPS)
    bonus = jnp.sum(rh * kh * r_k.astype(f32).reshape(RWKV_HEADS, RWKV_HEAD_DIM), axis=-1, keepdims=True)
    y = y + bonus * vh
    return y.reshape(B, S, RWKV_WIDTH) * g


def setup_inputs(seed: int = 0) -> dict:
    key = jax.random.key(seed)
    ks = jax.random.split(key, 26)
    L, D = DEPTH, D_MODEL
    nrm = lambda k, shp, s: jax.random.normal(k, shp, jnp.float32) * s
    x = jax.random.normal(ks[0], (BATCH, SEQ, D), jnp.float32)
    offset = jax.random.randint(ks[1], (BATCH, 1), 0, 4096, dtype=jnp.int32)
    positions = offset + jnp.arange(SEQ, dtype=jnp.int32)[None, :]
    return {
        'x': x,
        'positions': positions,
        'norm1_g': 1.0 + nrm(ks[2], (L, D), 0.02),
        'w_in': nrm(ks[3], (L, D, IN_WIDTH), D ** -0.5),
        'ret_gn_g': 1.0 + nrm(ks[4], (L, RET_WIDTH), 0.02),
        'ret_gn_b': nrm(ks[5], (L, RET_WIDTH), 0.02),
        'rwkv_mu': jax.random.uniform(ks[6], (L, RWKV_SHIFT_WIDTH), jnp.float32),
        'rwkv_w0': jax.random.uniform(ks[7], (L, RWKV_WIDTH), jnp.float32, -6.0, 0.5),
        'rwkv_w_up': nrm(ks[8], (L, LORA_W, RWKV_WIDTH), 0.5 * LORA_W ** -0.5),
        'rwkv_a0': nrm(ks[9], (L, RWKV_WIDTH), 0.1),
        'rwkv_a_up': nrm(ks[10], (L, LORA_A, RWKV_WIDTH), LORA_A ** -0.5),
        'rwkv_g_up': nrm(ks[11], (L, LORA_G, RWKV_WIDTH), LORA_G ** -0.5),
        'rwkv_k_k': 0.85 + nrm(ks[12], (L, RWKV_WIDTH), 0.05),
        'rwkv_k_a': 1.0 + nrm(ks[13], (L, RWKV_WIDTH), 0.05),
        'rwkv_r_k': nrm(ks[14], (L, RWKV_WIDTH), 0.1),
        'rwkv_ln_g': 1.0 + nrm(ks[15], (L, RWKV_WIDTH), 0.02),
        'rwkv_ln_b': nrm(ks[16], (L, RWKV_WIDTH), 0.02),
        'w_branch_a': nrm(ks[17], (L, RET_WIDTH, D), RET_WIDTH ** -0.5),
        'w_branch_b': nrm(ks[18], (L, RWKV_WIDTH, D), RWKV_WIDTH ** -0.5),
        'w_out': nrm(ks[19], (L, D, D), D ** -0.5),
        'norm2_g': 1.0 + nrm(ks[20], (L, D), 0.02),
        'mlp_up': nrm(ks[21], (L, D, D_FF), D ** -0.5),
        'mlp_down': nrm(ks[22], (L, D_FF, D), D_FF ** -0.5),
        'final_g': 1.0 + nrm(ks[23], (D,), 0.02),
    }


def reference(x, positions, norm1_g, w_in, ret_gn_g, ret_gn_b, rwkv_mu, rwkv_w0, rwkv_w_up,
              rwkv_a0, rwkv_a_up, rwkv_g_up, rwkv_k_k, rwkv_k_a, rwkv_r_k, rwkv_ln_g, rwkv_ln_b,
              w_branch_a, w_branch_b, w_out, norm2_g, mlp_up, mlp_down, final_g):
    for l in range(DEPTH):
        h = rms_norm(x, norm1_g[l])
        z = h @ w_in[l]
        z_ret, z_rwkv, z_ga, z_gb = _split_cols(z, (sum(RET_COLS), RWKV_SHIFT_WIDTH, D_MODEL, D_MODEL))
        y_ret = retention_branch(z_ret, positions, ret_gn_g[l], ret_gn_b[l]).astype(x.dtype)
        y_rwkv = rwkv7_branch(z_rwkv, rwkv_mu[l], rwkv_w0[l], rwkv_w_up[l], rwkv_a0[l], rwkv_a_up[l],
                              rwkv_g_up[l], rwkv_k_k[l], rwkv_k_a[l], rwkv_r_k[l],
                              rwkv_ln_g[l], rwkv_ln_b[l]).astype(x.dtype)
        merged = (jax.nn.sigmoid(z_ga) * (y_ret @ w_branch_a[l])
                  + jax.nn.sigmoid(z_gb) * (y_rwkv @ w_branch_b[l]))
        x = x + merged @ w_out[l]
        h = rms_norm(x, norm2_g[l])
        x = x + jnp.square(jax.nn.relu(h @ mlp_up[l])) @ mlp_down[l]
    return rms_norm(x, final_g)
```

```python
import functools
import math

import jax
import jax.numpy as jnp
from jax import lax
from jax.experimental import pallas as pl
from jax.experimental.pallas import tpu as pltpu

F32 = jnp.float32
BF16 = jnp.bfloat16
HIGHEST = lax.Precision.HIGHEST

RET_HEADS = 8
RET_HEAD_DIM = 128
RET_GN_EPS = 1e-5
ROPE_BASE = 10000.0
RWKV_HEAD_DIM = 64
RWKV_LN_EPS = 64e-5
NORM_EPS = 1e-6
LORA_W = 64
LORA_A = 64
LORA_G = 128

LANES = 128
V7X_VMEM_LIMIT_BYTES = 56 * 1024 * 1024

RET_CHUNK = 128
RWKV_CHUNK = 64

_NT = (((1,), (1,)), ((), ()))
_TN = (((0,), (0,)), ((), ()))


def _params(semantics):
    return pltpu.CompilerParams(dimension_semantics=semantics,
                                vmem_limit_bytes=V7X_VMEM_LIMIT_BYTES)


def _bdot(a, b):
    return jnp.dot(a.astype(BF16), b.astype(BF16), preferred_element_type=F32)


def _mm_kernel(*refs, norm, epilogue):
    if norm:
        x_ref, g_ref, w_ref = refs[:3]
        rest = refs[3:]
    else:
        x_ref, w_ref = refs[:2]
        rest = refs[2:]
    if epilogue == "residual":
        res_ref, o_ref = rest[:2]
        rest = rest[2:]
    else:
        o_ref = rest[0]
        rest = rest[1:]

    if norm:
        xn_ref = rest[0]

        @pl.when(pl.program_id(1) == 0)
        def _():
            x = x_ref[...]
            ms = jnp.mean(x * x, axis=-1, keepdims=True)
            xn_ref[...] = (x * lax.rsqrt(ms + NORM_EPS) * g_ref[...]).astype(BF16)

        lhs = xn_ref[...]
    else:
        lhs = x_ref[...]

    acc = jnp.dot(lhs, w_ref[...], preferred_element_type=F32)
    if epilogue == "relu2":
        acc = jnp.square(jnp.maximum(acc, 0.0))
    elif epilogue == "residual":
        acc = acc + res_ref[...]
    o_ref[...] = acc.astype(o_ref.dtype)


def _matmul(x, w, *, name, g=None, residual=None, epilogue="none", out_dtype=F32, tm, tn):
    M, K = x.shape
    N = w.shape[1]
    norm = g is not None
    in_specs = [pl.BlockSpec((tm, K), lambda i, j: (i, 0))]
    args = [x]
    if norm:
        in_specs.append(pl.BlockSpec((1, K), lambda i, j: (0, 0)))
        args.append(g.reshape(1, K))
    in_specs.append(pl.BlockSpec((K, tn), lambda i, j: (0, j)))
    args.append(w)
    if epilogue == "residual":
        in_specs.append(pl.BlockSpec((tm, tn), lambda i, j: (i, j)))
        args.append(residual)
    scratch = [pltpu.VMEM((tm, K), BF16)] if norm else []
    return pl.pallas_call(
        functools.partial(_mm_kernel, norm=norm, epilogue=epilogue),
        grid=(M // tm, N // tn),
        in_specs=in_specs,
        out_specs=pl.BlockSpec((tm, tn), lambda i, j: (i, j)),
        out_shape=jax.ShapeDtypeStruct((M, N), out_dtype),
        scratch_shapes=scratch,
        compiler_params=_params(("parallel", "arbitrary")),
        name=name,
    )(*args)


def _merge_kernel(ya_ref, yb_ref, wa_ref, wb_ref, ga_ref, gb_ref, o_ref):
    a = jnp.dot(ya_ref[...], wa_ref[...], preferred_element_type=F32)
    b = jnp.dot(yb_ref[...], wb_ref[...], preferred_element_type=F32)
    o_ref[...] = (jax.nn.sigmoid(ga_ref[...]) * a
                  + jax.nn.sigmoid(gb_ref[...]) * b).astype(o_ref.dtype)


def _merge(ya, yb, wa, wb, z, ga_col, gb_col, *, tm, tn):
    M, K = ya.shape
    N = wa.shape[1]
    return pl.pallas_call(
        _merge_kernel,
        grid=(M // tm, N // tn),
        in_specs=[
            pl.BlockSpec((tm, K), lambda i, j: (i, 0)),
            pl.BlockSpec((tm, K), lambda i, j: (i, 0)),
            pl.BlockSpec((K, tn), lambda i, j: (0, j)),
            pl.BlockSpec((K, tn), lambda i, j: (0, j)),
            pl.BlockSpec((tm, tn), lambda i, j: (i, ga_col // tn + j)),
            pl.BlockSpec((tm, tn), lambda i, j: (i, gb_col // tn + j)),
        ],
        out_specs=pl.BlockSpec((tm, tn), lambda i, j: (i, j)),
        out_shape=jax.ShapeDtypeStruct((M, N), BF16),
        compiler_params=_params(("parallel", "arbitrary")),
        name="gated_merge",
    )(ya, yb, wa, wb, z, z)


def _rmsnorm_kernel(x_ref, g_ref, o_ref):
    x = x_ref[...]
    ms = jnp.mean(x * x, axis=-1, keepdims=True)
    o_ref[...] = x * lax.rsqrt(ms + NORM_EPS) * g_ref[...]


def _rmsnorm(x, g, *, tm):
    M, D = x.shape
    return pl.pallas_call(
        _rmsnorm_kernel,
        grid=(M // tm,),
        in_specs=[pl.BlockSpec((tm, D), lambda i: (i, 0)),
                  pl.BlockSpec((1, D), lambda i: (0, 0))],
        out_specs=pl.BlockSpec((tm, D), lambda i: (i, 0)),
        out_shape=jax.ShapeDtypeStruct((M, D), F32),
        compiler_params=_params(("parallel",)),
        name="final_rmsnorm",
    )(x, g.reshape(1, D))


def _rope_kernel(pos_ref, cos_ref, sin_ref):
    half = RET_HEAD_DIM // 2
    pos = pos_ref[...].astype(F32)
    lane = lax.broadcasted_iota(jnp.int32, (1, RET_HEAD_DIM), 1)
    j = jnp.where(lane < half, lane, lane - half).astype(F32)
    inv_freq = jnp.exp(j * (-math.log(ROPE_BASE) / half))
    ang = pos * inv_freq
    cos_ref[...] = jnp.cos(ang)
    sin_ref[...] = jnp.where(lane < half, -1.0, 1.0) * jnp.sin(ang)


def _rope_tables(positions, *, tm):
    M = positions.size
    pos = positions.reshape(M, 1)
    return pl.pallas_call(
        _rope_kernel,
        grid=(M // tm,),
        in_specs=[pl.BlockSpec((tm, 1), lambda i: (i, 0))],
        out_specs=[pl.BlockSpec((tm, RET_HEAD_DIM), lambda i: (i, 0))] * 2,
        out_shape=[jax.ShapeDtypeStruct((M, RET_HEAD_DIM), F32)] * 2,
        compiler_params=_params(("parallel",)),
        name="rope_tables",
    )(pos)


def _retention_kernel(q_ref, k_ref, v_ref, gr_ref, cos_ref, sin_ref, gng_ref, gnb_ref,
                      o_ref, state_ref):
    C = q_ref.shape[1]
    d = RET_HEAD_DIM

    @pl.when(pl.program_id(1) == 0)
    def _():
        state_ref[...] = jnp.zeros_like(state_ref)

    cos2 = cos_ref[0]
    sin2 = sin_ref[0]
    row = lax.broadcasted_iota(jnp.int32, (C, C), 0)
    col = lax.broadcasted_iota(jnp.int32, (C, C), 1)
    diff = (row - col).astype(F32)
    causal = row >= col
    rowd = lax.broadcasted_iota(jnp.int32, (C, d), 0).astype(F32)

    for h in range(RET_HEADS):
        sl = slice(h * d, (h + 1) * d)
        log_gamma = math.log1p(-(2.0 ** (-5.0 - h)))
        q = q_ref[0, :, sl]
        k = k_ref[0, :, sl]
        v = v_ref[0, :, sl].astype(BF16)
        q = q * cos2 + pltpu.roll(q, d // 2, axis=1) * sin2
        k = (k * cos2 + pltpu.roll(k, d // 2, axis=1) * sin2) * (d ** -0.5)
        inner_decay = jnp.where(causal, jnp.exp(jnp.maximum(diff, 0.0) * log_gamma), 0.0)
        query_decay = jnp.exp((rowd + 1.0) * log_gamma)
        key_decay = jnp.exp((C - 1.0 - rowd) * log_gamma)
        chunk_decay = math.exp(C * log_gamma)

        qb = q.astype(BF16)
        scores = lax.dot_general(qb, k.astype(BF16), _NT, preferred_element_type=F32)
        inner = _bdot(scores * inner_decay, v)
        state = state_ref[h]
        cross = _bdot(qb, state) * query_decay
        state_ref[h] = chunk_decay * state + lax.dot_general(
            (k * key_decay).astype(BF16), v, _TN, preferred_element_type=F32)

        y = inner + cross
        mean = jnp.mean(y, axis=-1, keepdims=True)
        yc = y - mean
        var = jnp.mean(yc * yc, axis=-1, keepdims=True)
        yn = yc * lax.rsqrt(var + RET_GN_EPS) * gng_ref[:, sl] + gnb_ref[:, sl]
        gr = gr_ref[0, :, sl]
        o_ref[0, :, sl] = (gr * jax.nn.sigmoid(gr) * yn).astype(o_ref.dtype)


def _retention(z, cos2, sin2, gn_g, gn_b, *, col0):
    B, S, _ = z.shape
    W = RET_HEADS * RET_HEAD_DIM
    C = RET_CHUNK
    c0 = col0 // W
    zspec = lambda off: pl.BlockSpec((1, C, W), lambda b, n: (b, n, c0 + off))
    rspec = pl.BlockSpec((1, C, RET_HEAD_DIM), lambda b, n: (b, n, 0))
    pspec = pl.BlockSpec((1, W), lambda b, n: (0, 0))
    return pl.pallas_call(
        _retention_kernel,
        grid=(B, S // C),
        in_specs=[zspec(0), zspec(1), zspec(2), zspec(3), rspec, rspec, pspec, pspec],
        out_specs=pl.BlockSpec((1, C, W), lambda b, n: (b, n, 0)),
        out_shape=jax.ShapeDtypeStruct((B, S, W), BF16),
        scratch_shapes=[pltpu.VMEM((RET_HEADS, RET_HEAD_DIM, RET_HEAD_DIM), F32)],
        compiler_params=_params(("parallel", "arbitrary")),
        name="retention",
    )(z, z, z, z, cos2, sin2, gn_g.reshape(1, W), gn_b.reshape(1, W))


def _rwkv_kernel(r_ref, k_ref, v_ref, lo_ref, mu_r_ref, mu_k_ref, mu_v_ref, mu_lo_ref,
                 w0_ref, wup_ref, a0_ref, aup_ref, gup_ref, kk_ref, ka_ref, rk_ref,
                 lng_ref, lnb_ref, o_ref,
                 state_ref, pr_ref, pk_ref, pv_ref, plo_ref):
    C = r_ref.shape[1]
    W = r_ref.shape[2]
    hd = RWKV_HEAD_DIM
    P = 2 * hd
    assert P == LANES and 2 * C == P

    @pl.when(pl.program_id(1) == 0)
    def _():
        state_ref[...] = jnp.zeros_like(state_ref)
        pr_ref[...] = jnp.zeros_like(pr_ref)
        pk_ref[...] = jnp.zeros_like(pk_ref)
        pv_ref[...] = jnp.zeros_like(pv_ref)
        plo_ref[...] = jnp.zeros_like(plo_ref)

    def shift_mix(x, prev_ref, mu, sl):
        first = lax.broadcasted_iota(jnp.int32, x.shape, 0) == 0
        xprev = jnp.where(first, prev_ref[:, sl], pltpu.roll(x, 1, axis=0))
        return x + mu * (xprev - x)

    lo_raw = lo_ref[0]
    lo = shift_mix(lo_raw, plo_ref, mu_lo_ref[...], slice(None))
    tw = jnp.tanh(lo[:, 0:LORA_W]).astype(BF16)
    za = lo[:, LORA_W:LORA_W + LORA_A].astype(BF16)
    sg = jax.nn.sigmoid(lo[:, LORA_W + LORA_A:]).astype(BF16)

    row = lax.broadcasted_iota(jnp.int32, (C, P), 0)
    lane = lax.broadcasted_iota(jnp.int32, (C, P), 1)
    head0 = lane < hd
    cj = jnp.where(head0, lane, lane - hd)
    strict = row > cj
    incl = row >= cj
    eye2 = jnp.where(row == cj, 1.0, 0.0)
    r2 = lax.broadcasted_iota(jnp.int32, (P, P), 0)
    l2 = lax.broadcasted_iota(jnp.int32, (P, P), 1)
    bdmask = (r2 < hd) == (l2 < hd)
    bd_ones = jnp.where(bdmask, 1.0, 0.0)
    tri = jnp.where(lax.broadcasted_iota(jnp.int32, (C, C), 0)
                    >= lax.broadcasted_iota(jnp.int32, (C, C), 1), 1.0, 0.0)

    def blockdiag(a):
        return jnp.where(bdmask, jnp.concatenate([a, a], axis=0), jnp.zeros((), a.dtype))

    def segsum(a):
        return jnp.dot(a, bd_ones, precision=HIGHEST, preferred_element_type=F32)

    for p in range(W // P):
        sl = slice(p * P, (p + 1) * P)
        r_raw = r_ref[0, :, sl]
        k_raw = k_ref[0, :, sl]
        v_raw = v_ref[0, :, sl]
        r = shift_mix(r_raw, pr_ref, mu_r_ref[:, sl], sl)
        k = shift_mix(k_raw, pk_ref, mu_k_ref[:, sl], sl)
        v = shift_mix(v_raw, pv_ref, mu_v_ref[:, sl], sl)

        u = -(w0_ref[:, sl] + jnp.dot(tw, wup_ref[:, sl], preferred_element_type=F32))
        softplus = jnp.maximum(u, 0.0) + jnp.log1p(jnp.exp(-jnp.abs(u)))
        logdec = -jnp.exp(-softplus - 0.5)
        a = jax.nn.sigmoid(a0_ref[:, sl] + jnp.dot(za, aup_ref[:, sl], preferred_element_type=F32))
        g = jnp.dot(sg, gup_ref[:, sl], preferred_element_type=F32)

        kk = k * kk_ref[:, sl]
        kk = kk / jnp.maximum(jnp.sqrt(segsum(kk * kk)), 1e-12)
        k = k * (1.0 + (a - 1.0) * ka_ref[:, sl])

        cum = jnp.dot(tri, logdec, precision=HIGHEST, preferred_element_type=F32)
        cum_last = cum[C - 1:C, :]
        dec_to_end = jnp.exp(cum_last - cum)
        inv_p = jnp.exp(-cum)
        rt = r * jnp.exp(cum)
        kt = k * inv_p
        at = -kk * jnp.exp(cum - logdec)
        bt = kk * a * inv_p
        b_end = kk * a * dec_to_end
        k_end = k * dec_to_end

        lhs = jnp.concatenate([at, rt], axis=0).astype(BF16)
        btb = bt.astype(BF16)
        ktb = kt.astype(BF16)
        zero = jnp.zeros((), BF16)
        rhs = jnp.concatenate([jnp.where(head0, btb, zero), jnp.where(head0, zero, btb),
                               jnp.where(head0, ktb, zero), jnp.where(head0, zero, ktb)], axis=0)
        gram = lax.dot_general(lhs, rhs, _NT, preferred_element_type=F32)
        l_ab = jnp.where(strict, gram[:C, :P], 0.0)
        l_ak = jnp.where(strict, gram[:C, P:], 0.0)
        m_rb = jnp.where(incl, gram[C:, :P], 0.0)
        m_rk = jnp.where(incl, gram[C:, P:], 0.0)

        t = eye2 + l_ab
        m = _bdot(l_ab, blockdiag(l_ab.astype(BF16)))
        n_sq = int(math.log2(C)) - 1
        for step in range(n_sq):
            mb = m.astype(BF16)
            if step < n_sq - 1:
                both = jnp.concatenate([blockdiag(t.astype(BF16)), blockdiag(mb)], axis=1)
                prod = jnp.dot(mb, both, preferred_element_type=F32)
                t = t + prod[:, :P]
                m = prod[:, P:]
            else:
                t = t + jnp.dot(mb, blockdiag(t.astype(BF16)), preferred_element_type=F32)

        vb = v.astype(BF16)
        state = state_ref[p]
        ar = lax.dot_general(lhs, state.astype(BF16), _NT, preferred_element_type=F32)
        lv = jnp.dot(jnp.concatenate([l_ak, m_rk], axis=0).astype(BF16), blockdiag(vb),
                     preferred_element_type=F32)
        w_in = ar[:C] + lv[:C]
        u_mat = _bdot(t, blockdiag(w_in.astype(BF16)))
        ub = u_mat.astype(BF16)
        y = ar[C:] + lv[C:] + _bdot(m_rb, blockdiag(ub))
        upd = lax.dot_general(jnp.concatenate([ub, vb], axis=0),
                              jnp.concatenate([b_end, k_end], axis=0).astype(BF16),
                              _TN, preferred_element_type=F32)
        state_ref[p] = state * jnp.exp(cum_last) + jnp.where(bdmask, upd, 0.0)

        mean = segsum(y) * (1.0 / hd)
        yc = y - mean
        var = segsum(yc * yc) * (1.0 / hd)
        yn = yc * lax.rsqrt(var + RWKV_LN_EPS) * lng_ref[:, sl] + lnb_ref[:, sl]
        bonus = segsum(r * k * rk_ref[:, sl])
        o_ref[0, :, sl] = ((yn + bonus * v) * g).astype(o_ref.dtype)

        pr_ref[:, sl] = r_raw[C - 1:C, :]
        pk_ref[:, sl] = k_raw[C - 1:C, :]
        pv_ref[:, sl] = v_raw[C - 1:C, :]
    plo_ref[...] = lo_raw[C - 1:C, :]


def _rwkv(z, mu, w0, w_up, a0, a_up, g_up, k_k, k_a, r_k, ln_g, ln_b, *, col0):
    B, S, _ = z.shape
    W = w0.shape[-1]
    C = RWKV_CHUNK
    LO = LORA_W + LORA_A + LORA_G
    P = 2 * RWKV_HEAD_DIM
    c0 = col0 // W
    lo0 = (col0 + 3 * W) // LO
    zspec = lambda off: pl.BlockSpec((1, C, W), lambda b, n: (b, n, c0 + off))
    full = lambda a: pl.BlockSpec(a.shape, lambda b, n: (0,) * a.ndim)
    row = lambda a: a.reshape(1, -1)
    params = [row(mu[:W]), row(mu[W:2 * W]), row(mu[2 * W:3 * W]), row(mu[3 * W:]),
              row(w0), w_up.astype(BF16), row(a0), a_up.astype(BF16), g_up.astype(BF16),
              row(k_k), row(k_a), row(r_k), row(ln_g), row(ln_b)]
    return pl.pallas_call(
        _rwkv_kernel,
        grid=(B, S // C),
        in_specs=[zspec(0), zspec(1), zspec(2),
                  pl.BlockSpec((1, C, LO), lambda b, n: (b, n, lo0))] + [full(a) for a in params],
        out_specs=pl.BlockSpec((1, C, W), lambda b, n: (b, n, 0)),
        out_shape=jax.ShapeDtypeStruct((B, S, W), BF16),
        scratch_shapes=[pltpu.VMEM((W // P, P, P), F32),
                        pltpu.VMEM((1, W), F32), pltpu.VMEM((1, W), F32),
                        pltpu.VMEM((1, W), F32), pltpu.VMEM((1, LO), F32)],
        compiler_params=_params(("parallel", "arbitrary")),
        name="rwkv7",
    )(z, z, z, z, *params)


def kernel(x, positions, norm1_g, w_in, ret_gn_g, ret_gn_b, rwkv_mu, rwkv_w0, rwkv_w_up, rwkv_a0, rwkv_a_up, rwkv_g_up, rwkv_k_k, rwkv_k_a, rwkv_r_k, rwkv_ln_g, rwkv_ln_b, w_branch_a, w_branch_b, w_out, norm2_g, mlp_up, mlp_down, final_g):
    B, S, D = x.shape
    M = B * S
    depth = w_in.shape[0]
    ret_w = ret_gn_g.shape[-1]
    rwkv_w = rwkv_w0.shape[-1]
    ret_cols = 4 * ret_w
    rwkv_cols = rwkv_mu.shape[-1]
    ga_col = ret_cols
    gb_col = ret_cols + D
    rwkv_col = ret_cols + 2 * D

    cos2, sin2 = _rope_tables(positions, tm=2048)
    cos2 = cos2.reshape(B, S, RET_HEAD_DIM)
    sin2 = sin2.reshape(B, S, RET_HEAD_DIM)

    xf = x.reshape(M, D)
    for l in range(depth):
        w = w_in[l]
        w_cat = jnp.concatenate([w[:, :ret_cols], w[:, ret_cols + rwkv_cols:],
                                 w[:, ret_cols:ret_cols + rwkv_cols]], axis=1).astype(BF16)
        z = _matmul(xf, w_cat, name="norm_in_proj", g=norm1_g[l], tm=512, tn=1280)
        z3 = z.reshape(B, S, -1)
        y_ret = _retention(z3, cos2, sin2, ret_gn_g[l], ret_gn_b[l], col0=0)
        y_rwkv = _rwkv(z3, rwkv_mu[l], rwkv_w0[l], rwkv_w_up[l], rwkv_a0[l], rwkv_a_up[l],
                       rwkv_g_up[l], rwkv_k_k[l], rwkv_k_a[l], rwkv_r_k[l],
                       rwkv_ln_g[l], rwkv_ln_b[l], col0=rwkv_col)
        merged = _merge(y_ret.reshape(M, ret_w), y_rwkv.reshape(M, rwkv_w),
                        w_branch_a[l].astype(BF16), w_branch_b[l].astype(BF16),
                        z, ga_col, gb_col, tm=512, tn=512)
        xf = _matmul(merged, w_out[l].astype(BF16), name="out_proj", residual=xf,
                     epilogue="residual", tm=512, tn=1024)
        h = _matmul(xf, mlp_up[l].astype(BF16), name="norm_mlp_up", g=norm2_g[l],
                    epilogue="relu2", out_dtype=BF16, tm=512, tn=1024)
        xf = _matmul(h, mlp_down[l].astype(BF16), name="mlp_down", residual=xf,
                     epilogue="residual", tm=512, tn=512)
    return _rmsnorm(xf, final_g, tm=512).reshape(B, S, D)
```

```python
import functools
import math

import jax
import jax.numpy as jnp
from jax import lax
from jax.experimental import pallas as pl
from jax.experimental.pallas import tpu as pltpu

F32 = jnp.float32
BF16 = jnp.bfloat16
HIGHEST = lax.Precision.HIGHEST

RET_HEADS = 8
RET_HEAD_DIM = 128
RET_GN_EPS = 1e-5
ROPE_BASE = 10000.0
RWKV_HEAD_DIM = 64
RWKV_LN_EPS = 64e-5
NORM_EPS = 1e-6
LORA_W = 64
LORA_A = 64
LORA_G = 128

LANES = 128
V7X_VMEM_LIMIT_BYTES = 56 * 1024 * 1024

RET_CHUNK = 128
RWKV_CHUNK = 64

_NT = (((1,), (1,)), ((), ()))
_TN = (((0,), (0,)), ((), ()))


def _params(semantics):
    return pltpu.CompilerParams(dimension_semantics=semantics,
                                vmem_limit_bytes=V7X_VMEM_LIMIT_BYTES)


def _bdot(a, b):
    return jnp.dot(a.astype(BF16), b.astype(BF16), preferred_element_type=F32)


def _mm_kernel(*refs, norm, epilogue):
    if norm:
        x_ref, g_ref, w_ref = refs[:3]
        rest = refs[3:]
    else:
        x_ref, w_ref = refs[:2]
        rest = refs[2:]
    if epilogue == "residual":
        res_ref, o_ref = rest[:2]
        rest = rest[2:]
    else:
        o_ref = rest[0]
        rest = rest[1:]

    if norm:
        xn_ref = rest[0]

        @pl.when(pl.program_id(1) == 0)
        def _():
            x = x_ref[...]
            ms = jnp.mean(x * x, axis=-1, keepdims=True)
            xn_ref[...] = (x * lax.rsqrt(ms + NORM_EPS) * g_ref[...]).astype(BF16)

        lhs = xn_ref[...]
    else:
        lhs = x_ref[...]

    acc = jnp.dot(lhs, w_ref[...], preferred_element_type=F32)
    if epilogue == "relu2":
        acc = jnp.square(jnp.maximum(acc, 0.0))
    elif epilogue == "residual":
        acc = acc + res_ref[...]
    o_ref[...] = acc.astype(o_ref.dtype)


def _matmul(x, w, *, name, g=None, residual=None, epilogue="none", out_dtype=F32, tm, tn):
    M, K = x.shape
    N = w.shape[1]
    norm = g is not None
    in_specs = [pl.BlockSpec((tm, K), lambda i, j: (i, 0))]
    args = [x]
    if norm:
        in_specs.append(pl.BlockSpec((1, K), lambda i, j: (0, 0)))
        args.append(g.reshape(1, K))
    in_specs.append(pl.BlockSpec((K, tn), lambda i, j: (0, j)))
    args.append(w)
    if epilogue == "residual":
        in_specs.append(pl.BlockSpec((tm, tn), lambda i, j: (i, j)))
        args.append(residual)
    scratch = [pltpu.VMEM((tm, K), BF16)] if norm else []
    return pl.pallas_call(
        functools.partial(_mm_kernel, norm=norm, epilogue=epilogue),
        grid=(M // tm, N // tn),
        in_specs=in_specs,
        out_specs=pl.BlockSpec((tm, tn), lambda i, j: (i, j)),
        out_shape=jax.ShapeDtypeStruct((M, N), out_dtype),
        scratch_shapes=scratch,
        compiler_params=_params(("parallel", "arbitrary")),
        name=name,
    )(*args)


def _merge_kernel(ya_ref, yb_ref, wa_ref, wb_ref, ga_ref, gb_ref, o_ref):
    a = jnp.dot(ya_ref[...], wa_ref[...], preferred_element_type=F32)
    b = jnp.dot(yb_ref[...], wb_ref[...], preferred_element_type=F32)
    o_ref[...] = (jax.nn.sigmoid(ga_ref[...]) * a
                  + jax.nn.sigmoid(gb_ref[...]) * b).astype(o_ref.dtype)


def _merge(ya, yb, wa, wb, z, ga_col, gb_col, *, tm, tn):
    M, K = ya.shape
    N = wa.shape[1]
    return pl.pallas_call(
        _merge_kernel,
        grid=(M // tm, N // tn),
        in_specs=[
            pl.BlockSpec((tm, K), lambda i, j: (i, 0)),
            pl.BlockSpec((tm, K), lambda i, j: (i, 0)),
            pl.BlockSpec((K, tn), lambda i, j: (0, j)),
            pl.BlockSpec((K, tn), lambda i, j: (0, j)),
            pl.BlockSpec((tm, tn), lambda i, j: (i, ga_col // tn + j)),
            pl.BlockSpec((tm, tn), lambda i, j: (i, gb_col // tn + j)),
        ],
        out_specs=pl.BlockSpec((tm, tn), lambda i, j: (i, j)),
        out_shape=jax.ShapeDtypeStruct((M, N), BF16),
        compiler_params=_params(("parallel", "arbitrary")),
        name="gated_merge",
    )(ya, yb, wa, wb, z, z)


def _rmsnorm_kernel(x_ref, g_ref, o_ref):
    x = x_ref[...]
    ms = jnp.mean(x * x, axis=-1, keepdims=True)
    o_ref[...] = x * lax.rsqrt(ms + NORM_EPS) * g_ref[...]


def _rmsnorm(x, g, *, tm):
    M, D = x.shape
    return pl.pallas_call(
        _rmsnorm_kernel,
        grid=(M // tm,),
        in_specs=[pl.BlockSpec((tm, D), lambda i: (i, 0)),
                  pl.BlockSpec((1, D), lambda i: (0, 0))],
        out_specs=pl.BlockSpec((tm, D), lambda i: (i, 0)),
        out_shape=jax.ShapeDtypeStruct((M, D), F32),
        compiler_params=_params(("parallel",)),
        name="final_rmsnorm",
    )(x, g.reshape(1, D))


def _rope_kernel(pos_ref, cos_ref, sin_ref):
    half = RET_HEAD_DIM // 2
    pos = pos_ref[...].astype(F32)
    lane = lax.broadcasted_iota(jnp.int32, (1, RET_HEAD_DIM), 1)
    j = jnp.where(lane < half, lane, lane - half).astype(F32)
    inv_freq = jnp.exp(j * (-math.log(ROPE_BASE) / half))
    ang = pos * inv_freq
    cos_ref[...] = jnp.cos(ang)
    sin_ref[...] = jnp.where(lane < half, -1.0, 1.0) * jnp.sin(ang)


def _rope_tables(positions, *, tm):
    M = positions.size
    pos = positions.reshape(M, 1)
    return pl.pallas_call(
        _rope_kernel,
        grid=(M // tm,),
        in_specs=[pl.BlockSpec((tm, 1), lambda i: (i, 0))],
        out_specs=[pl.BlockSpec((tm, RET_HEAD_DIM), lambda i: (i, 0))] * 2,
        out_shape=[jax.ShapeDtypeStruct((M, RET_HEAD_DIM), F32)] * 2,
        compiler_params=_params(("parallel",)),
        name="rope_tables",
    )(pos)


def _retention_kernel(q_ref, k_ref, v_ref, gr_ref, cos_ref, sin_ref, gng_ref, gnb_ref,
                      o_ref, state_ref):
    C = q_ref.shape[1]
    d = RET_HEAD_DIM

    @pl.when(pl.program_id(1) == 0)
    def _():
        state_ref[...] = jnp.zeros_like(state_ref)

    cos2 = cos_ref[0]
    sin2 = sin_ref[0]
    row = lax.broadcasted_iota(jnp.int32, (C, C), 0)
    col = lax.broadcasted_iota(jnp.int32, (C, C), 1)
    diff = (row - col).astype(F32)
    causal = row >= col
    rowd = lax.broadcasted_iota(jnp.int32, (C, d), 0).astype(F32)

    for h in range(RET_HEADS):
        sl = slice(h * d, (h + 1) * d)
        log_gamma = math.log1p(-(2.0 ** (-5.0 - h)))
        q = q_ref[0, :, sl]
        k = k_ref[0, :, sl]
        v = v_ref[0, :, sl].astype(BF16)
        q = q * cos2 + pltpu.roll(q, d // 2, axis=1) * sin2
        k = (k * cos2 + pltpu.roll(k, d // 2, axis=1) * sin2) * (d ** -0.5)
        inner_decay = jnp.where(causal, jnp.exp(jnp.maximum(diff, 0.0) * log_gamma), 0.0)
        query_decay = jnp.exp((rowd + 1.0) * log_gamma)
        key_decay = jnp.exp((C - 1.0 - rowd) * log_gamma)
        chunk_decay = math.exp(C * log_gamma)

        qb = q.astype(BF16)
        scores = lax.dot_general(qb, k.astype(BF16), _NT, preferred_element_type=F32)
        inner = _bdot(scores * inner_decay, v)
        state = state_ref[h]
        cross = _bdot(qb, state) * query_decay
        state_ref[h] = chunk_decay * state + lax.dot_general(
            (k * key_decay).astype(BF16), v, _TN, preferred_element_type=F32)

        y = inner + cross
        mean = jnp.mean(y, axis=-1, keepdims=True)
        yc = y - mean
        var = jnp.mean(yc * yc, axis=-1, keepdims=True)
        yn = yc * lax.rsqrt(var + RET_GN_EPS) * gng_ref[:, sl] + gnb_ref[:, sl]
        gr = gr_ref[0, :, sl]
        o_ref[0, :, sl] = (gr * jax.nn.sigmoid(gr) * yn).astype(o_ref.dtype)


def _retention(z, cos2, sin2, gn_g, gn_b, *, col0):
    B, S, _ = z.shape
    W = RET_HEADS * RET_HEAD_DIM
    C = RET_CHUNK
    c0 = col0 // W
    zspec = lambda off: pl.BlockSpec((1, C, W), lambda b, n: (b, n, c0 + off))
    rspec = pl.BlockSpec((1, C, RET_HEAD_DIM), lambda b, n: (b, n, 0))
    pspec = pl.BlockSpec((1, W), lambda b, n: (0, 0))
    return pl.pallas_call(
        _retention_kernel,
        grid=(B, S // C),
        in_specs=[zspec(0), zspec(1), zspec(2), zspec(3), rspec, rspec, pspec, pspec],
        out_specs=pl.BlockSpec((1, C, W), lambda b, n: (b, n, 0)),
        out_shape=jax.ShapeDtypeStruct((B, S, W), BF16),
        scratch_shapes=[pltpu.VMEM((RET_HEADS, RET_HEAD_DIM, RET_HEAD_DIM), F32)],
        compiler_params=_params(("parallel", "arbitrary")),
        name="retention",
    )(z, z, z, z, cos2, sin2, gn_g.reshape(1, W), gn_b.reshape(1, W))


def _rwkv_kernel(r_ref, k_ref, v_ref, lo_ref, mu_r_ref, mu_k_ref, mu_v_ref, mu_lo_ref,
                 w0_ref, wup_ref, a0_ref, aup_ref, gup_ref, kk_ref, ka_ref, rk_ref,
                 lng_ref, lnb_ref, o_ref,
                 state_ref, pr_ref, pk_ref, pv_ref, plo_ref):
    C = r_ref.shape[1]
    W = r_ref.shape[2]
    hd = RWKV_HEAD_DIM
    P = 2 * hd
    assert P == LANES and 2 * C == P

    @pl.when(pl.program_id(1) == 0)
    def _():
        state_ref[...] = jnp.zeros_like(state_ref)
        pr_ref[...] = jnp.zeros_like(pr_ref)
        pk_ref[...] = jnp.zeros_like(pk_ref)
        pv_ref[...] = jnp.zeros_like(pv_ref)
        plo_ref[...] = jnp.zeros_like(plo_ref)

    def shift_mix(x, prev_ref, mu, sl):
        first = lax.broadcasted_iota(jnp.int32, x.shape, 0) == 0
        xprev = jnp.where(first, prev_ref[:, sl], pltpu.roll(x, 1, axis=0))
        return x + mu * (xprev - x)

    lo_raw = lo_ref[0]
    lo = shift_mix(lo_raw, plo_ref, mu_lo_ref[...], slice(None))
    tw = jnp.tanh(lo[:, 0:LORA_W]).astype(BF16)
    za = lo[:, LORA_W:LORA_W + LORA_A].astype(BF16)
    sg = jax.nn.sigmoid(lo[:, LORA_W + LORA_A:]).astype(BF16)

    row = lax.broadcasted_iota(jnp.int32, (C, P), 0)
    lane = lax.broadcasted_iota(jnp.int32, (C, P), 1)
    head0 = lane < hd
    cj = jnp.where(head0, lane, lane - hd)
    strict = row > cj
    incl = row >= cj
    eye2 = jnp.where(row == cj, 1.0, 0.0)
    r2 = lax.broadcasted_iota(jnp.int32, (P, P), 0)
    l2 = lax.broadcasted_iota(jnp.int32, (P, P), 1)
    bdmask = (r2 < hd) == (l2 < hd)
    bd_ones = jnp.where(bdmask, 1.0, 0.0)
    tri = jnp.where(lax.broadcasted_iota(jnp.int32, (C, C), 0)
                    >= lax.broadcasted_iota(jnp.int32, (C, C), 1), 1.0, 0.0)

    def blockdiag(a):
        return jnp.where(bdmask, jnp.concatenate([a, a], axis=0), jnp.zeros((), a.dtype))

    def segsum(a):
        return jnp.dot(a, bd_ones, precision=HIGHEST, preferred_element_type=F32)

    for p in range(W // P):
        sl = slice(p * P, (p + 1) * P)
        r_raw = r_ref[0, :, sl]
        k_raw = k_ref[0, :, sl]
        v_raw = v_ref[0, :, sl]
        r = shift_mix(r_raw, pr_ref, mu_r_ref[:, sl], sl)
        k = shift_mix(k_raw, pk_ref, mu_k_ref[:, sl], sl)
        v = shift_mix(v_raw, pv_ref, mu_v_ref[:, sl], sl)

        u = -(w0_ref[:, sl] + jnp.dot(tw, wup_ref[:, sl], preferred_element_type=F32))
        softplus = jnp.maximum(u, 0.0) + jnp.log1p(jnp.exp(-jnp.abs(u)))
        logdec = -jnp.exp(-softplus - 0.5)
        a = jax.nn.sigmoid(a0_ref[:, sl] + jnp.dot(za, aup_ref[:, sl], preferred_element_type=F32))
        g = jnp.dot(sg, gup_ref[:, sl], preferred_element_type=F32)

        kk = k * kk_ref[:, sl]
        kk = kk / jnp.maximum(jnp.sqrt(segsum(kk * kk)), 1e-12)
        k = k * (1.0 + (a - 1.0) * ka_ref[:, sl])

        cum = jnp.dot(tri, logdec, precision=HIGHEST, preferred_element_type=F32)
        cum_last = cum[C - 1:C, :]
        dec_to_end = jnp.exp(cum_last - cum)
        inv_p = jnp.exp(-cum)
        rt = r * jnp.exp(cum)
        kt = k * inv_p
        at = -kk * jnp.exp(cum - logdec)
        bt = kk * a * inv_p
        b_end = kk * a * dec_to_end
        k_end = k * dec_to_end

        lhs = jnp.concatenate([at, rt], axis=0).astype(BF16)
        btb = bt.astype(BF16)
        ktb = kt.astype(BF16)
        zero = jnp.zeros((), BF16)
        rhs = jnp.concatenate([jnp.where(head0, btb, zero), jnp.where(head0, zero, btb),
                               jnp.where(head0, ktb, zero), jnp.where(head0, zero, ktb)], axis=0)
        gram = lax.dot_general(lhs, rhs, _NT, preferred_element_type=F32)
        l_ab = jnp.where(strict, gram[:C, :P], 0.0)
        l_ak = jnp.where(strict, gram[:C, P:], 0.0)
        m_rb = jnp.where(incl, gram[C:, :P], 0.0)
        m_rk = jnp.where(incl, gram[C:, P:], 0.0)

        t = eye2 + l_ab
        m = _bdot(l_ab, blockdiag(l_ab.astype(BF16)))
        n_sq = int(math.log2(C)) - 1
        for step in range(n_sq):
            mb = m.astype(BF16)
            if step < n_sq - 1:
                both = jnp.concatenate([blockdiag(t.astype(BF16)), blockdiag(mb)], axis=1)
                prod = jnp.dot(mb, both, preferred_element_type=F32)
                t = t + prod[:, :P]
                m = prod[:, P:]
            else:
                t = t + jnp.dot(mb, blockdiag(t.astype(BF16)), preferred_element_type=F32)

        vb = v.astype(BF16)
        state = state_ref[p]
        ar = lax.dot_general(lhs, state.astype(BF16), _NT, preferred_element_type=F32)
        lv = jnp.dot(jnp.concatenate([l_ak, m_rk], axis=0).astype(BF16), blockdiag(vb),
                     preferred_element_type=F32)
        w_in = ar[:C] + lv[:C]
        u_mat = _bdot(t, blockdiag(w_in.astype(BF16)))
        ub = u_mat.astype(BF16)
        y = ar[C:] + lv[C:] + _bdot(m_rb, blockdiag(ub))
        upd = lax.dot_general(jnp.concatenate([ub, vb], axis=0),
                              jnp.concatenate([b_end, k_end], axis=0).astype(BF16),
                              _TN, preferred_element_type=F32)
        state_ref[p] = state * jnp.exp(cum_last) + jnp.where(bdmask, upd, 0.0)

        mean = segsum(y) * (1.0 / hd)
        yc = y - mean
        var = segsum(yc * yc) * (1.0 / hd)
        yn = yc * lax.rsqrt(var + RWKV_LN_EPS) * lng_ref[:, sl] + lnb_ref[:, sl]
        bonus = segsum(r * k * rk_ref[:, sl])
        o_ref[0, :, sl] = ((yn + bonus * v) * g).astype(o_ref.dtype)

        pr_ref[:, sl] = r_raw[C - 1:C, :]
        pk_ref[:, sl] = k_raw[C - 1:C, :]
        pv_ref[:, sl] = v_raw[C - 1:C, :]
    plo_ref[...] = lo_raw[C - 1:C, :]


def _split2(x):
    hi = x.astype(BF16)
    return hi, (x - hi.astype(F32)).astype(BF16)


def _rwkv_staged_kernel(r_ref, k_ref, v_ref, lo_ref, mu_r_ref, mu_k_ref, mu_v_ref, mu_lo_ref,
                        w0_ref, wup_ref, a0_ref, aup_ref, gup_ref, kk_ref, ka_ref, rk_ref,
                        lng_ref, lnb_ref, o_ref,
                        state_ref, pr_ref, pk_ref, pv_ref, plo_ref):
    C = r_ref.shape[1]
    W = r_ref.shape[2]
    hd = RWKV_HEAD_DIM
    P = 2 * hd
    assert P == LANES and 2 * C == P
    pairs = range(W // P)
    sls = [slice(p * P, (p + 1) * P) for p in pairs]

    @pl.when(pl.program_id(1) == 0)
    def _():
        state_ref[...] = jnp.zeros_like(state_ref)
        pr_ref[...] = jnp.zeros_like(pr_ref)
        pk_ref[...] = jnp.zeros_like(pk_ref)
        pv_ref[...] = jnp.zeros_like(pv_ref)
        plo_ref[...] = jnp.zeros_like(plo_ref)

    def shift_mix(x, prev_ref, mu):
        first = lax.broadcasted_iota(jnp.int32, x.shape, 0) == 0
        xprev = jnp.where(first, prev_ref[...], pltpu.roll(x, 1, axis=0))
        prev_ref[...] = x[C - 1:C, :]
        return x + mu * (xprev - x)

    lo = shift_mix(lo_ref[0], plo_ref, mu_lo_ref[...])
    tw = jnp.tanh(lo[:, 0:LORA_W]).astype(BF16)
    za = lo[:, LORA_W:LORA_W + LORA_A].astype(BF16)
    sg = jax.nn.sigmoid(lo[:, LORA_W + LORA_A:]).astype(BF16)

    r = shift_mix(r_ref[0], pr_ref, mu_r_ref[...])
    k = shift_mix(k_ref[0], pk_ref, mu_k_ref[...])
    v = shift_mix(v_ref[0], pv_ref, mu_v_ref[...])
    u = -(w0_ref[...] + jnp.dot(tw, wup_ref[...], preferred_element_type=F32))
    softplus = jnp.maximum(u, 0.0) + jnp.log1p(jnp.exp(-jnp.abs(u)))
    logdec = -jnp.exp(-softplus - 0.5)
    a = jax.nn.sigmoid(a0_ref[...] + jnp.dot(za, aup_ref[...], preferred_element_type=F32))
    g = jnp.dot(sg, gup_ref[...], preferred_element_type=F32)
    kk_raw = k * kk_ref[...]
    k = k * (1.0 + (a - 1.0) * ka_ref[...])

    tri = jnp.where(lax.broadcasted_iota(jnp.int32, (C, C), 0)
                    >= lax.broadcasted_iota(jnp.int32, (C, C), 1), 1.0, 0.0).astype(BF16)
    ld_hi, ld_rest = logdec.astype(BF16), None
    ld_rest = logdec - ld_hi.astype(F32)
    ld_mid, ld_lo = _split2(ld_rest)
    cum = (jnp.dot(tri, ld_hi, preferred_element_type=F32)
           + jnp.dot(tri, ld_mid, preferred_element_type=F32)
           + jnp.dot(tri, ld_lo, preferred_element_type=F32))

    row = lax.broadcasted_iota(jnp.int32, (C, P), 0)
    lane = lax.broadcasted_iota(jnp.int32, (C, P), 1)
    head0 = lane < hd
    cj = jnp.where(head0, lane, lane - hd)
    strict = row > cj
    incl = row >= cj
    eye2 = jnp.where(row == cj, 1.0, 0.0)
    r2 = lax.broadcasted_iota(jnp.int32, (P, P), 0)
    l2 = lax.broadcasted_iota(jnp.int32, (P, P), 1)
    bdmask = (r2 < hd) == (l2 < hd)
    bd_ones = jnp.where(bdmask, 1.0, 0.0).astype(BF16)
    zero = jnp.zeros((), BF16)

    def blockdiag(x):
        return jnp.where(bdmask, jnp.concatenate([x, x], axis=0), jnp.zeros((), x.dtype))

    def segsum(x):
        hi, lo_ = _split2(x)
        s = jnp.dot(jnp.concatenate([hi, lo_], axis=0), bd_ones, preferred_element_type=F32)
        return s[:C] + s[C:]

    kk_sq = [segsum(jnp.square(kk_raw[:, sl])) for sl in sls]

    lhs, gram, b_end, k_end, dec_chunk = [], [], [], [], []
    for p, sl in zip(pairs, sls):
        kk = kk_raw[:, sl] / jnp.maximum(jnp.sqrt(kk_sq[p]), 1e-12)
        cum_p = cum[:, sl]
        cum_last = cum_p[C - 1:C, :]
        dec_to_end = jnp.exp(cum_last - cum_p)
        inv_p = jnp.exp(-cum_p)
        kka = kk * a[:, sl]
        rt = r[:, sl] * jnp.exp(cum_p)
        kt = (k[:, sl] * inv_p).astype(BF16)
        at = -kk * jnp.exp(cum_p - logdec[:, sl])
        bt = (kka * inv_p).astype(BF16)
        b_end.append(kka * dec_to_end)
        k_end.append(k[:, sl] * dec_to_end)
        dec_chunk.append(jnp.exp(cum_last))
        lhs.append(jnp.concatenate([at, rt], axis=0).astype(BF16))
        rhs = jnp.concatenate([jnp.where(head0, bt, zero), jnp.where(head0, zero, bt),
                               jnp.where(head0, kt, zero), jnp.where(head0, zero, kt)], axis=0)
        gram.append(lax.dot_general(lhs[p], rhs, _NT, preferred_element_type=F32))

    l_ab = [jnp.where(strict, gm[:C, :P], 0.0) for gm in gram]
    l_akrk = [jnp.concatenate([jnp.where(strict, gm[:C, P:], 0.0),
                               jnp.where(incl, gm[C:, P:], 0.0)], axis=0).astype(BF16) for gm in gram]
    m_rb = [jnp.where(incl, gm[C:, :P], 0.0).astype(BF16) for gm in gram]

    t = [eye2 + l for l in l_ab]
    m = [_bdot(l, blockdiag(l.astype(BF16))) for l in l_ab]
    n_sq = int(math.log2(C)) - 1
    for step in range(n_sq):
        for p in pairs:
            mb = m[p].astype(BF16)
            if step < n_sq - 1:
                both = jnp.concatenate([blockdiag(t[p].astype(BF16)), blockdiag(mb)], axis=1)
                prod = jnp.dot(mb, both, preferred_element_type=F32)
                t[p] = t[p] + prod[:, :P]
                m[p] = prod[:, P:]
            else:
                t[p] = t[p] + jnp.dot(mb, blockdiag(t[p].astype(BF16)), preferred_element_type=F32)

    vb = [v[:, sl].astype(BF16) for sl in sls]
    state = [state_ref[p] for p in pairs]
    ar = [lax.dot_general(lhs[p], state[p].astype(BF16), _NT, preferred_element_type=F32)
          for p in pairs]
    lv = [jnp.dot(l_akrk[p], blockdiag(vb[p]), preferred_element_type=F32) for p in pairs]
    ub = [_bdot(t[p], blockdiag((ar[p][:C] + lv[p][:C]).astype(BF16))).astype(BF16) for p in pairs]
    y = [ar[p][C:] + lv[p][C:] + jnp.dot(m_rb[p], blockdiag(ub[p]), preferred_element_type=F32)
         for p in pairs]
    for p in pairs:
        upd = lax.dot_general(jnp.concatenate([ub[p], vb[p]], axis=0),
                              jnp.concatenate([b_end[p], k_end[p]], axis=0).astype(BF16),
                              _TN, preferred_element_type=F32)
        state_ref[p] = state[p] * dec_chunk[p] + jnp.where(bdmask, upd, 0.0)

    mean = [segsum(y[p]) * (1.0 / hd) for p in pairs]
    yc = [y[p] - mean[p] for p in pairs]
    var = [segsum(jnp.square(yc[p])) * (1.0 / hd) for p in pairs]
    bonus = [segsum(r[:, sl] * k[:, sl] * rk_ref[:, sl]) for sl in sls]
    for p, sl in zip(pairs, sls):
        yn = yc[p] * lax.rsqrt(var[p] + RWKV_LN_EPS) * lng_ref[:, sl] + lnb_ref[:, sl]
        o_ref[0, :, sl] = ((yn + bonus[p] * v[:, sl]) * g[:, sl]).astype(o_ref.dtype)


def _rwkv(z, mu, w0, w_up, a0, a_up, g_up, k_k, k_a, r_k, ln_g, ln_b, *, col0):
    B, S, _ = z.shape
    W = w0.shape[-1]
    C = RWKV_CHUNK
    LO = LORA_W + LORA_A + LORA_G
    P = 2 * RWKV_HEAD_DIM
    c0 = col0 // W
    lo0 = (col0 + 3 * W) // LO
    zspec = lambda off: pl.BlockSpec((1, C, W), lambda b, n: (b, n, c0 + off))
    full = lambda a: pl.BlockSpec(a.shape, lambda b, n: (0,) * a.ndim)
    row = lambda a: a.reshape(1, -1)
    params = [row(mu[:W]), row(mu[W:2 * W]), row(mu[2 * W:3 * W]), row(mu[3 * W:]),
              row(w0), w_up.astype(BF16), row(a0), a_up.astype(BF16), g_up.astype(BF16),
              row(k_k), row(k_a), row(r_k), row(ln_g), row(ln_b)]
    return pl.pallas_call(
        _rwkv_staged_kernel,
        grid=(B, S // C),
        in_specs=[zspec(0), zspec(1), zspec(2),
                  pl.BlockSpec((1, C, LO), lambda b, n: (b, n, lo0))] + [full(a) for a in params],
        out_specs=pl.BlockSpec((1, C, W), lambda b, n: (b, n, 0)),
        out_shape=jax.ShapeDtypeStruct((B, S, W), BF16),
        scratch_shapes=[pltpu.VMEM((W // P, P, P), F32),
                        pltpu.VMEM((1, W), F32), pltpu.VMEM((1, W), F32),
                        pltpu.VMEM((1, W), F32), pltpu.VMEM((1, LO), F32)],
        compiler_params=_params(("parallel", "arbitrary")),
        name="rwkv7",
    )(z, z, z, z, *params)


def kernel(x, positions, norm1_g, w_in, ret_gn_g, ret_gn_b, rwkv_mu, rwkv_w0, rwkv_w_up, rwkv_a0, rwkv_a_up, rwkv_g_up, rwkv_k_k, rwkv_k_a, rwkv_r_k, rwkv_ln_g, rwkv_ln_b, w_branch_a, w_branch_b, w_out, norm2_g, mlp_up, mlp_down, final_g):
    B, S, D = x.shape
    M = B * S
    depth = w_in.shape[0]
    ret_w = ret_gn_g.shape[-1]
    rwkv_w = rwkv_w0.shape[-1]
    ret_cols = 4 * ret_w
    rwkv_cols = rwkv_mu.shape[-1]
    ga_col = ret_cols
    gb_col = ret_cols + D
    rwkv_col = ret_cols + 2 * D

    cos2, sin2 = _rope_tables(positions, tm=2048)
    cos2 = cos2.reshape(B, S, RET_HEAD_DIM)
    sin2 = sin2.reshape(B, S, RET_HEAD_DIM)

    xf = x.reshape(M, D)
    for l in range(depth):
        w = w_in[l]
        w_cat = jnp.concatenate([w[:, :ret_cols], w[:, ret_cols + rwkv_cols:],
                                 w[:, ret_cols:ret_cols + rwkv_cols]], axis=1).astype(BF16)
        z = _matmul(xf, w_cat, name="norm_in_proj", g=norm1_g[l], tm=512, tn=1280)
        z3 = z.reshape(B, S, -1)
        y_ret = _retention(z3, cos2, sin2, ret_gn_g[l], ret_gn_b[l], col0=0)
        y_rwkv = _rwkv(z3, rwkv_mu[l], rwkv_w0[l], rwkv_w_up[l], rwkv_a0[l], rwkv_a_up[l],
                       rwkv_g_up[l], rwkv_k_k[l], rwkv_k_a[l], rwkv_r_k[l],
                       rwkv_ln_g[l], rwkv_ln_b[l], col0=rwkv_col)
        merged = _merge(y_ret.reshape(M, ret_w), y_rwkv.reshape(M, rwkv_w),
                        w_branch_a[l].astype(BF16), w_branch_b[l].astype(BF16),
                        z, ga_col, gb_col, tm=512, tn=512)
        xf = _matmul(merged, w_out[l].astype(BF16), name="out_proj", residual=xf,
                     epilogue="residual", tm=512, tn=1024)
        h = _matmul(xf, mlp_up[l].astype(BF16), name="norm_mlp_up", g=norm2_g[l],
                    epilogue="relu2", out_dtype=BF16, tm=512, tn=1024)
        xf = _matmul(h, mlp_down[l].astype(BF16), name="mlp_down", residual=xf,
                     epilogue="residual", tm=512, tn=512)
    return _rmsnorm(xf, final_g, tm=512).reshape(B, S, D)
```

```python
import functools
import math

import jax
import jax.numpy as jnp
from jax import lax
from jax.experimental import pallas as pl
from jax.experimental.pallas import tpu as pltpu

F32 = jnp.float32
BF16 = jnp.bfloat16

RET_HEADS = 8
RET_HEAD_DIM = 128
RET_GN_EPS = 1e-5
ROPE_BASE = 10000.0
RWKV_HEAD_DIM = 64
RWKV_LN_EPS = 64e-5
NORM_EPS = 1e-6
LORA_W = 64
LORA_A = 64
LORA_G = 128

LANES = 128
V7X_VMEM_LIMIT_BYTES = 56 * 1024 * 1024

RET_CHUNK = 128
RWKV_CHUNK = 64

_NT = (((1,), (1,)), ((), ()))
_TN = (((0,), (0,)), ((), ()))


def _params(semantics):
    return pltpu.CompilerParams(dimension_semantics=semantics,
                                vmem_limit_bytes=V7X_VMEM_LIMIT_BYTES)


def _bdot(a, b):
    return jnp.dot(a.astype(BF16), b.astype(BF16), preferred_element_type=F32)


def _mm_kernel(*refs, norm, epilogue, sigmoid_tiles):
    if norm:
        x_ref, g_ref, w_ref = refs[:3]
        rest = refs[3:]
    else:
        x_ref, w_ref = refs[:2]
        rest = refs[2:]
    if epilogue == "residual":
        res_ref, o_ref = rest[:2]
        rest = rest[2:]
    else:
        o_ref = rest[0]
        rest = rest[1:]

    if norm:
        xn_ref = rest[0]

        @pl.when(pl.program_id(1) == 0)
        def _():
            x = x_ref[...]
            ms = jnp.mean(x * x, axis=-1, keepdims=True)
            xn_ref[...] = (x * lax.rsqrt(ms + NORM_EPS) * g_ref[...]).astype(BF16)

        lhs = xn_ref[...]
    else:
        lhs = x_ref[...]

    acc = jnp.dot(lhs, w_ref[...], preferred_element_type=F32)
    if epilogue == "residual":
        acc = acc + res_ref[...]
    if sigmoid_tiles:
        is_gate = pl.program_id(1) < sigmoid_tiles

        @pl.when(is_gate)
        def _():
            o_ref[...] = jax.nn.sigmoid(acc).astype(o_ref.dtype)

        @pl.when(jnp.logical_not(is_gate))
        def _():
            o_ref[...] = acc.astype(o_ref.dtype)
    else:
        o_ref[...] = acc.astype(o_ref.dtype)


def _matmul(x, w, *, name, g=None, residual=None, epilogue="none", sigmoid_cols=0,
            out_dtype=F32, tm, tn):
    M, K = x.shape
    N = w.shape[1]
    norm = g is not None
    sigmoid_tiles, rem = divmod(sigmoid_cols, tn)
    assert rem == 0
    in_specs = [pl.BlockSpec((tm, K), lambda i, j: (i, 0))]
    args = [x]
    if norm:
        in_specs.append(pl.BlockSpec((1, K), lambda i, j: (0, 0)))
        args.append(g.reshape(1, K))
    in_specs.append(pl.BlockSpec((K, tn), lambda i, j: (0, j)))
    args.append(w)
    if epilogue == "residual":
        in_specs.append(pl.BlockSpec((tm, tn), lambda i, j: (i, j)))
        args.append(residual)
    scratch = [pltpu.VMEM((tm, K), BF16)] if norm else []
    return pl.pallas_call(
        functools.partial(_mm_kernel, norm=norm, epilogue=epilogue,
                          sigmoid_tiles=sigmoid_tiles),
        grid=(M // tm, N // tn),
        in_specs=in_specs,
        out_specs=pl.BlockSpec((tm, tn), lambda i, j: (i, j)),
        out_shape=jax.ShapeDtypeStruct((M, N), out_dtype),
        scratch_shapes=scratch,
        compiler_params=_params(("parallel", "arbitrary")),
        name=name,
    )(*args)


def _mlp_kernel(*refs, final_norm):
    if final_norm:
        x_ref, g_ref, wu_ref, wd_ref, fg_ref, o_ref, xn_ref = refs
    else:
        x_ref, g_ref, wu_ref, wd_ref, o_ref, xn_ref = refs
    j = pl.program_id(1)

    @pl.when(j == 0)
    def _():
        x = x_ref[...]
        ms = jnp.mean(x * x, axis=-1, keepdims=True)
        xn_ref[...] = (x * lax.rsqrt(ms + NORM_EPS) * g_ref[...]).astype(BF16)
        o_ref[...] = x

    h = jnp.dot(xn_ref[...], wu_ref[...], preferred_element_type=F32)
    h = jnp.square(jnp.maximum(h, 0.0)).astype(BF16)
    o_ref[...] += jnp.dot(h, wd_ref[...], preferred_element_type=F32)

    if final_norm:
        @pl.when(j == pl.num_programs(1) - 1)
        def _():
            y = o_ref[...]
            ms = jnp.mean(y * y, axis=-1, keepdims=True)
            o_ref[...] = y * lax.rsqrt(ms + NORM_EPS) * fg_ref[...]


def _mlp(x, g, w_up, w_down, final_g=None, *, tm, tf):
    M, D = x.shape
    F = w_up.shape[1]
    final_norm = final_g is not None
    vec = pl.BlockSpec((1, D), lambda i, j: (0, 0))
    in_specs = [pl.BlockSpec((tm, D), lambda i, j: (i, 0)), vec,
                pl.BlockSpec((D, tf), lambda i, j: (0, j)),
                pl.BlockSpec((tf, D), lambda i, j: (j, 0))]
    args = [x, g.reshape(1, D), w_up, w_down]
    if final_norm:
        in_specs.append(vec)
        args.append(final_g.reshape(1, D))
    return pl.pallas_call(
        functools.partial(_mlp_kernel, final_norm=final_norm),
        grid=(M // tm, F // tf),
        in_specs=in_specs,
        out_specs=pl.BlockSpec((tm, D), lambda i, j: (i, 0)),
        out_shape=jax.ShapeDtypeStruct((M, D), F32),
        scratch_shapes=[pltpu.VMEM((tm, D), BF16)],
        compiler_params=_params(("parallel", "arbitrary")),
        name="mlp_final" if final_norm else "mlp",
    )(*args)


def _merge_kernel(ya_ref, yb_ref, wa_ref, wb_ref, ga_ref, gb_ref, o_ref):
    a = jnp.dot(ya_ref[...], wa_ref[...], preferred_element_type=F32)
    b = jnp.dot(yb_ref[...], wb_ref[...], preferred_element_type=F32)
    o_ref[...] = (ga_ref[...].astype(F32) * a + gb_ref[...].astype(F32) * b).astype(o_ref.dtype)


def _merge(ya, yb, wa, wb, z, ga_col, gb_col, *, tm, tn):
    M, K = ya.shape
    N = wa.shape[1]
    return pl.pallas_call(
        _merge_kernel,
        grid=(M // tm, N // tn),
        in_specs=[
            pl.BlockSpec((tm, K), lambda i, j: (i, 0)),
            pl.BlockSpec((tm, K), lambda i, j: (i, 0)),
            pl.BlockSpec((K, tn), lambda i, j: (0, j)),
            pl.BlockSpec((K, tn), lambda i, j: (0, j)),
            pl.BlockSpec((tm, tn), lambda i, j: (i, ga_col // tn + j)),
            pl.BlockSpec((tm, tn), lambda i, j: (i, gb_col // tn + j)),
        ],
        out_specs=pl.BlockSpec((tm, tn), lambda i, j: (i, j)),
        out_shape=jax.ShapeDtypeStruct((M, N), BF16),
        compiler_params=_params(("parallel", "arbitrary")),
        name="gated_merge",
    )(ya, yb, wa, wb, z, z)


def _rope_kernel(pos_ref, cos_ref, sin_ref):
    half = RET_HEAD_DIM // 2
    pos = pos_ref[...].astype(F32)
    lane = lax.broadcasted_iota(jnp.int32, (1, RET_HEAD_DIM), 1)
    j = jnp.where(lane < half, lane, lane - half).astype(F32)
    inv_freq = jnp.exp(j * (-math.log(ROPE_BASE) / half))
    ang = pos * inv_freq
    cos_ref[...] = jnp.cos(ang)
    sin_ref[...] = jnp.where(lane < half, -1.0, 1.0) * jnp.sin(ang)


def _rope_tables(positions, *, tm):
    M = positions.size
    pos = positions.reshape(M, 1)
    return pl.pallas_call(
        _rope_kernel,
        grid=(M // tm,),
        in_specs=[pl.BlockSpec((tm, 1), lambda i: (i, 0))],
        out_specs=[pl.BlockSpec((tm, RET_HEAD_DIM), lambda i: (i, 0))] * 2,
        out_shape=[jax.ShapeDtypeStruct((M, RET_HEAD_DIM), F32)] * 2,
        compiler_params=_params(("parallel",)),
        name="rope_tables",
    )(pos)


def _retention_kernel(q_ref, k_ref, v_ref, gr_ref, cos_ref, sin_ref, gng_ref, gnb_ref,
                      o_ref, state_ref):
    C = q_ref.shape[1]
    d = RET_HEAD_DIM

    @pl.when(pl.program_id(1) == 0)
    def _():
        state_ref[...] = jnp.zeros_like(state_ref)

    cos2 = cos_ref[0]
    sin2 = sin_ref[0]
    row = lax.broadcasted_iota(jnp.int32, (C, C), 0)
    col = lax.broadcasted_iota(jnp.int32, (C, C), 1)
    diff = (row - col).astype(F32)
    causal = row >= col
    rowd = lax.broadcasted_iota(jnp.int32, (C, d), 0).astype(F32)

    for h in range(RET_HEADS):
        sl = slice(h * d, (h + 1) * d)
        log_gamma = math.log1p(-(2.0 ** (-5.0 - h)))
        q = q_ref[0, :, sl].astype(F32)
        k = k_ref[0, :, sl].astype(F32)
        v = v_ref[0, :, sl].astype(BF16)
        q = q * cos2 + pltpu.roll(q, d // 2, axis=1) * sin2
        k = (k * cos2 + pltpu.roll(k, d // 2, axis=1) * sin2) * (d ** -0.5)
        inner_decay = jnp.where(causal, jnp.exp(jnp.maximum(diff, 0.0) * log_gamma), 0.0)
        query_decay = jnp.exp((rowd + 1.0) * log_gamma)
        key_decay = jnp.exp((C - 1.0 - rowd) * log_gamma)
        chunk_decay = math.exp(C * log_gamma)

        qb = q.astype(BF16)
        scores = lax.dot_general(qb, k.astype(BF16), _NT, preferred_element_type=F32)
        inner = _bdot(scores * inner_decay, v)
        state = state_ref[h]
        cross = _bdot(qb, state) * query_decay
        state_ref[h] = chunk_decay * state + lax.dot_general(
            (k * key_decay).astype(BF16), v, _TN, preferred_element_type=F32)

        y = inner + cross
        mean = jnp.mean(y, axis=-1, keepdims=True)
        yc = y - mean
        var = jnp.mean(yc * yc, axis=-1, keepdims=True)
        yn = yc * lax.rsqrt(var + RET_GN_EPS) * gng_ref[:, sl] + gnb_ref[:, sl]
        gr = gr_ref[0, :, sl].astype(F32)
        o_ref[0, :, sl] = (gr * jax.nn.sigmoid(gr) * yn).astype(o_ref.dtype)


def _retention(z, cos2, sin2, gn_g, gn_b, *, col0):
    B, S, _ = z.shape
    W = RET_HEADS * RET_HEAD_DIM
    C = RET_CHUNK
    c0 = col0 // W
    zspec = lambda off: pl.BlockSpec((1, C, W), lambda b, n: (b, n, c0 + off))
    rspec = pl.BlockSpec((1, C, RET_HEAD_DIM), lambda b, n: (b, n, 0))
    pspec = pl.BlockSpec((1, W), lambda b, n: (0, 0))
    return pl.pallas_call(
        _retention_kernel,
        grid=(B, S // C),
        in_specs=[zspec(0), zspec(1), zspec(2), zspec(3), rspec, rspec, pspec, pspec],
        out_specs=pl.BlockSpec((1, C, W), lambda b, n: (b, n, 0)),
        out_shape=jax.ShapeDtypeStruct((B, S, W), BF16),
        scratch_shapes=[pltpu.VMEM((RET_HEADS, RET_HEAD_DIM, RET_HEAD_DIM), F32)],
        compiler_params=_params(("parallel", "arbitrary")),
        name="retention",
    )(z, z, z, z, cos2, sin2, gn_g.reshape(1, W), gn_b.reshape(1, W))


def _split2(x):
    hi = x.astype(BF16)
    return hi, (x - hi.astype(F32)).astype(BF16)


def _rwkv_staged_kernel(r_ref, k_ref, v_ref, lo_ref, mu_r_ref, mu_k_ref, mu_v_ref, mu_lo_ref,
                        w0_ref, wup_ref, a0_ref, aup_ref, gup_ref, kk_ref, ka_ref, rk_ref,
                        lng_ref, lnb_ref, o_ref,
                        state_ref, pr_ref, pk_ref, pv_ref, plo_ref):
    C = r_ref.shape[1]
    W = r_ref.shape[2]
    hd = RWKV_HEAD_DIM
    P = 2 * hd
    assert P == LANES and 2 * C == P
    pairs = range(W // P)
    sls = [slice(p * P, (p + 1) * P) for p in pairs]

    @pl.when(pl.program_id(1) == 0)
    def _():
        state_ref[...] = jnp.zeros_like(state_ref)
        pr_ref[...] = jnp.zeros_like(pr_ref)
        pk_ref[...] = jnp.zeros_like(pk_ref)
        pv_ref[...] = jnp.zeros_like(pv_ref)
        plo_ref[...] = jnp.zeros_like(plo_ref)

    def shift_mix(x, prev_ref, mu):
        first = lax.broadcasted_iota(jnp.int32, x.shape, 0) == 0
        xprev = jnp.where(first, prev_ref[...], pltpu.roll(x, 1, axis=0))
        prev_ref[...] = x[C - 1:C, :]
        return x + mu * (xprev - x)

    lo = shift_mix(lo_ref[0], plo_ref, mu_lo_ref[...])
    tw = jnp.tanh(lo[:, 0:LORA_W]).astype(BF16)
    za = lo[:, LORA_W:LORA_W + LORA_A].astype(BF16)
    sg = jax.nn.sigmoid(lo[:, LORA_W + LORA_A:]).astype(BF16)

    r = shift_mix(r_ref[0], pr_ref, mu_r_ref[...])
    k = shift_mix(k_ref[0], pk_ref, mu_k_ref[...])
    v = shift_mix(v_ref[0], pv_ref, mu_v_ref[...])
    u = -(w0_ref[...] + jnp.dot(tw, wup_ref[...], preferred_element_type=F32))
    softplus = jnp.maximum(u, 0.0) + jnp.log1p(jnp.exp(-jnp.abs(u)))
    logdec = -jnp.exp(-softplus - 0.5)
    a = jax.nn.sigmoid(a0_ref[...] + jnp.dot(za, aup_ref[...], preferred_element_type=F32))
    g = jnp.dot(sg, gup_ref[...], preferred_element_type=F32)
    kk_raw = k * kk_ref[...]
    k = k * (1.0 + (a - 1.0) * ka_ref[...])

    tri = jnp.where(lax.broadcasted_iota(jnp.int32, (C, C), 0)
                    >= lax.broadcasted_iota(jnp.int32, (C, C), 1), 1.0, 0.0).astype(BF16)
    ld_hi = logdec.astype(BF16)
    ld_mid, ld_lo = _split2(logdec - ld_hi.astype(F32))
    cum = (jnp.dot(tri, ld_hi, preferred_element_type=F32)
           + jnp.dot(tri, ld_mid, preferred_element_type=F32)
           + jnp.dot(tri, ld_lo, preferred_element_type=F32))

    row = lax.broadcasted_iota(jnp.int32, (C, P), 0)
    lane = lax.broadcasted_iota(jnp.int32, (C, P), 1)
    head0 = lane < hd
    cj = jnp.where(head0, lane, lane - hd)
    strict = row > cj
    incl = row >= cj
    eye2 = jnp.where(row == cj, 1.0, 0.0)
    r2 = lax.broadcasted_iota(jnp.int32, (P, P), 0)
    l2 = lax.broadcasted_iota(jnp.int32, (P, P), 1)
    bdmask = (r2 < hd) == (l2 < hd)
    bd_ones = jnp.where(bdmask, 1.0, 0.0).astype(BF16)
    zero = jnp.zeros((), BF16)

    def blockdiag(x):
        return jnp.where(bdmask, jnp.concatenate([x, x], axis=0), jnp.zeros((), x.dtype))

    def segsum(x):
        hi, lo_ = _split2(x)
        s = jnp.dot(jnp.concatenate([hi, lo_], axis=0), bd_ones, preferred_element_type=F32)
        return s[:C] + s[C:]

    kk_sq = [segsum(jnp.square(kk_raw[:, sl])) for sl in sls]

    lhs, gram, b_end, k_end, dec_chunk = [], [], [], [], []
    for p, sl in zip(pairs, sls):
        kk = kk_raw[:, sl] / jnp.maximum(jnp.sqrt(kk_sq[p]), 1e-12)
        cum_p = cum[:, sl]
        cum_last = cum_p[C - 1:C, :]
        dec_to_end = jnp.exp(cum_last - cum_p)
        inv_p = jnp.exp(-cum_p)
        kka = kk * a[:, sl]
        rt = r[:, sl] * jnp.exp(cum_p)
        kt = (k[:, sl] * inv_p).astype(BF16)
        at = -kk * jnp.exp(cum_p - logdec[:, sl])
        bt = (kka * inv_p).astype(BF16)
        b_end.append(kka * dec_to_end)
        k_end.append(k[:, sl] * dec_to_end)
        dec_chunk.append(jnp.exp(cum_last))
        lhs.append(jnp.concatenate([at, rt], axis=0).astype(BF16))
        rhs = jnp.concatenate([jnp.where(head0, bt, zero), jnp.where(head0, zero, bt),
                               jnp.where(head0, kt, zero), jnp.where(head0, zero, kt)], axis=0)
        gram.append(lax.dot_general(lhs[p], rhs, _NT, preferred_element_type=F32))

    l_ab = [jnp.where(strict, gm[:C, :P], 0.0) for gm in gram]
    l_akrk = [jnp.concatenate([jnp.where(strict, gm[:C, P:], 0.0),
                               jnp.where(incl, gm[C:, P:], 0.0)], axis=0).astype(BF16) for gm in gram]
    m_rb = [jnp.where(incl, gm[C:, :P], 0.0).astype(BF16) for gm in gram]

    t = [eye2 + l for l in l_ab]
    m = [_bdot(l, blockdiag(l.astype(BF16))) for l in l_ab]
    n_sq = int(math.log2(C)) - 1
    for step in range(n_sq):
        for p in pairs:
            mb = m[p].astype(BF16)
            if step < n_sq - 1:
                both = jnp.concatenate([blockdiag(t[p].astype(BF16)), blockdiag(mb)], axis=1)
                prod = jnp.dot(mb, both, preferred_element_type=F32)
                t[p] = t[p] + prod[:, :P]
                m[p] = prod[:, P:]
            else:
                t[p] = t[p] + jnp.dot(mb, blockdiag(t[p].astype(BF16)), preferred_element_type=F32)

    vb = [v[:, sl].astype(BF16) for sl in sls]
    state = [state_ref[p] for p in pairs]
    ar = [lax.dot_general(lhs[p], state[p].astype(BF16), _NT, preferred_element_type=F32)
          for p in pairs]
    lv = [jnp.dot(l_akrk[p], blockdiag(vb[p]), preferred_element_type=F32) for p in pairs]
    ub = [_bdot(t[p], blockdiag((ar[p][:C] + lv[p][:C]).astype(BF16))).astype(BF16) for p in pairs]
    y = [ar[p][C:] + lv[p][C:] + jnp.dot(m_rb[p], blockdiag(ub[p]), preferred_element_type=F32)
         for p in pairs]
    for p in pairs:
        upd = lax.dot_general(jnp.concatenate([ub[p], vb[p]], axis=0),
                              jnp.concatenate([b_end[p], k_end[p]], axis=0).astype(BF16),
                              _TN, preferred_element_type=F32)
        state_ref[p] = state[p] * dec_chunk[p] + jnp.where(bdmask, upd, 0.0)

    mean = [segsum(y[p]) * (1.0 / hd) for p in pairs]
    yc = [y[p] - mean[p] for p in pairs]
    var = [segsum(jnp.square(yc[p])) * (1.0 / hd) for p in pairs]
    bonus = [segsum(r[:, sl] * k[:, sl] * rk_ref[:, sl]) for sl in sls]
    for p, sl in zip(pairs, sls):
        yn = yc[p] * lax.rsqrt(var[p] + RWKV_LN_EPS) * lng_ref[:, sl] + lnb_ref[:, sl]
        o_ref[0, :, sl] = ((yn + bonus[p] * v[:, sl]) * g[:, sl]).astype(o_ref.dtype)


def _rwkv(z, mu, w0, w_up, a0, a_up, g_up, k_k, k_a, r_k, ln_g, ln_b, *, col0):
    B, S, _ = z.shape
    W = w0.shape[-1]
    C = RWKV_CHUNK
    LO = LORA_W + LORA_A + LORA_G
    P = 2 * RWKV_HEAD_DIM
    c0 = col0 // W
    lo0 = (col0 + 3 * W) // LO
    zspec = lambda off: pl.BlockSpec((1, C, W), lambda b, n: (b, n, c0 + off))
    full = lambda a: pl.BlockSpec(a.shape, lambda b, n: (0,) * a.ndim)
    row = lambda a: a.reshape(1, -1)
    params = [row(mu[:W]), row(mu[W:2 * W]), row(mu[2 * W:3 * W]), row(mu[3 * W:]),
              row(w0), w_up.astype(BF16), row(a0), a_up.astype(BF16), g_up.astype(BF16),
              row(k_k), row(k_a), row(r_k), row(ln_g), row(ln_b)]
    return pl.pallas_call(
        _rwkv_staged_kernel,
        grid=(B, S // C),
        in_specs=[zspec(0), zspec(1), zspec(2),
                  pl.BlockSpec((1, C, LO), lambda b, n: (b, n, lo0))] + [full(a) for a in params],
        out_specs=pl.BlockSpec((1, C, W), lambda b, n: (b, n, 0)),
        out_shape=jax.ShapeDtypeStruct((B, S, W), BF16),
        scratch_shapes=[pltpu.VMEM((W // P, P, P), F32),
                        pltpu.VMEM((1, W), F32), pltpu.VMEM((1, W), F32),
                        pltpu.VMEM((1, W), F32), pltpu.VMEM((1, LO), F32)],
        compiler_params=_params(("parallel", "arbitrary")),
        name="rwkv7",
    )(z, z, z, z, *params)


def kernel(x, positions, norm1_g, w_in, ret_gn_g, ret_gn_b, rwkv_mu, rwkv_w0, rwkv_w_up, rwkv_a0, rwkv_a_up, rwkv_g_up, rwkv_k_k, rwkv_k_a, rwkv_r_k, rwkv_ln_g, rwkv_ln_b, w_branch_a, w_branch_b, w_out, norm2_g, mlp_up, mlp_down, final_g):
    B, S, D = x.shape
    M = B * S
    depth = w_in.shape[0]
    ret_w = ret_gn_g.shape[-1]
    rwkv_w = rwkv_w0.shape[-1]
    ret_cols = 4 * ret_w
    rwkv_cols = rwkv_mu.shape[-1]
    cos2, sin2 = _rope_tables(positions, tm=2048)
    cos2 = cos2.reshape(B, S, RET_HEAD_DIM)
    sin2 = sin2.reshape(B, S, RET_HEAD_DIM)

    xf = x.reshape(M, D)
    for l in range(depth):
        w = w_in[l]
        w_a = jnp.concatenate([w[:, ret_cols + rwkv_cols:], w[:, :ret_cols]], axis=1).astype(BF16)
        w_b = w[:, ret_cols:ret_cols + rwkv_cols].astype(BF16)
        z_a = _matmul(xf, w_a, name="norm_in_proj_a", g=norm1_g[l], sigmoid_cols=2 * D,
                      out_dtype=BF16, tm=1024, tn=1024)
        z_b = _matmul(xf, w_b, name="norm_in_proj_b", g=norm1_g[l], tm=1024, tn=rwkv_cols // 2)
        y_ret = _retention(z_a.reshape(B, S, -1), cos2, sin2, ret_gn_g[l], ret_gn_b[l],
                           col0=2 * D)
        y_rwkv = _rwkv(z_b.reshape(B, S, -1), rwkv_mu[l], rwkv_w0[l], rwkv_w_up[l], rwkv_a0[l],
                       rwkv_a_up[l], rwkv_g_up[l], rwkv_k_k[l], rwkv_k_a[l], rwkv_r_k[l],
                       rwkv_ln_g[l], rwkv_ln_b[l], col0=0)
        merged = _merge(y_ret.reshape(M, ret_w), y_rwkv.reshape(M, rwkv_w),
                        w_branch_a[l].astype(BF16), w_branch_b[l].astype(BF16),
                        z_a, 0, D, tm=1024, tn=512)
        xf = _matmul(merged, w_out[l].astype(BF16), name="out_proj", residual=xf,
                     epilogue="residual", tm=1024, tn=1024)
        xf = _mlp(xf, norm2_g[l], mlp_up[l].astype(BF16), mlp_down[l].astype(BF16),
                  final_g if l == depth - 1 else None, tm=1024, tf=512)
    return xf.reshape(B, S, D)
```

```python
import functools
import math

import jax
import jax.numpy as jnp
from jax import lax
from jax.experimental import pallas as pl
from jax.experimental.pallas import tpu as pltpu

F32 = jnp.float32
BF16 = jnp.bfloat16

RET_HEADS = 8
RET_HEAD_DIM = 128
RET_GN_EPS = 1e-5
ROPE_BASE = 10000.0
RWKV_HEAD_DIM = 64
RWKV_LN_EPS = 64e-5
NORM_EPS = 1e-6
LORA_W = 64
LORA_A = 64
LORA_G = 128

LANES = 128
V7X_VMEM_LIMIT_BYTES = 56 * 1024 * 1024

RET_CHUNK = 128
RWKV_CHUNK = 64
RWKV_ROWS = 256

_NT = (((1,), (1,)), ((), ()))
_TN = (((0,), (0,)), ((), ()))


def _params(semantics):
    return pltpu.CompilerParams(dimension_semantics=semantics,
                                vmem_limit_bytes=V7X_VMEM_LIMIT_BYTES)


def _bdot(a, b):
    return jnp.dot(a.astype(BF16), b.astype(BF16), preferred_element_type=F32)


def _mm_kernel(*refs, norm, epilogue):
    if norm:
        x_ref, g_ref, w_ref = refs[:3]
        rest = refs[3:]
    else:
        x_ref, w_ref = refs[:2]
        rest = refs[2:]
    if epilogue == "residual":
        res_ref, o_ref = rest[:2]
        rest = rest[2:]
    else:
        o_ref = rest[0]
        rest = rest[1:]

    if norm:
        xn_ref = rest[0]

        @pl.when(pl.program_id(1) == 0)
        def _():
            x = x_ref[...]
            ms = jnp.mean(x * x, axis=-1, keepdims=True)
            xn_ref[...] = (x * lax.rsqrt(ms + NORM_EPS) * g_ref[...]).astype(BF16)

        lhs = xn_ref[...]
    else:
        lhs = x_ref[...]

    acc = jnp.dot(lhs, w_ref[...], preferred_element_type=F32)
    if epilogue == "residual":
        acc = acc + res_ref[...]
    elif epilogue == "sigmoid":
        acc = jax.nn.sigmoid(acc)
    o_ref[...] = acc.astype(o_ref.dtype)


def _matmul(x, w, *, name, g=None, residual=None, epilogue="none", out_dtype=F32, tm, tn):
    M, K = x.shape
    N = w.shape[1]
    norm = g is not None
    in_specs = [pl.BlockSpec((tm, K), lambda i, j: (i, 0))]
    args = [x]
    if norm:
        in_specs.append(pl.BlockSpec((1, K), lambda i, j: (0, 0)))
        args.append(g.reshape(1, K))
    in_specs.append(pl.BlockSpec((K, tn), lambda i, j: (0, j)))
    args.append(w)
    if epilogue == "residual":
        in_specs.append(pl.BlockSpec((tm, tn), lambda i, j: (i, j)))
        args.append(residual)
    scratch = [pltpu.VMEM((tm, K), BF16)] if norm else []
    return pl.pallas_call(
        functools.partial(_mm_kernel, norm=norm, epilogue=epilogue),
        grid=(M // tm, N // tn),
        in_specs=in_specs,
        out_specs=pl.BlockSpec((tm, tn), lambda i, j: (i, j)),
        out_shape=jax.ShapeDtypeStruct((M, N), out_dtype),
        scratch_shapes=scratch,
        compiler_params=_params(("parallel", "arbitrary")),
        name=name,
    )(*args)


def _mlp_kernel(*refs, final_norm):
    if final_norm:
        x_ref, g_ref, wu_ref, wd_ref, fg_ref, o_ref, xn_ref = refs
    else:
        x_ref, g_ref, wu_ref, wd_ref, o_ref, xn_ref = refs
    j = pl.program_id(1)

    @pl.when(j == 0)
    def _():
        x = x_ref[...]
        ms = jnp.mean(x * x, axis=-1, keepdims=True)
        xn_ref[...] = (x * lax.rsqrt(ms + NORM_EPS) * g_ref[...]).astype(BF16)
        o_ref[...] = x

    h = jnp.dot(xn_ref[...], wu_ref[...], preferred_element_type=F32)
    h = jnp.square(jnp.maximum(h, 0.0)).astype(BF16)
    o_ref[...] += jnp.dot(h, wd_ref[...], preferred_element_type=F32)

    if final_norm:
        @pl.when(j == pl.num_programs(1) - 1)
        def _():
            y = o_ref[...]
            ms = jnp.mean(y * y, axis=-1, keepdims=True)
            o_ref[...] = y * lax.rsqrt(ms + NORM_EPS) * fg_ref[...]


def _mlp(x, g, w_up, w_down, final_g=None, *, tm, tf):
    M, D = x.shape
    F = w_up.shape[1]
    final_norm = final_g is not None
    vec = pl.BlockSpec((1, D), lambda i, j: (0, 0))
    in_specs = [pl.BlockSpec((tm, D), lambda i, j: (i, 0)), vec,
                pl.BlockSpec((D, tf), lambda i, j: (0, j)),
                pl.BlockSpec((tf, D), lambda i, j: (j, 0))]
    args = [x, g.reshape(1, D), w_up, w_down]
    if final_norm:
        in_specs.append(vec)
        args.append(final_g.reshape(1, D))
    return pl.pallas_call(
        functools.partial(_mlp_kernel, final_norm=final_norm),
        grid=(M // tm, F // tf),
        in_specs=in_specs,
        out_specs=pl.BlockSpec((tm, D), lambda i, j: (i, 0)),
        out_shape=jax.ShapeDtypeStruct((M, D), F32),
        scratch_shapes=[pltpu.VMEM((tm, D), BF16)],
        compiler_params=_params(("parallel", "arbitrary")),
        name="mlp_final" if final_norm else "mlp",
    )(*args)


def _merge_kernel(ya_ref, yb_ref, wa_ref, wb_ref, ga_ref, gb_ref, o_ref):
    a = jnp.dot(ya_ref[...], wa_ref[...], preferred_element_type=F32)
    b = jnp.dot(yb_ref[...], wb_ref[...], preferred_element_type=F32)
    o_ref[...] = (ga_ref[...].astype(F32) * a + gb_ref[...].astype(F32) * b).astype(o_ref.dtype)


def _merge(ya, yb, wa, wb, z, ga_col, gb_col, *, tm, tn):
    M, K = ya.shape
    N = wa.shape[1]
    return pl.pallas_call(
        _merge_kernel,
        grid=(M // tm, N // tn),
        in_specs=[
            pl.BlockSpec((tm, K), lambda i, j: (i, 0)),
            pl.BlockSpec((tm, K), lambda i, j: (i, 0)),
            pl.BlockSpec((K, tn), lambda i, j: (0, j)),
            pl.BlockSpec((K, tn), lambda i, j: (0, j)),
            pl.BlockSpec((tm, tn), lambda i, j: (i, ga_col // tn + j)),
            pl.BlockSpec((tm, tn), lambda i, j: (i, gb_col // tn + j)),
        ],
        out_specs=pl.BlockSpec((tm, tn), lambda i, j: (i, j)),
        out_shape=jax.ShapeDtypeStruct((M, N), BF16),
        compiler_params=_params(("parallel", "arbitrary")),
        name="gated_merge",
    )(ya, yb, wa, wb, z, z)


def _rope_kernel(pos_ref, cos_ref, sin_ref):
    half = RET_HEAD_DIM // 2
    pos = pos_ref[...].astype(F32)
    lane = lax.broadcasted_iota(jnp.int32, (1, RET_HEAD_DIM), 1)
    j = jnp.where(lane < half, lane, lane - half).astype(F32)
    inv_freq = jnp.exp(j * (-math.log(ROPE_BASE) / half))
    ang = pos * inv_freq
    cos_ref[...] = jnp.cos(ang)
    sin_ref[...] = jnp.where(lane < half, -1.0, 1.0) * jnp.sin(ang)


def _rope_tables(positions, *, tm):
    M = positions.size
    pos = positions.reshape(M, 1)
    return pl.pallas_call(
        _rope_kernel,
        grid=(M // tm,),
        in_specs=[pl.BlockSpec((tm, 1), lambda i: (i, 0))],
        out_specs=[pl.BlockSpec((tm, RET_HEAD_DIM), lambda i: (i, 0))] * 2,
        out_shape=[jax.ShapeDtypeStruct((M, RET_HEAD_DIM), F32)] * 2,
        compiler_params=_params(("parallel",)),
        name="rope_tables",
    )(pos)


def _retention_kernel(q_ref, k_ref, v_ref, gr_ref, cos_ref, sin_ref, gng_ref, gnb_ref,
                      o_ref, state_ref):
    C = q_ref.shape[1]
    d = RET_HEAD_DIM

    @pl.when(pl.program_id(1) == 0)
    def _():
        state_ref[...] = jnp.zeros_like(state_ref)

    cos2 = cos_ref[0]
    sin2 = sin_ref[0]
    row = lax.broadcasted_iota(jnp.int32, (C, C), 0)
    col = lax.broadcasted_iota(jnp.int32, (C, C), 1)
    diff = (row - col).astype(F32)
    causal = row >= col
    rowd = lax.broadcasted_iota(jnp.int32, (C, d), 0).astype(F32)

    for h in range(RET_HEADS):
        sl = slice(h * d, (h + 1) * d)
        log_gamma = math.log1p(-(2.0 ** (-5.0 - h)))
        q = q_ref[0, :, sl].astype(F32)
        k = k_ref[0, :, sl].astype(F32)
        v = v_ref[0, :, sl].astype(BF16)
        q = q * cos2 + pltpu.roll(q, d // 2, axis=1) * sin2
        k = (k * cos2 + pltpu.roll(k, d // 2, axis=1) * sin2) * (d ** -0.5)
        inner_decay = jnp.where(causal, jnp.exp(jnp.maximum(diff, 0.0) * log_gamma), 0.0)
        query_decay = jnp.exp((rowd + 1.0) * log_gamma)
        key_decay = jnp.exp((C - 1.0 - rowd) * log_gamma)
        chunk_decay = math.exp(C * log_gamma)

        qb = q.astype(BF16)
        scores = lax.dot_general(qb, k.astype(BF16), _NT, preferred_element_type=F32)
        inner = _bdot(scores * inner_decay, v)
        state = state_ref[h]
        cross = _bdot(qb, state) * query_decay
        state_ref[h] = chunk_decay * state + lax.dot_general(
            (k * key_decay).astype(BF16), v, _TN, preferred_element_type=F32)

        y = inner + cross
        mean = jnp.mean(y, axis=-1, keepdims=True)
        yc = y - mean
        var = jnp.mean(yc * yc, axis=-1, keepdims=True)
        yn = yc * lax.rsqrt(var + RET_GN_EPS) * gng_ref[:, sl] + gnb_ref[:, sl]
        gr = gr_ref[0, :, sl].astype(F32)
        o_ref[0, :, sl] = (gr * jax.nn.sigmoid(gr) * yn).astype(o_ref.dtype)


def _retention(z, cos2, sin2, gn_g, gn_b, *, col0):
    B, S, _ = z.shape
    W = RET_HEADS * RET_HEAD_DIM
    C = RET_CHUNK
    c0 = col0 // W
    zspec = lambda off: pl.BlockSpec((1, C, W), lambda b, n: (b, n, c0 + off))
    rspec = pl.BlockSpec((1, C, RET_HEAD_DIM), lambda b, n: (b, n, 0))
    pspec = pl.BlockSpec((1, W), lambda b, n: (0, 0))
    return pl.pallas_call(
        _retention_kernel,
        grid=(B, S // C),
        in_specs=[zspec(0), zspec(1), zspec(2), zspec(3), rspec, rspec, pspec, pspec],
        out_specs=pl.BlockSpec((1, C, W), lambda b, n: (b, n, 0)),
        out_shape=jax.ShapeDtypeStruct((B, S, W), BF16),
        scratch_shapes=[pltpu.VMEM((RET_HEADS, RET_HEAD_DIM, RET_HEAD_DIM), F32)],
        compiler_params=_params(("parallel", "arbitrary")),
        name="retention",
    )(z, z, z, z, cos2, sin2, gn_g.reshape(1, W), gn_b.reshape(1, W))


def _split2(x):
    hi = x.astype(BF16)
    return hi, (x - hi.astype(F32)).astype(BF16)


def _rwkv_staged_kernel(r_ref, k_ref, v_ref, lo_ref, mu_r_ref, mu_k_ref, mu_v_ref, mu_lo_ref,
                        w0_ref, wup_ref, a0_ref, aup_ref, gup_ref, kk_ref, ka_ref, rk_ref,
                        lng_ref, lnb_ref, o_ref,
                        state_ref, pr_ref, pk_ref, pv_ref, plo_ref):
    R = r_ref.shape[1]
    W = r_ref.shape[2]
    C = RWKV_CHUNK
    hd = RWKV_HEAD_DIM
    P = 2 * hd
    assert P == LANES and 2 * C == P and R % C == 0
    chunks = range(R // C)
    rows = [slice(c * C, (c + 1) * C) for c in chunks]
    pairs = range(W // P)
    sls = [slice(p * P, (p + 1) * P) for p in pairs]

    @pl.when(pl.program_id(1) == 0)
    def _():
        state_ref[...] = jnp.zeros_like(state_ref)
        pr_ref[...] = jnp.zeros_like(pr_ref)
        pk_ref[...] = jnp.zeros_like(pk_ref)
        pv_ref[...] = jnp.zeros_like(pv_ref)
        plo_ref[...] = jnp.zeros_like(plo_ref)

    def shift_mix(x, prev_ref, mu):
        first = lax.broadcasted_iota(jnp.int32, x.shape, 0) == 0
        xprev = jnp.where(first, prev_ref[...], pltpu.roll(x, 1, axis=0))
        prev_ref[...] = x[R - 1:R, :]
        return x + mu * (xprev - x)

    lo = shift_mix(lo_ref[0], plo_ref, mu_lo_ref[...])
    tw = jnp.tanh(lo[:, 0:LORA_W]).astype(BF16)
    za = lo[:, LORA_W:LORA_W + LORA_A].astype(BF16)
    sg = jax.nn.sigmoid(lo[:, LORA_W + LORA_A:]).astype(BF16)

    r = shift_mix(r_ref[0], pr_ref, mu_r_ref[...])
    k = shift_mix(k_ref[0], pk_ref, mu_k_ref[...])
    v = shift_mix(v_ref[0], pv_ref, mu_v_ref[...])
    logdec = -math.exp(-0.5) * jax.nn.sigmoid(
        w0_ref[...] + jnp.dot(tw, wup_ref[...], preferred_element_type=F32))
    a = jax.nn.sigmoid(a0_ref[...] + jnp.dot(za, aup_ref[...], preferred_element_type=F32))
    g = jnp.dot(sg, gup_ref[...], preferred_element_type=F32)
    kk_raw = k * kk_ref[...]
    k = k * (1.0 + (a - 1.0) * ka_ref[...])

    tri = jnp.where(lax.broadcasted_iota(jnp.int32, (C, C), 0)
                    >= lax.broadcasted_iota(jnp.int32, (C, C), 1), 1.0, 0.0).astype(BF16)
    ld_hi = logdec.astype(BF16)
    ld_mid, ld_lo = _split2(logdec - ld_hi.astype(F32))
    cum = [jnp.dot(tri, ld_hi[rc], preferred_element_type=F32)
           + jnp.dot(tri, ld_mid[rc], preferred_element_type=F32)
           + jnp.dot(tri, ld_lo[rc], preferred_element_type=F32) for rc in rows]

    row = lax.broadcasted_iota(jnp.int32, (C, P), 0)
    lane = lax.broadcasted_iota(jnp.int32, (C, P), 1)
    head0 = lane < hd
    cj = jnp.where(head0, lane, lane - hd)
    strict = row > cj
    incl = row >= cj
    eye2 = jnp.where(row == cj, 1.0, 0.0)
    r2 = lax.broadcasted_iota(jnp.int32, (P, P), 0)
    l2 = lax.broadcasted_iota(jnp.int32, (P, P), 1)
    bdmask = (r2 < hd) == (l2 < hd)
    zero = jnp.zeros((), BF16)

    def blockdiag(x):
        return jnp.where(bdmask, jnp.concatenate([x, x], axis=0), jnp.zeros((), x.dtype))

    def segsum(x):
        h0 = lax.broadcasted_iota(jnp.int32, x.shape, 1) < hd
        s0 = jnp.sum(jnp.where(h0, x, 0.0), axis=-1, keepdims=True)
        s1 = jnp.sum(jnp.where(h0, 0.0, x), axis=-1, keepdims=True)
        return jnp.where(h0, s0, s1)

    kk_sq = [segsum(jnp.square(kk_raw[:, sl])) for sl in sls]
    bonus = [segsum(r[:, sl] * k[:, sl] * rk_ref[:, sl]) for sl in sls]

    keys = [(c, p) for c in chunks for p in pairs]
    lhs, gram, bk_end, dec_chunk, vb = {}, {}, {}, {}, {}
    for q in keys:
        rc, sl = rows[q[0]], sls[q[1]]
        kk = kk_raw[rc, sl] * lax.rsqrt(jnp.maximum(kk_sq[q[1]][rc], 1e-24))
        cum_p = cum[q[0]][:, sl]
        cum_last = cum_p[C - 1:C, :]
        dec_to_end = jnp.exp(cum_last - cum_p)
        inv_p = jnp.exp(-cum_p)
        kka = kk * a[rc, sl]
        rt = r[rc, sl] * jnp.exp(cum_p)
        kt = (k[rc, sl] * inv_p).astype(BF16)
        at = -kk * jnp.exp(cum_p - logdec[rc, sl])
        bt = (kka * inv_p).astype(BF16)
        bk_end[q] = jnp.concatenate([kka * dec_to_end, k[rc, sl] * dec_to_end],
                                    axis=0).astype(BF16)
        dec_chunk[q] = jnp.exp(cum_last)
        vb[q] = v[rc, sl].astype(BF16)
        lhs[q] = jnp.concatenate([at, rt], axis=0).astype(BF16)
        rhs = jnp.concatenate([jnp.where(head0, bt, zero), jnp.where(head0, zero, bt),
                               jnp.where(head0, kt, zero), jnp.where(head0, zero, kt)], axis=0)
        gram[q] = lax.dot_general(lhs[q], rhs, _NT, preferred_element_type=F32)

    l_ab = {q: jnp.where(strict, gram[q][:C, :P], 0.0) for q in keys}
    lv = {q: jnp.dot(jnp.concatenate([jnp.where(strict, gram[q][:C, P:], 0.0),
                                      jnp.where(incl, gram[q][C:, P:], 0.0)], axis=0).astype(BF16),
                     blockdiag(vb[q]), preferred_element_type=F32) for q in keys}
    m_rb = {q: jnp.where(incl, gram[q][C:, :P], 0.0).astype(BF16) for q in keys}

    t = {q: eye2 + l_ab[q] for q in keys}
    m = {q: _bdot(l_ab[q], blockdiag(l_ab[q].astype(BF16))) for q in keys}
    n_sq = int(math.log2(C)) - 1
    for step in range(n_sq):
        for q in keys:
            mb = m[q].astype(BF16)
            if step < n_sq - 1:
                both = jnp.concatenate([blockdiag(t[q].astype(BF16)), blockdiag(mb)], axis=1)
                prod = jnp.dot(mb, both, preferred_element_type=F32)
                t[q] = t[q] + prod[:, :P]
                m[q] = prod[:, P:]
            else:
                t[q] = t[q] + jnp.dot(mb, blockdiag(t[q].astype(BF16)), preferred_element_type=F32)
    tb = {q: t[q].astype(BF16) for q in keys}

    state = [state_ref[p] for p in pairs]
    for c, rc in zip(chunks, rows):
        ar = [lax.dot_general(lhs[c, p], state[p].astype(BF16), _NT, preferred_element_type=F32)
              for p in pairs]
        ub = [jnp.dot(tb[c, p], blockdiag((ar[p][:C] + lv[c, p][:C]).astype(BF16)),
                      preferred_element_type=F32).astype(BF16) for p in pairs]
        y = [ar[p][C:] + lv[c, p][C:]
             + jnp.dot(m_rb[c, p], blockdiag(ub[p]), preferred_element_type=F32) for p in pairs]
        for p in pairs:
            upd = lax.dot_general(jnp.concatenate([ub[p], vb[c, p]], axis=0), bk_end[c, p],
                                  _TN, preferred_element_type=F32)
            state[p] = state[p] * dec_chunk[c, p] + jnp.where(bdmask, upd, 0.0)

        yc = [y[p] - segsum(y[p]) * (1.0 / hd) for p in pairs]
        var = [segsum(jnp.square(yc[p])) * (1.0 / hd) for p in pairs]
        for p, sl in zip(pairs, sls):
            yn = yc[p] * lax.rsqrt(var[p] + RWKV_LN_EPS) * lng_ref[:, sl] + lnb_ref[:, sl]
            o_ref[0, rc, sl] = ((yn + bonus[p][rc] * v[rc, sl]) * g[rc, sl]).astype(o_ref.dtype)
    for p in pairs:
        state_ref[p] = state[p]


def _rwkv(z, mu, w0, w_up, a0, a_up, g_up, k_k, k_a, r_k, ln_g, ln_b, *, col0):
    B, S, _ = z.shape
    W = w0.shape[-1]
    C = RWKV_ROWS
    LO = LORA_W + LORA_A + LORA_G
    P = 2 * RWKV_HEAD_DIM
    c0 = col0 // W
    lo0 = (col0 + 3 * W) // LO
    zspec = lambda off: pl.BlockSpec((1, C, W), lambda b, n: (b, n, c0 + off))
    full = lambda a: pl.BlockSpec(a.shape, lambda b, n: (0,) * a.ndim)
    row = lambda a: a.reshape(1, -1)
    params = [row(mu[:W]), row(mu[W:2 * W]), row(mu[2 * W:3 * W]), row(mu[3 * W:]),
              row(w0), w_up.astype(BF16), row(a0), a_up.astype(BF16), g_up.astype(BF16),
              row(k_k), row(k_a), row(r_k), row(ln_g), row(ln_b)]
    return pl.pallas_call(
        _rwkv_staged_kernel,
        grid=(B, S // C),
        in_specs=[zspec(0), zspec(1), zspec(2),
                  pl.BlockSpec((1, C, LO), lambda b, n: (b, n, lo0))] + [full(a) for a in params],
        out_specs=pl.BlockSpec((1, C, W), lambda b, n: (b, n, 0)),
        out_shape=jax.ShapeDtypeStruct((B, S, W), BF16),
        scratch_shapes=[pltpu.VMEM((W // P, P, P), F32),
                        pltpu.VMEM((1, W), F32), pltpu.VMEM((1, W), F32),
                        pltpu.VMEM((1, W), F32), pltpu.VMEM((1, LO), F32)],
        compiler_params=_params(("parallel", "arbitrary")),
        name="rwkv7",
    )(z, z, z, z, *params)


def kernel(x, positions, norm1_g, w_in, ret_gn_g, ret_gn_b, rwkv_mu, rwkv_w0, rwkv_w_up, rwkv_a0, rwkv_a_up, rwkv_g_up, rwkv_k_k, rwkv_k_a, rwkv_r_k, rwkv_ln_g, rwkv_ln_b, w_branch_a, w_branch_b, w_out, norm2_g, mlp_up, mlp_down, final_g):
    B, S, D = x.shape
    M = B * S
    depth = w_in.shape[0]
    ret_w = ret_gn_g.shape[-1]
    rwkv_w = rwkv_w0.shape[-1]
    ret_cols = 4 * ret_w
    rwkv_cols = rwkv_mu.shape[-1]
    cos2, sin2 = _rope_tables(positions, tm=2048)
    cos2 = cos2.reshape(B, S, RET_HEAD_DIM)
    sin2 = sin2.reshape(B, S, RET_HEAD_DIM)

    xf = x.reshape(M, D)
    for l in range(depth):
        w = w_in[l]
        z_ret = _matmul(xf, w[:, :ret_cols].astype(BF16), name="norm_in_proj_ret",
                        g=norm1_g[l], out_dtype=BF16, tm=1024, tn=2048)
        z_gate = _matmul(xf, w[:, ret_cols + rwkv_cols:].astype(BF16), name="norm_in_proj_gate",
                         g=norm1_g[l], epilogue="sigmoid", out_dtype=BF16, tm=1024, tn=2048)
        z_rwkv = _matmul(xf, w[:, ret_cols:ret_cols + rwkv_cols].astype(BF16),
                         name="norm_in_proj_rwkv", g=norm1_g[l], tm=1024, tn=rwkv_cols // 2)
        y_ret = _retention(z_ret.reshape(B, S, -1), cos2, sin2, ret_gn_g[l], ret_gn_b[l], col0=0)
        y_rwkv = _rwkv(z_rwkv.reshape(B, S, -1), rwkv_mu[l], rwkv_w0[l], rwkv_w_up[l],
                       rwkv_a0[l], rwkv_a_up[l], rwkv_g_up[l], rwkv_k_k[l], rwkv_k_a[l],
                       rwkv_r_k[l], rwkv_ln_g[l], rwkv_ln_b[l], col0=0)
        merged = _merge(y_ret.reshape(M, ret_w), y_rwkv.reshape(M, rwkv_w),
                        w_branch_a[l].astype(BF16), w_branch_b[l].astype(BF16),
                        z_gate, 0, D, tm=1024, tn=512)
        xf = _matmul(merged, w_out[l].astype(BF16), name="out_proj", residual=xf,
                     epilogue="residual", tm=1024, tn=1024)
        xf = _mlp(xf, norm2_g[l], mlp_up[l].astype(BF16), mlp_down[l].astype(BF16),
                  final_g if l == depth - 1 else None, tm=1024, tf=512)
    return xf.reshape(B, S, D)
```

```python
import functools
import math

import jax
import jax.numpy as jnp
from jax import lax
from jax.experimental import pallas as pl
from jax.experimental.pallas import tpu as pltpu

F32 = jnp.float32
BF16 = jnp.bfloat16

RET_HEADS = 8
RET_HEAD_DIM = 128
RET_GN_EPS = 1e-5
ROPE_BASE = 10000.0
RWKV_HEAD_DIM = 64
RWKV_LN_EPS = 64e-5
NORM_EPS = 1e-6
LORA_W = 64
LORA_A = 64
LORA_G = 128

LANES = 128
BF16_SUBLANES = 16
V7X_VMEM_LIMIT_BYTES = 56 * 1024 * 1024

RET_CHUNK = 128
RWKV_CHUNK = 64
RWKV_ROWS = 256

_NT = (((1,), (1,)), ((), ()))
_TN = (((0,), (0,)), ((), ()))


def _params(semantics):
    return pltpu.CompilerParams(dimension_semantics=semantics,
                                vmem_limit_bytes=V7X_VMEM_LIMIT_BYTES)


def _bdot(a, b):
    return jnp.dot(a.astype(BF16), b.astype(BF16), preferred_element_type=F32)


def _norm_mm_kernel(x_ref, g_ref, w_ref, o_ref, xn_ref, *, sigmoid):
    @pl.when(pl.program_id(1) == 0)
    def _():
        x = x_ref[...]
        ms = jnp.mean(x * x, axis=-1, keepdims=True)
        xn_ref[...] = (x * lax.rsqrt(ms + NORM_EPS) * g_ref[...]).astype(BF16)

    acc = jnp.dot(xn_ref[...], w_ref[...], preferred_element_type=F32)
    if sigmoid:
        acc = jax.nn.sigmoid(acc)
    o_ref[...] = acc.astype(o_ref.dtype)


def _norm_matmul(x, g, w, *, name, sigmoid=False, out_dtype=F32, tm, tn):
    M, K = x.shape
    N = w.shape[1]
    return pl.pallas_call(
        functools.partial(_norm_mm_kernel, sigmoid=sigmoid),
        grid=(M // tm, N // tn),
        in_specs=[pl.BlockSpec((tm, K), lambda i, j: (i, 0)),
                  pl.BlockSpec((1, K), lambda i, j: (0, 0)),
                  pl.BlockSpec((K, tn), lambda i, j: (0, j))],
        out_specs=pl.BlockSpec((tm, tn), lambda i, j: (i, j)),
        out_shape=jax.ShapeDtypeStruct((M, N), out_dtype),
        scratch_shapes=[pltpu.VMEM((tm, K), BF16)],
        compiler_params=_params(("parallel", "arbitrary")),
        name=name,
    )(x, g.reshape(1, K), w)


def _mlp_kernel(*refs, final_norm, cast_splits):
    n_cast_in = len(cast_splits)
    n_cast_out = sum(len(s) - 1 for s in cast_splits)
    n_in = 5 if final_norm else 4
    x_ref, g_ref, wu_ref, wd_ref = refs[:4]
    fg_ref = refs[4] if final_norm else None
    cast_in = refs[n_in:n_in + n_cast_in]
    o_ref = refs[n_in + n_cast_in]
    cast_out = refs[n_in + n_cast_in + 1:n_in + n_cast_in + 1 + n_cast_out]
    xn_ref = refs[-1]
    j = pl.program_id(1)

    k = 0
    for w_ref, splits in zip(cast_in, cast_splits):
        for lo, hi in zip(splits[:-1], splits[1:]):
            cast_out[k][...] = w_ref[:, lo:hi].astype(BF16)
            k += 1

    @pl.when(j == 0)
    def _():
        x = x_ref[...]
        ms = jnp.mean(x * x, axis=-1, keepdims=True)
        xn_ref[...] = (x * lax.rsqrt(ms + NORM_EPS) * g_ref[...]).astype(BF16)
        o_ref[...] = x

    h = jnp.dot(xn_ref[...], wu_ref[...], preferred_element_type=F32)
    h = jnp.square(jnp.maximum(h, 0.0)).astype(BF16)
    o_ref[...] += jnp.dot(h, wd_ref[...], preferred_element_type=F32)

    if final_norm:
        @pl.when(j == pl.num_programs(1) - 1)
        def _():
            y = o_ref[...]
            ms = jnp.mean(y * y, axis=-1, keepdims=True)
            o_ref[...] = y * lax.rsqrt(ms + NORM_EPS) * fg_ref[...]


def _mlp(x, g, w_up, w_down, final_g=None, cast_next=(), *, tm, tf):
    M, D = x.shape
    F = w_up.shape[1]
    final_norm = final_g is not None
    ni, nj = M // tm, F // tf
    vec = pl.BlockSpec((1, D), lambda i, j: (0, 0))
    in_specs = [pl.BlockSpec((tm, D), lambda i, j: (i, 0)), vec,
                pl.BlockSpec((D, tf), lambda i, j: (0, j)),
                pl.BlockSpec((tf, D), lambda i, j: (j, 0))]
    args = [x, g.reshape(1, D), w_up, w_down]
    if final_norm:
        in_specs.append(vec)
        args.append(final_g.reshape(1, D))
    out_specs = [pl.BlockSpec((tm, D), lambda i, j: (i, 0))]
    out_shape = [jax.ShapeDtypeStruct((M, D), F32)]
    cast_splits = []
    for w, layer, splits in cast_next:
        _, R, N = w.shape
        rows = max(BF16_SUBLANES, R // (ni * nj))
        repeat = rows * ni * nj // R
        slab = lambda i, j, layer=layer, repeat=repeat: (layer, (i * nj + j) // repeat, 0)
        in_specs.append(pl.BlockSpec((None, rows, N), slab))
        args.append(w)
        for lo, hi in zip(splits[:-1], splits[1:]):
            out_specs.append(pl.BlockSpec((rows, hi - lo),
                                          lambda i, j, repeat=repeat: ((i * nj + j) // repeat, 0)))
            out_shape.append(jax.ShapeDtypeStruct((R, hi - lo), BF16))
        cast_splits.append(tuple(splits))
    outs = pl.pallas_call(
        functools.partial(_mlp_kernel, final_norm=final_norm, cast_splits=tuple(cast_splits)),
        grid=(ni, nj),
        in_specs=in_specs,
        out_specs=out_specs,
        out_shape=out_shape,
        scratch_shapes=[pltpu.VMEM((tm, D), BF16)],
        compiler_params=_params(("parallel", "arbitrary")),
        name="mlp_final" if final_norm else "mlp",
    )(*args)
    return outs[0], outs[1:]


def _merge_out_kernel(ya_ref, yb_ref, wa_ref, wb_ref, ga_ref, gb_ref, wo_ref, x_ref, o_ref):
    D = o_ref.shape[1]
    n_split = 2
    width = D // n_split
    ya = ya_ref[...]
    yb = yb_ref[...]
    acc = x_ref[...]
    for c in range(n_split):
        cs = slice(c * width, (c + 1) * width)
        a = jnp.dot(ya, wa_ref[:, cs], preferred_element_type=F32)
        b = jnp.dot(yb, wb_ref[:, cs], preferred_element_type=F32)
        m = (ga_ref[:, cs].astype(F32) * a + gb_ref[:, cs].astype(F32) * b).astype(BF16)
        acc = acc + jnp.dot(m, wo_ref[cs, :], preferred_element_type=F32)
    o_ref[...] = acc


def _merge_out(ya, yb, wa, wb, gates, wo, x, *, tm):
    M, K = ya.shape
    D = wo.shape[1]
    resident = lambda a: pl.BlockSpec(a.shape, lambda i: (0, 0), pipeline_mode=pl.Buffered(1))
    return pl.pallas_call(
        _merge_out_kernel,
        grid=(M // tm,),
        in_specs=[
            pl.BlockSpec((tm, K), lambda i: (i, 0)),
            pl.BlockSpec((tm, K), lambda i: (i, 0)),
            resident(wa), resident(wb),
            pl.BlockSpec((tm, D), lambda i: (i, 0)),
            pl.BlockSpec((tm, D), lambda i: (i, 1)),
            resident(wo),
            pl.BlockSpec((tm, D), lambda i: (i, 0)),
        ],
        out_specs=pl.BlockSpec((tm, D), lambda i: (i, 0)),
        out_shape=jax.ShapeDtypeStruct((M, D), F32),
        compiler_params=_params(("parallel",)),
        name="merge_out_proj",
    )(ya, yb, wa, wb, gates, gates, wo, x)


def _rope_kernel(pos_ref, cos_ref, sin_ref):
    half = RET_HEAD_DIM // 2
    pos = pos_ref[...].astype(F32)
    lane = lax.broadcasted_iota(jnp.int32, (1, RET_HEAD_DIM), 1)
    j = jnp.where(lane < half, lane, lane - half).astype(F32)
    inv_freq = jnp.exp(j * (-math.log(ROPE_BASE) / half))
    ang = pos * inv_freq
    cos_ref[...] = jnp.cos(ang)
    sin_ref[...] = jnp.where(lane < half, -1.0, 1.0) * jnp.sin(ang)


def _rope_tables(positions, *, tm):
    M = positions.size
    pos = positions.reshape(M, 1)
    return pl.pallas_call(
        _rope_kernel,
        grid=(M // tm,),
        in_specs=[pl.BlockSpec((tm, 1), lambda i: (i, 0))],
        out_specs=[pl.BlockSpec((tm, RET_HEAD_DIM), lambda i: (i, 0))] * 2,
        out_shape=[jax.ShapeDtypeStruct((M, RET_HEAD_DIM), F32)] * 2,
        compiler_params=_params(("parallel",)),
        name="rope_tables",
    )(pos)


def _retention_kernel(q_ref, k_ref, v_ref, gr_ref, cos_ref, sin_ref, gng_ref, gnb_ref,
                      o_ref, state_ref, inner_ref, qdec_ref, kdec_ref):
    C = q_ref.shape[1]
    d = RET_HEAD_DIM
    log_gammas = [math.log1p(-(2.0 ** (-5.0 - h))) for h in range(RET_HEADS)]

    @pl.when(pl.program_id(1) == 0)
    def _():
        state_ref[...] = jnp.zeros_like(state_ref)
        row = lax.broadcasted_iota(jnp.int32, (C, C), 0)
        col = lax.broadcasted_iota(jnp.int32, (C, C), 1)
        diff = (row - col).astype(F32)
        rowd = lax.broadcasted_iota(jnp.int32, (C, d), 0).astype(F32)
        scale = d ** -0.5
        for h, lg in enumerate(log_gammas):
            inner_ref[h] = jnp.where(row >= col, jnp.exp(jnp.maximum(diff, 0.0) * lg) * scale, 0.0)
            qdec_ref[h] = jnp.exp((rowd + 1.0) * lg)
            kdec_ref[h] = jnp.exp((C - 1.0 - rowd) * lg) * scale

    cos2 = cos_ref[0]
    sin2 = sin_ref[0]

    for h in range(RET_HEADS):
        sl = slice(h * d, (h + 1) * d)
        q = q_ref[0, :, sl].astype(F32)
        k = k_ref[0, :, sl].astype(F32)
        v = v_ref[0, :, sl].astype(BF16)
        q = q * cos2 + pltpu.roll(q, d // 2, axis=1) * sin2
        k = k * cos2 + pltpu.roll(k, d // 2, axis=1) * sin2
        chunk_decay = math.exp(C * log_gammas[h])

        qb = q.astype(BF16)
        scores = lax.dot_general(qb, k.astype(BF16), _NT, preferred_element_type=F32)
        inner = _bdot(scores * inner_ref[h], v)
        state = state_ref[h]
        cross = _bdot(qb, state) * qdec_ref[h]
        state_ref[h] = chunk_decay * state + lax.dot_general(
            (k * kdec_ref[h]).astype(BF16), v, _TN, preferred_element_type=F32)

        y = inner + cross
        mean = jnp.mean(y, axis=-1, keepdims=True)
        yc = y - mean
        var = jnp.mean(yc * yc, axis=-1, keepdims=True)
        yn = yc * lax.rsqrt(var + RET_GN_EPS) * gng_ref[:, sl] + gnb_ref[:, sl]
        gr = gr_ref[0, :, sl].astype(F32)
        o_ref[0, :, sl] = (gr * jax.nn.sigmoid(gr) * yn).astype(o_ref.dtype)


def _retention(z, cos2, sin2, gn_g, gn_b, *, col0):
    B, S, _ = z.shape
    W = RET_HEADS * RET_HEAD_DIM
    C = RET_CHUNK
    c0 = col0 // W
    zspec = lambda off: pl.BlockSpec((1, C, W), lambda b, n: (b, n, c0 + off))
    rspec = pl.BlockSpec((1, C, RET_HEAD_DIM), lambda b, n: (b, n, 0))
    pspec = pl.BlockSpec((1, W), lambda b, n: (0, 0))
    return pl.pallas_call(
        _retention_kernel,
        grid=(B, S // C),
        in_specs=[zspec(0), zspec(1), zspec(2), zspec(3), rspec, rspec, pspec, pspec],
        out_specs=pl.BlockSpec((1, C, W), lambda b, n: (b, n, 0)),
        out_shape=jax.ShapeDtypeStruct((B, S, W), BF16),
        scratch_shapes=[pltpu.VMEM((RET_HEADS, RET_HEAD_DIM, RET_HEAD_DIM), F32),
                        pltpu.VMEM((RET_HEADS, C, C), F32),
                        pltpu.VMEM((RET_HEADS, C, RET_HEAD_DIM), F32),
                        pltpu.VMEM((RET_HEADS, C, RET_HEAD_DIM), F32)],
        compiler_params=_params(("parallel", "arbitrary")),
        name="retention",
    )(z, z, z, z, cos2, sin2, gn_g.reshape(1, W), gn_b.reshape(1, W))


def _split2(x):
    hi = x.astype(BF16)
    return hi, (x - hi.astype(F32)).astype(BF16)


def _rwkv_staged_kernel(r_ref, k_ref, v_ref, lo_ref, mu_r_ref, mu_k_ref, mu_v_ref, mu_lo_ref,
                        w0_ref, wup_ref, a0_ref, aup_ref, gup_ref, kk_ref, ka_ref, rk_ref,
                        lng_ref, lnb_ref, o_ref,
                        state_ref, pr_ref, pk_ref, pv_ref, plo_ref):
    R = r_ref.shape[1]
    W = r_ref.shape[2]
    C = RWKV_CHUNK
    hd = RWKV_HEAD_DIM
    P = 2 * hd
    assert P == LANES and 2 * C == P and R % C == 0
    chunks = range(R // C)
    rows = [slice(c * C, (c + 1) * C) for c in chunks]
    pairs = range(W // P)
    sls = [slice(p * P, (p + 1) * P) for p in pairs]

    @pl.when(pl.program_id(1) == 0)
    def _():
        state_ref[...] = jnp.zeros_like(state_ref)
        pr_ref[...] = jnp.zeros_like(pr_ref)
        pk_ref[...] = jnp.zeros_like(pk_ref)
        pv_ref[...] = jnp.zeros_like(pv_ref)
        plo_ref[...] = jnp.zeros_like(plo_ref)

    def shift_mix(x, prev_ref, mu):
        first = lax.broadcasted_iota(jnp.int32, x.shape, 0) == 0
        xprev = jnp.where(first, prev_ref[...], pltpu.roll(x, 1, axis=0))
        prev_ref[...] = x[R - 1:R, :]
        return x + mu * (xprev - x)

    lo = shift_mix(lo_ref[0], plo_ref, mu_lo_ref[...])
    tw = jnp.tanh(lo[:, 0:LORA_W]).astype(BF16)
    za = lo[:, LORA_W:LORA_W + LORA_A].astype(BF16)
    sg = jax.nn.sigmoid(lo[:, LORA_W + LORA_A:]).astype(BF16)

    r = shift_mix(r_ref[0], pr_ref, mu_r_ref[...])
    k = shift_mix(k_ref[0], pk_ref, mu_k_ref[...])
    v = shift_mix(v_ref[0], pv_ref, mu_v_ref[...])
    logdec = -math.exp(-0.5) * jax.nn.sigmoid(
        w0_ref[...] + jnp.dot(tw, wup_ref[...], preferred_element_type=F32))
    a = jax.nn.sigmoid(a0_ref[...] + jnp.dot(za, aup_ref[...], preferred_element_type=F32))
    g = jnp.dot(sg, gup_ref[...], preferred_element_type=F32)
    kk_raw = k * kk_ref[...]
    k = k * (1.0 + (a - 1.0) * ka_ref[...])

    tri = jnp.where(lax.broadcasted_iota(jnp.int32, (C, C), 0)
                    >= lax.broadcasted_iota(jnp.int32, (C, C), 1), 1.0, 0.0).astype(BF16)
    ld_hi = logdec.astype(BF16)
    ld_mid, ld_lo = _split2(logdec - ld_hi.astype(F32))
    cum = [jnp.dot(tri, ld_hi[rc], preferred_element_type=F32)
           + jnp.dot(tri, ld_mid[rc], preferred_element_type=F32)
           + jnp.dot(tri, ld_lo[rc], preferred_element_type=F32) for rc in rows]

    row = lax.broadcasted_iota(jnp.int32, (C, P), 0)
    lane = lax.broadcasted_iota(jnp.int32, (C, P), 1)
    head0 = lane < hd
    cj = jnp.where(head0, lane, lane - hd)
    strict = row > cj
    incl = row >= cj
    eye2 = jnp.where(row == cj, 1.0, 0.0)
    r2 = lax.broadcasted_iota(jnp.int32, (P, P), 0)
    l2 = lax.broadcasted_iota(jnp.int32, (P, P), 1)
    bdmask = (r2 < hd) == (l2 < hd)
    zero = jnp.zeros((), BF16)

    def blockdiag(x):
        return jnp.where(bdmask, jnp.concatenate([x, x], axis=0), jnp.zeros((), x.dtype))

    def segsum(x):
        h0 = lax.broadcasted_iota(jnp.int32, x.shape, 1) < hd
        s0 = jnp.sum(jnp.where(h0, x, 0.0), axis=-1, keepdims=True)
        s1 = jnp.sum(jnp.where(h0, 0.0, x), axis=-1, keepdims=True)
        return jnp.where(h0, s0, s1)

    kk_sq = [segsum(jnp.square(kk_raw[:, sl])) for sl in sls]
    bonus = [segsum(r[:, sl] * k[:, sl] * rk_ref[:, sl]) for sl in sls]

    keys = [(c, p) for c in chunks for p in pairs]
    lhs, gram, bk_end, dec_chunk, vb = {}, {}, {}, {}, {}
    for q in keys:
        rc, sl = rows[q[0]], sls[q[1]]
        kk = kk_raw[rc, sl] * lax.rsqrt(jnp.maximum(kk_sq[q[1]][rc], 1e-24))
        cum_p = cum[q[0]][:, sl]
        cum_last = cum_p[C - 1:C, :]
        dec_to_end = jnp.exp(cum_last - cum_p)
        inv_p = jnp.exp(-cum_p)
        kka = kk * a[rc, sl]
        rt = r[rc, sl] * jnp.exp(cum_p)
        kt = (k[rc, sl] * inv_p).astype(BF16)
        at = -kk * jnp.exp(cum_p - logdec[rc, sl])
        bt = (kka * inv_p).astype(BF16)
        bk_end[q] = jnp.concatenate([kka * dec_to_end, k[rc, sl] * dec_to_end],
                                    axis=0).astype(BF16)
        dec_chunk[q] = jnp.exp(cum_last)
        vb[q] = v[rc, sl].astype(BF16)
        lhs[q] = jnp.concatenate([at, rt], axis=0).astype(BF16)
        rhs = jnp.concatenate([jnp.where(head0, bt, zero), jnp.where(head0, zero, bt),
                               jnp.where(head0, kt, zero), jnp.where(head0, zero, kt)], axis=0)
        gram[q] = lax.dot_general(lhs[q], rhs, _NT, preferred_element_type=F32)

    l_ab = {q: jnp.where(strict, gram[q][:C, :P], 0.0) for q in keys}
    lv = {q: jnp.dot(jnp.concatenate([jnp.where(strict, gram[q][:C, P:], 0.0),
                                      jnp.where(incl, gram[q][C:, P:], 0.0)], axis=0).astype(BF16),
                     blockdiag(vb[q]), preferred_element_type=F32) for q in keys}
    m_rb = {q: jnp.where(incl, gram[q][C:, :P], 0.0).astype(BF16) for q in keys}

    t = {q: eye2 + l_ab[q] for q in keys}
    m = {q: _bdot(l_ab[q], blockdiag(l_ab[q].astype(BF16))) for q in keys}
    n_sq = int(math.log2(C)) - 1
    for step in range(n_sq):
        for q in keys:
            mb = m[q].astype(BF16)
            if step < n_sq - 1:
                both = jnp.concatenate([blockdiag(t[q].astype(BF16)), blockdiag(mb)], axis=1)
                prod = jnp.dot(mb, both, preferred_element_type=F32)
                t[q] = t[q] + prod[:, :P]
                m[q] = prod[:, P:]
            else:
                t[q] = t[q] + jnp.dot(mb, blockdiag(t[q].astype(BF16)), preferred_element_type=F32)
    tb = {q: t[q].astype(BF16) for q in keys}

    state = [state_ref[p] for p in pairs]
    for c, rc in zip(chunks, rows):
        ar = [lax.dot_general(lhs[c, p], state[p].astype(BF16), _NT, preferred_element_type=F32)
              for p in pairs]
        ub = [jnp.dot(tb[c, p], blockdiag((ar[p][:C] + lv[c, p][:C]).astype(BF16)),
                      preferred_element_type=F32).astype(BF16) for p in pairs]
        y = [ar[p][C:] + lv[c, p][C:]
             + jnp.dot(m_rb[c, p], blockdiag(ub[p]), preferred_element_type=F32) for p in pairs]
        for p in pairs:
            upd = lax.dot_general(jnp.concatenate([ub[p], vb[c, p]], axis=0), bk_end[c, p],
                                  _TN, preferred_element_type=F32)
            state[p] = state[p] * dec_chunk[c, p] + jnp.where(bdmask, upd, 0.0)

        yc = [y[p] - segsum(y[p]) * (1.0 / hd) for p in pairs]
        var = [segsum(jnp.square(yc[p])) * (1.0 / hd) for p in pairs]
        for p, sl in zip(pairs, sls):
            yn = yc[p] * lax.rsqrt(var[p] + RWKV_LN_EPS) * lng_ref[:, sl] + lnb_ref[:, sl]
            o_ref[0, rc, sl] = ((yn + bonus[p][rc] * v[rc, sl]) * g[rc, sl]).astype(o_ref.dtype)
    for p in pairs:
        state_ref[p] = state[p]


def _rwkv(z, mu, w0, w_up, a0, a_up, g_up, k_k, k_a, r_k, ln_g, ln_b, *, col0):
    B, S, _ = z.shape
    W = w0.shape[-1]
    C = RWKV_ROWS
    LO = LORA_W + LORA_A + LORA_G
    P = 2 * RWKV_HEAD_DIM
    c0 = col0 // W
    lo0 = (col0 + 3 * W) // LO
    zspec = lambda off: pl.BlockSpec((1, C, W), lambda b, n: (b, n, c0 + off))
    full = lambda a: pl.BlockSpec(a.shape, lambda b, n: (0,) * a.ndim)
    row = lambda a: a.reshape(1, -1)
    params = [row(mu[:W]), row(mu[W:2 * W]), row(mu[2 * W:3 * W]), row(mu[3 * W:]),
              row(w0), w_up.astype(BF16), row(a0), a_up.astype(BF16), g_up.astype(BF16),
              row(k_k), row(k_a), row(r_k), row(ln_g), row(ln_b)]
    return pl.pallas_call(
        _rwkv_staged_kernel,
        grid=(B, S // C),
        in_specs=[zspec(0), zspec(1), zspec(2),
                  pl.BlockSpec((1, C, LO), lambda b, n: (b, n, lo0))] + [full(a) for a in params],
        out_specs=pl.BlockSpec((1, C, W), lambda b, n: (b, n, 0)),
        out_shape=jax.ShapeDtypeStruct((B, S, W), BF16),
        scratch_shapes=[pltpu.VMEM((W // P, P, P), F32),
                        pltpu.VMEM((1, W), F32), pltpu.VMEM((1, W), F32),
                        pltpu.VMEM((1, W), F32), pltpu.VMEM((1, LO), F32)],
        compiler_params=_params(("parallel", "arbitrary")),
        name="rwkv7",
    )(z, z, z, z, *params)


def kernel(x, positions, norm1_g, w_in, ret_gn_g, ret_gn_b, rwkv_mu, rwkv_w0, rwkv_w_up, rwkv_a0, rwkv_a_up, rwkv_g_up, rwkv_k_k, rwkv_k_a, rwkv_r_k, rwkv_ln_g, rwkv_ln_b, w_branch_a, w_branch_b, w_out, norm2_g, mlp_up, mlp_down, final_g):
    B, S, D = x.shape
    M = B * S
    depth = w_in.shape[0]
    ret_w = ret_gn_g.shape[-1]
    rwkv_w = rwkv_w0.shape[-1]
    ret_cols = 4 * ret_w
    rwkv_cols = rwkv_mu.shape[-1]
    cos2, sin2 = _rope_tables(positions, tm=2048)
    cos2 = cos2.reshape(B, S, RET_HEAD_DIM)
    sin2 = sin2.reshape(B, S, RET_HEAD_DIM)

    in_splits = (0, ret_cols, ret_cols + rwkv_cols, w_in.shape[-1])
    w0 = w_in[0]
    wts = [w0[:, in_splits[i]:in_splits[i + 1]].astype(BF16) for i in range(3)]
    wts += [t[0].astype(BF16) for t in (w_branch_a, w_branch_b, w_out, mlp_up, mlp_down)]

    xf = x.reshape(M, D)
    for l in range(depth):
        w_ret, w_rwkv, w_gate, w_a, w_b, w_o, w_up, w_down = wts
        z_ret = _norm_matmul(xf, norm1_g[l], w_ret, name="norm_in_proj_ret",
                             out_dtype=BF16, tm=1024, tn=2048)
        z_gate = _norm_matmul(xf, norm1_g[l], w_gate, name="norm_in_proj_gate",
                              sigmoid=True, out_dtype=BF16, tm=1024, tn=2048)
        z_rwkv = _norm_matmul(xf, norm1_g[l], w_rwkv, name="norm_in_proj_rwkv",
                              tm=1024, tn=rwkv_cols // 2)
        y_ret = _retention(z_ret.reshape(B, S, -1), cos2, sin2, ret_gn_g[l], ret_gn_b[l], col0=0)
        y_rwkv = _rwkv(z_rwkv.reshape(B, S, -1), rwkv_mu[l], rwkv_w0[l], rwkv_w_up[l],
                       rwkv_a0[l], rwkv_a_up[l], rwkv_g_up[l], rwkv_k_k[l], rwkv_k_a[l],
                       rwkv_r_k[l], rwkv_ln_g[l], rwkv_ln_b[l], col0=0)
        xf = _merge_out(y_ret.reshape(M, ret_w), y_rwkv.reshape(M, rwkv_w),
                        w_a, w_b, z_gate, w_o, xf, tm=512)
        if l + 1 < depth:
            full = lambda t: (t, l + 1, (0, t.shape[-1]))
            cast_next = [(w_in, l + 1, in_splits), full(w_branch_a), full(w_branch_b),
                         full(w_out), full(mlp_up), full(mlp_down)]
            xf, wts = _mlp(xf, norm2_g[l], w_up, w_down, cast_next=cast_next, tm=1024, tf=512)
        else:
            xf, _ = _mlp(xf, norm2_g[l], w_up, w_down, final_g, tm=1024, tf=512)
    return xf.reshape(B, S, D)
```

```python
import functools
import math

import jax
import jax.numpy as jnp
from jax import lax
from jax.experimental import pallas as pl
from jax.experimental.pallas import tpu as pltpu

F32 = jnp.float32
BF16 = jnp.bfloat16

RET_HEADS = 8
RET_HEAD_DIM = 128
RET_GN_EPS = 1e-5
ROPE_BASE = 10000.0
RWKV_HEAD_DIM = 64
RWKV_LN_EPS = 64e-5
NORM_EPS = 1e-6
LORA_W = 64
LORA_A = 64
LORA_G = 128

LANES = 128
BF16_SUBLANES = 16
V7X_VMEM_LIMIT_BYTES = 56 * 1024 * 1024

RET_CHUNK = 128
RWKV_CHUNK = 64
RWKV_ROWS = 256
RWKV_GROUP = 2

_NT = (((1,), (1,)), ((), ()))
_TN = (((0,), (0,)), ((), ()))


def _params(semantics):
    return pltpu.CompilerParams(dimension_semantics=semantics,
                                vmem_limit_bytes=V7X_VMEM_LIMIT_BYTES)


def _bdot(a, b):
    return jnp.dot(a.astype(BF16), b.astype(BF16), preferred_element_type=F32)


def _norm_mm_kernel(x_ref, g_ref, w_ref, o_ref, xn_ref):
    @pl.when(pl.program_id(1) == 0)
    def _():
        x = x_ref[...]
        ms = jnp.mean(x * x, axis=-1, keepdims=True)
        xn_ref[...] = (x * lax.rsqrt(ms + NORM_EPS) * g_ref[...]).astype(BF16)

    o_ref[...] = jnp.dot(xn_ref[...], w_ref[...], preferred_element_type=F32).astype(o_ref.dtype)


def _norm_matmul(x, g, w, *, name, out_dtype, tm, tn):
    M, K = x.shape
    N = w.shape[1]
    return pl.pallas_call(
        _norm_mm_kernel,
        grid=(M // tm, N // tn),
        in_specs=[pl.BlockSpec((tm, K), lambda i, j: (i, 0)),
                  pl.BlockSpec((1, K), lambda i, j: (0, 0)),
                  pl.BlockSpec((K, tn), lambda i, j: (0, j))],
        out_specs=[pl.BlockSpec((tm, tn), lambda i, j: (i, j)),
                   pl.BlockSpec((tm, K), lambda i, j: (i, 0))],
        out_shape=[jax.ShapeDtypeStruct((M, N), out_dtype), jax.ShapeDtypeStruct((M, K), BF16)],
        compiler_params=_params(("parallel", "arbitrary")),
        name=name,
    )(x, g.reshape(1, K), w)


def _mm_kernel(x_ref, w_ref, o_ref, *, sigmoid):
    acc = jnp.dot(x_ref[...], w_ref[...], preferred_element_type=F32)
    if sigmoid:
        acc = jax.nn.sigmoid(acc)
    o_ref[...] = acc.astype(o_ref.dtype)


def _matmul(x, w, *, name, sigmoid=False, out_dtype, tm, tn):
    M, K = x.shape
    N = w.shape[1]
    return pl.pallas_call(
        functools.partial(_mm_kernel, sigmoid=sigmoid),
        grid=(M // tm, N // tn),
        in_specs=[pl.BlockSpec((tm, K), lambda i, j: (i, 0)),
                  pl.BlockSpec((K, tn), lambda i, j: (0, j))],
        out_specs=pl.BlockSpec((tm, tn), lambda i, j: (i, j)),
        out_shape=jax.ShapeDtypeStruct((M, N), out_dtype),
        compiler_params=_params(("parallel", "arbitrary")),
        name=name,
    )(x, w)


def _mlp_kernel(*refs, final_norm, cast_splits):
    n_cast_in = len(cast_splits)
    n_cast_out = sum(len(s) - 1 for s in cast_splits)
    n_in = 5 if final_norm else 4
    x_ref, g_ref, wu_ref, wd_ref = refs[:4]
    fg_ref = refs[4] if final_norm else None
    cast_in = refs[n_in:n_in + n_cast_in]
    o_ref = refs[n_in + n_cast_in]
    cast_out = refs[n_in + n_cast_in + 1:n_in + n_cast_in + 1 + n_cast_out]
    xn_ref = refs[-1]
    j = pl.program_id(1)

    k = 0
    for w_ref, splits in zip(cast_in, cast_splits):
        for lo, hi in zip(splits[:-1], splits[1:]):
            cast_out[k][...] = w_ref[:, lo:hi].astype(BF16)
            k += 1

    @pl.when(j == 0)
    def _():
        x = x_ref[...]
        ms = jnp.mean(x * x, axis=-1, keepdims=True)
        xn_ref[...] = (x * lax.rsqrt(ms + NORM_EPS) * g_ref[...]).astype(BF16)
        o_ref[...] = x

    h = jnp.dot(xn_ref[...], wu_ref[...], preferred_element_type=F32)
    h = jnp.square(jnp.maximum(h, 0.0)).astype(BF16)
    o_ref[...] += jnp.dot(h, wd_ref[...], preferred_element_type=F32)

    if final_norm:
        @pl.when(j == pl.num_programs(1) - 1)
        def _():
            y = o_ref[...]
            ms = jnp.mean(y * y, axis=-1, keepdims=True)
            o_ref[...] = y * lax.rsqrt(ms + NORM_EPS) * fg_ref[...]


def _mlp(x, g, w_up, w_down, final_g=None, cast_next=(), *, tm, tf):
    M, D = x.shape
    F = w_up.shape[1]
    final_norm = final_g is not None
    ni, nj = M // tm, F // tf
    vec = pl.BlockSpec((1, D), lambda i, j: (0, 0))
    in_specs = [pl.BlockSpec((tm, D), lambda i, j: (i, 0)), vec,
                pl.BlockSpec((D, tf), lambda i, j: (0, j)),
                pl.BlockSpec((tf, D), lambda i, j: (j, 0))]
    args = [x, g.reshape(1, D), w_up, w_down]
    if final_norm:
        in_specs.append(vec)
        args.append(final_g.reshape(1, D))
    out_specs = [pl.BlockSpec((tm, D), lambda i, j: (i, 0))]
    out_shape = [jax.ShapeDtypeStruct((M, D), F32)]
    cast_splits = []
    for w, layer, splits in cast_next:
        _, R, N = w.shape
        rows = max(BF16_SUBLANES, R // (ni * nj))
        repeat = rows * ni * nj // R
        slab = lambda i, j, layer=layer, repeat=repeat: (layer, (i * nj + j) // repeat, 0)
        in_specs.append(pl.BlockSpec((None, rows, N), slab))
        args.append(w)
        for lo, hi in zip(splits[:-1], splits[1:]):
            out_specs.append(pl.BlockSpec((rows, hi - lo),
                                          lambda i, j, repeat=repeat: ((i * nj + j) // repeat, 0)))
            out_shape.append(jax.ShapeDtypeStruct((R, hi - lo), BF16))
        cast_splits.append(tuple(splits))
    outs = pl.pallas_call(
        functools.partial(_mlp_kernel, final_norm=final_norm, cast_splits=tuple(cast_splits)),
        grid=(ni, nj),
        in_specs=in_specs,
        out_specs=out_specs,
        out_shape=out_shape,
        scratch_shapes=[pltpu.VMEM((tm, D), BF16)],
        compiler_params=_params(("parallel", "arbitrary")),
        name="mlp_final" if final_norm else "mlp",
    )(*args)
    return outs[0], outs[1:]


def _merge_out_kernel(ya_ref, yb_ref, wa_ref, wb_ref, ga_ref, gb_ref, wo_ref, x_ref, o_ref):
    D = o_ref.shape[1]
    n_split = 2
    width = D // n_split
    ya = ya_ref[...]
    yb = yb_ref[...]
    acc = x_ref[...]
    for c in range(n_split):
        cs = slice(c * width, (c + 1) * width)
        a = jnp.dot(ya, wa_ref[:, cs], preferred_element_type=F32)
        b = jnp.dot(yb, wb_ref[:, cs], preferred_element_type=F32)
        m = (ga_ref[:, cs].astype(F32) * a + gb_ref[:, cs].astype(F32) * b).astype(BF16)
        acc = acc + jnp.dot(m, wo_ref[cs, :], preferred_element_type=F32)
    o_ref[...] = acc


def _merge_out(ya, yb, wa, wb, gates, wo, x, *, tm):
    M, K = ya.shape
    D = wo.shape[1]
    resident = lambda a: pl.BlockSpec(a.shape, lambda i: (0, 0), pipeline_mode=pl.Buffered(1))
    return pl.pallas_call(
        _merge_out_kernel,
        grid=(M // tm,),
        in_specs=[
            pl.BlockSpec((tm, K), lambda i: (i, 0)),
            pl.BlockSpec((tm, K), lambda i: (i, 0)),
            resident(wa), resident(wb),
            pl.BlockSpec((tm, D), lambda i: (i, 0)),
            pl.BlockSpec((tm, D), lambda i: (i, 1)),
            resident(wo),
            pl.BlockSpec((tm, D), lambda i: (i, 0)),
        ],
        out_specs=pl.BlockSpec((tm, D), lambda i: (i, 0)),
        out_shape=jax.ShapeDtypeStruct((M, D), F32),
        compiler_params=_params(("parallel",)),
        name="merge_out_proj",
    )(ya, yb, wa, wb, gates, gates, wo, x)


def _rope_kernel(pos_ref, cos_ref, sin_ref):
    half = RET_HEAD_DIM // 2
    pos = pos_ref[...].astype(F32)
    lane = lax.broadcasted_iota(jnp.int32, (1, RET_HEAD_DIM), 1)
    j = jnp.where(lane < half, lane, lane - half).astype(F32)
    inv_freq = jnp.exp(j * (-math.log(ROPE_BASE) / half))
    ang = pos * inv_freq
    cos_ref[...] = jnp.cos(ang)
    sin_ref[...] = jnp.where(lane < half, -1.0, 1.0) * jnp.sin(ang)


def _rope_tables(positions, *, tm):
    M = positions.size
    pos = positions.reshape(M, 1)
    return pl.pallas_call(
        _rope_kernel,
        grid=(M // tm,),
        in_specs=[pl.BlockSpec((tm, 1), lambda i: (i, 0))],
        out_specs=[pl.BlockSpec((tm, RET_HEAD_DIM), lambda i: (i, 0))] * 2,
        out_shape=[jax.ShapeDtypeStruct((M, RET_HEAD_DIM), F32)] * 2,
        compiler_params=_params(("parallel",)),
        name="rope_tables",
    )(pos)


def _retention_kernel(q_ref, k_ref, v_ref, gr_ref, cos_ref, sin_ref, gng_ref, gnb_ref,
                      o_ref, state_ref, inner_ref, qdec_ref, kdec_ref):
    C = q_ref.shape[1]
    d = RET_HEAD_DIM
    log_gammas = [math.log1p(-(2.0 ** (-5.0 - h))) for h in range(RET_HEADS)]

    @pl.when(pl.program_id(1) == 0)
    def _():
        state_ref[...] = jnp.zeros_like(state_ref)
        row = lax.broadcasted_iota(jnp.int32, (C, C), 0)
        col = lax.broadcasted_iota(jnp.int32, (C, C), 1)
        diff = (row - col).astype(F32)
        rowd = lax.broadcasted_iota(jnp.int32, (C, d), 0).astype(F32)
        scale = d ** -0.5
        for h, lg in enumerate(log_gammas):
            inner_ref[h] = jnp.where(row >= col, jnp.exp(jnp.maximum(diff, 0.0) * lg) * scale, 0.0)
            qdec_ref[h] = jnp.exp((rowd + 1.0) * lg)
            kdec_ref[h] = jnp.exp((C - 1.0 - rowd) * lg) * scale

    cos2 = cos_ref[0]
    sin2 = sin_ref[0]

    for h in range(RET_HEADS):
        sl = slice(h * d, (h + 1) * d)
        q = q_ref[0, :, sl].astype(F32)
        k = k_ref[0, :, sl].astype(F32)
        v = v_ref[0, :, sl].astype(BF16)
        q = q * cos2 + pltpu.roll(q, d // 2, axis=1) * sin2
        k = k * cos2 + pltpu.roll(k, d // 2, axis=1) * sin2
        chunk_decay = math.exp(C * log_gammas[h])

        qb = q.astype(BF16)
        scores = lax.dot_general(qb, k.astype(BF16), _NT, preferred_element_type=F32)
        inner = _bdot(scores * inner_ref[h], v)
        state = state_ref[h]
        cross = _bdot(qb, state) * qdec_ref[h]
        state_ref[h] = chunk_decay * state + lax.dot_general(
            (k * kdec_ref[h]).astype(BF16), v, _TN, preferred_element_type=F32)

        y = inner + cross
        mean = jnp.mean(y, axis=-1, keepdims=True)
        yc = y - mean
        var = jnp.mean(yc * yc, axis=-1, keepdims=True)
        yn = yc * lax.rsqrt(var + RET_GN_EPS) * gng_ref[:, sl] + gnb_ref[:, sl]
        gr = gr_ref[0, :, sl].astype(F32)
        o_ref[0, :, sl] = (gr * jax.nn.sigmoid(gr) * yn).astype(o_ref.dtype)


def _retention(z, cos2, sin2, gn_g, gn_b, *, col0):
    B, S, _ = z.shape
    W = RET_HEADS * RET_HEAD_DIM
    C = RET_CHUNK
    c0 = col0 // W
    zspec = lambda off: pl.BlockSpec((1, C, W), lambda b, n: (b, n, c0 + off))
    rspec = pl.BlockSpec((1, C, RET_HEAD_DIM), lambda b, n: (b, n, 0))
    pspec = pl.BlockSpec((1, W), lambda b, n: (0, 0))
    return pl.pallas_call(
        _retention_kernel,
        grid=(B, S // C),
        in_specs=[zspec(0), zspec(1), zspec(2), zspec(3), rspec, rspec, pspec, pspec],
        out_specs=pl.BlockSpec((1, C, W), lambda b, n: (b, n, 0)),
        out_shape=jax.ShapeDtypeStruct((B, S, W), BF16),
        scratch_shapes=[pltpu.VMEM((RET_HEADS, RET_HEAD_DIM, RET_HEAD_DIM), F32),
                        pltpu.VMEM((RET_HEADS, C, C), F32),
                        pltpu.VMEM((RET_HEADS, C, RET_HEAD_DIM), F32),
                        pltpu.VMEM((RET_HEADS, C, RET_HEAD_DIM), F32)],
        compiler_params=_params(("parallel", "arbitrary")),
        name="retention",
    )(z, z, z, z, cos2, sin2, gn_g.reshape(1, W), gn_b.reshape(1, W))


def _split2(x):
    hi = x.astype(BF16)
    return hi, (x - hi.astype(F32)).astype(BF16)


def _rwkv_staged_kernel(r_ref, k_ref, v_ref, lo_ref, mu_r_ref, mu_k_ref, mu_v_ref, mu_lo_ref,
                        w0_ref, wup_ref, a0_ref, aup_ref, gup_ref, kk_ref, ka_ref, rk_ref,
                        lng_ref, lnb_ref, o_ref,
                        state_ref, pr_ref, pk_ref, pv_ref, plo_ref):
    R = r_ref.shape[1]
    W = r_ref.shape[2]
    C = RWKV_CHUNK
    hd = RWKV_HEAD_DIM
    P = 2 * hd
    assert P == LANES and 2 * C == P and R % C == 0
    chunks = range(R // C)
    rows = [slice(c * C, (c + 1) * C) for c in chunks]
    pairs = range(W // P)
    sls = [slice(p * P, (p + 1) * P) for p in pairs]

    @pl.when(pl.program_id(1) == 0)
    def _():
        state_ref[...] = jnp.zeros_like(state_ref)
        pr_ref[...] = jnp.zeros_like(pr_ref)
        pk_ref[...] = jnp.zeros_like(pk_ref)
        pv_ref[...] = jnp.zeros_like(pv_ref)
        plo_ref[...] = jnp.zeros_like(plo_ref)

    def shift_mix(x, prev_ref, mu):
        first = lax.broadcasted_iota(jnp.int32, x.shape, 0) == 0
        xprev = jnp.where(first, prev_ref[...], pltpu.roll(x, 1, axis=0))
        prev_ref[...] = x[R - 1:R, :]
        return x + mu * (xprev - x)

    lo = shift_mix(lo_ref[0], plo_ref, mu_lo_ref[...])
    tw = jnp.tanh(lo[:, 0:LORA_W]).astype(BF16)
    za = lo[:, LORA_W:LORA_W + LORA_A].astype(BF16)
    sg = jax.nn.sigmoid(lo[:, LORA_W + LORA_A:]).astype(BF16)

    r = shift_mix(r_ref[0], pr_ref, mu_r_ref[...])
    k = shift_mix(k_ref[0], pk_ref, mu_k_ref[...])
    v = shift_mix(v_ref[0], pv_ref, mu_v_ref[...])
    logdec = -math.exp(-0.5) * jax.nn.sigmoid(
        w0_ref[...] + jnp.dot(tw, wup_ref[...], preferred_element_type=F32))
    a = jax.nn.sigmoid(a0_ref[...] + jnp.dot(za, aup_ref[...], preferred_element_type=F32))
    g = jnp.dot(sg, gup_ref[...], preferred_element_type=F32)
    kk_raw = k * kk_ref[...]
    k = k * (1.0 + (a - 1.0) * ka_ref[...])

    tri = jnp.where(lax.broadcasted_iota(jnp.int32, (C, C), 0)
                    >= lax.broadcasted_iota(jnp.int32, (C, C), 1), 1.0, 0.0).astype(BF16)
    ld_hi = logdec.astype(BF16)
    ld_mid, ld_lo = _split2(logdec - ld_hi.astype(F32))
    cum = [jnp.dot(tri, ld_hi[rc], preferred_element_type=F32)
           + jnp.dot(tri, ld_mid[rc], preferred_element_type=F32)
           + jnp.dot(tri, ld_lo[rc], preferred_element_type=F32) for rc in rows]

    row = lax.broadcasted_iota(jnp.int32, (C, P), 0)
    lane = lax.broadcasted_iota(jnp.int32, (C, P), 1)
    head0 = lane < hd
    cj = jnp.where(head0, lane, lane - hd)
    strict = row > cj
    incl = row >= cj
    eye2 = jnp.where(row == cj, 1.0, 0.0)
    r2 = lax.broadcasted_iota(jnp.int32, (P, P), 0)
    l2 = lax.broadcasted_iota(jnp.int32, (P, P), 1)
    bdmask = (r2 < hd) == (l2 < hd)
    zero = jnp.zeros((), BF16)

    def blockdiag(x):
        return jnp.where(bdmask, jnp.concatenate([x, x], axis=0), jnp.zeros((), x.dtype))

    def segsum(x):
        h0 = lax.broadcasted_iota(jnp.int32, x.shape, 1) < hd
        s0 = jnp.sum(jnp.where(h0, x, 0.0), axis=-1, keepdims=True)
        s1 = jnp.sum(jnp.where(h0, 0.0, x), axis=-1, keepdims=True)
        return jnp.where(h0, s0, s1)

    kk_sq = [segsum(jnp.square(kk_raw[:, sl])) for sl in sls]
    bonus = [segsum(r[:, sl] * k[:, sl] * rk_ref[:, sl]) for sl in sls]

    lhs, lv, m_rb, tb, bk_end, dec_chunk, vb = ({} for _ in range(7))

    def independent(group):
        keys = [(c, p) for c in group for p in pairs]
        gram = {}
        for q in keys:
            rc, sl = rows[q[0]], sls[q[1]]
            kk = kk_raw[rc, sl] * lax.rsqrt(jnp.maximum(kk_sq[q[1]][rc], 1e-24))
            cum_p = cum[q[0]][:, sl]
            cum_last = cum_p[C - 1:C, :]
            dec_to_end = jnp.exp(cum_last - cum_p)
            inv_p = jnp.exp(-cum_p)
            kka = kk * a[rc, sl]
            rt = r[rc, sl] * jnp.exp(cum_p)
            kt = (k[rc, sl] * inv_p).astype(BF16)
            at = -kk * jnp.exp(cum_p - logdec[rc, sl])
            bt = (kka * inv_p).astype(BF16)
            bk_end[q] = jnp.concatenate([kka * dec_to_end, k[rc, sl] * dec_to_end],
                                        axis=0).astype(BF16)
            dec_chunk[q] = jnp.exp(cum_last)
            vb[q] = v[rc, sl].astype(BF16)
            lhs[q] = jnp.concatenate([at, rt], axis=0).astype(BF16)
            rhs = jnp.concatenate([jnp.where(head0, bt, zero), jnp.where(head0, zero, bt),
                                   jnp.where(head0, kt, zero), jnp.where(head0, zero, kt)], axis=0)
            gram[q] = lax.dot_general(lhs[q], rhs, _NT, preferred_element_type=F32)
        yield

        l_ab = {q: jnp.where(strict, gram[q][:C, :P], 0.0) for q in keys}
        for q in keys:
            l_akrk = jnp.concatenate([jnp.where(strict, gram[q][:C, P:], 0.0),
                                      jnp.where(incl, gram[q][C:, P:], 0.0)], axis=0)
            lv[q] = jnp.dot(l_akrk.astype(BF16), blockdiag(vb[q]), preferred_element_type=F32)
            m_rb[q] = jnp.where(incl, gram[q][C:, :P], 0.0).astype(BF16)
        yield

        t = {q: eye2 + l_ab[q] for q in keys}
        m = {q: _bdot(l_ab[q], blockdiag(l_ab[q].astype(BF16))) for q in keys}
        yield
        n_sq = int(math.log2(C)) - 1
        for step in range(n_sq):
            for q in keys:
                mb = m[q].astype(BF16)
                if step < n_sq - 1:
                    both = jnp.concatenate([blockdiag(t[q].astype(BF16)), blockdiag(mb)], axis=1)
                    prod = jnp.dot(mb, both, preferred_element_type=F32)
                    t[q] = t[q] + prod[:, :P]
                    m[q] = prod[:, P:]
                else:
                    t[q] = t[q] + jnp.dot(mb, blockdiag(t[q].astype(BF16)),
                                          preferred_element_type=F32)
            yield
        for q in keys:
            tb[q] = t[q].astype(BF16)

    state = [state_ref[p] for p in pairs]

    def dependent(group):
        for c in group:
            rc = rows[c]
            ar = [lax.dot_general(lhs[c, p], state[p].astype(BF16), _NT,
                                  preferred_element_type=F32) for p in pairs]
            yield
            ub = [jnp.dot(tb[c, p], blockdiag((ar[p][:C] + lv[c, p][:C]).astype(BF16)),
                          preferred_element_type=F32).astype(BF16) for p in pairs]
            yield
            y = [ar[p][C:] + lv[c, p][C:]
                 + jnp.dot(m_rb[c, p], blockdiag(ub[p]), preferred_element_type=F32) for p in pairs]
            for p in pairs:
                upd = lax.dot_general(jnp.concatenate([ub[p], vb[c, p]], axis=0), bk_end[c, p],
                                      _TN, preferred_element_type=F32)
                state[p] = state[p] * dec_chunk[c, p] + jnp.where(bdmask, upd, 0.0)
            yield
            yc = [y[p] - segsum(y[p]) * (1.0 / hd) for p in pairs]
            var = [segsum(jnp.square(yc[p])) * (1.0 / hd) for p in pairs]
            for p, sl in zip(pairs, sls):
                yn = yc[p] * lax.rsqrt(var[p] + RWKV_LN_EPS) * lng_ref[:, sl] + lnb_ref[:, sl]
                o_ref[0, rc, sl] = ((yn + bonus[p][rc] * v[rc, sl]) * g[rc, sl]).astype(o_ref.dtype)
            yield

    def run(*stages):
        stages = list(stages)
        while stages:
            for stage in tuple(stages):
                if next(stage, stages) is stages:
                    stages.remove(stage)

    groups = [list(chunks[i:i + RWKV_GROUP]) for i in range(0, len(chunks), RWKV_GROUP)]
    run(independent(groups[0]))
    for gi, group in enumerate(groups):
        ahead = [independent(groups[gi + 1])] if gi + 1 < len(groups) else []
        run(dependent(group), *ahead)
    for p in pairs:
        state_ref[p] = state[p]


def _rwkv(z, mu, w0, w_up, a0, a_up, g_up, k_k, k_a, r_k, ln_g, ln_b, *, col0):
    B, S, _ = z.shape
    W = w0.shape[-1]
    C = RWKV_ROWS
    LO = LORA_W + LORA_A + LORA_G
    P = 2 * RWKV_HEAD_DIM
    c0 = col0 // W
    lo0 = (col0 + 3 * W) // LO
    zspec = lambda off: pl.BlockSpec((1, C, W), lambda b, n: (b, n, c0 + off))
    full = lambda a: pl.BlockSpec(a.shape, lambda b, n: (0,) * a.ndim)
    row = lambda a: a.reshape(1, -1)
    params = [row(mu[:W]), row(mu[W:2 * W]), row(mu[2 * W:3 * W]), row(mu[3 * W:]),
              row(w0), w_up.astype(BF16), row(a0), a_up.astype(BF16), g_up.astype(BF16),
              row(k_k), row(k_a), row(r_k), row(ln_g), row(ln_b)]
    return pl.pallas_call(
        _rwkv_staged_kernel,
        grid=(B, S // C),
        in_specs=[zspec(0), zspec(1), zspec(2),
                  pl.BlockSpec((1, C, LO), lambda b, n: (b, n, lo0))] + [full(a) for a in params],
        out_specs=pl.BlockSpec((1, C, W), lambda b, n: (b, n, 0)),
        out_shape=jax.ShapeDtypeStruct((B, S, W), BF16),
        scratch_shapes=[pltpu.VMEM((W // P, P, P), F32),
                        pltpu.VMEM((1, W), F32), pltpu.VMEM((1, W), F32),
                        pltpu.VMEM((1, W), F32), pltpu.VMEM((1, LO), F32)],
        compiler_params=_params(("parallel", "arbitrary")),
        name="rwkv7",
    )(z, z, z, z, *params)


def kernel(x, positions, norm1_g, w_in, ret_gn_g, ret_gn_b, rwkv_mu, rwkv_w0, rwkv_w_up, rwkv_a0, rwkv_a_up, rwkv_g_up, rwkv_k_k, rwkv_k_a, rwkv_r_k, rwkv_ln_g, rwkv_ln_b, w_branch_a, w_branch_b, w_out, norm2_g, mlp_up, mlp_down, final_g):
    B, S, D = x.shape
    M = B * S
    depth = w_in.shape[0]
    ret_w = ret_gn_g.shape[-1]
    rwkv_w = rwkv_w0.shape[-1]
    ret_cols = 4 * ret_w
    rwkv_cols = rwkv_mu.shape[-1]
    cos2, sin2 = _rope_tables(positions, tm=2048)
    cos2 = cos2.reshape(B, S, RET_HEAD_DIM)
    sin2 = sin2.reshape(B, S, RET_HEAD_DIM)

    in_splits = (0, ret_cols, ret_cols + rwkv_cols, w_in.shape[-1])
    w0 = w_in[0]
    wts = [w0[:, in_splits[i]:in_splits[i + 1]].astype(BF16) for i in range(3)]
    wts += [t[0].astype(BF16) for t in (w_branch_a, w_branch_b, w_out, mlp_up, mlp_down)]

    xf = x.reshape(M, D)
    for l in range(depth):
        w_ret, w_rwkv, w_gate, w_a, w_b, w_o, w_up, w_down = wts
        z_ret, xn = _norm_matmul(xf, norm1_g[l], w_ret, name="norm_in_proj_ret",
                                 out_dtype=BF16, tm=1024, tn=2048)
        z_gate = _matmul(xn, w_gate, name="in_proj_gate", sigmoid=True, out_dtype=BF16,
                         tm=1024, tn=2048)
        z_rwkv = _matmul(xn, w_rwkv, name="in_proj_rwkv", out_dtype=F32,
                         tm=1024, tn=rwkv_cols // 2)
        y_ret = _retention(z_ret.reshape(B, S, -1), cos2, sin2, ret_gn_g[l], ret_gn_b[l], col0=0)
        y_rwkv = _rwkv(z_rwkv.reshape(B, S, -1), rwkv_mu[l], rwkv_w0[l], rwkv_w_up[l],
                       rwkv_a0[l], rwkv_a_up[l], rwkv_g_up[l], rwkv_k_k[l], rwkv_k_a[l],
                       rwkv_r_k[l], rwkv_ln_g[l], rwkv_ln_b[l], col0=0)
        xf = _merge_out(y_ret.reshape(M, ret_w), y_rwkv.reshape(M, rwkv_w),
                        w_a, w_b, z_gate, w_o, xf, tm=512)
        if l + 1 < depth:
            full = lambda t: (t, l + 1, (0, t.shape[-1]))
            cast_next = [(w_in, l + 1, in_splits), full(w_branch_a), full(w_branch_b),
                         full(w_out), full(mlp_up), full(mlp_down)]
            xf, wts = _mlp(xf, norm2_g[l], w_up, w_down, cast_next=cast_next, tm=1024, tf=512)
        else:
            xf, _ = _mlp(xf, norm2_g[l], w_up, w_down, final_g, tm=1024, tf=512)
    return xf.reshape(B, S, D)
```

```python
import functools
import math

import jax
import jax.numpy as jnp
from jax import lax
from jax.experimental import pallas as pl
from jax.experimental.pallas import tpu as pltpu

F32 = jnp.float32
BF16 = jnp.bfloat16

RET_HEADS = 8
RET_HEAD_DIM = 128
RET_GN_EPS = 1e-5
ROPE_BASE = 10000.0
RWKV_HEAD_DIM = 64
RWKV_LN_EPS = 64e-5
NORM_EPS = 1e-6
LORA_W = 64
LORA_A = 64
LORA_G = 128

LANES = 128
BF16_SUBLANES = 16
V7X_VMEM_LIMIT_BYTES = 56 * 1024 * 1024

RET_CHUNK = 128
RWKV_CHUNK = 64
RWKV_ROWS = 256
RWKV_GROUP = 2

_NT = (((1,), (1,)), ((), ()))
_TN = (((0,), (0,)), ((), ()))


def _params(semantics):
    return pltpu.CompilerParams(dimension_semantics=semantics,
                                vmem_limit_bytes=V7X_VMEM_LIMIT_BYTES)


def _bdot(a, b):
    return jnp.dot(a.astype(BF16), b.astype(BF16), preferred_element_type=F32)


def _rotary(t, cos2, sin2):
    return t * cos2 + pltpu.roll(t, RET_HEAD_DIM // 2, axis=1) * sin2


def _norm_qk_kernel(x_ref, g_ref, w_ref, cos_ref, sin_ref, o_ref, xn_ref):
    x = x_ref[...]
    ms = jnp.mean(x * x, axis=-1, keepdims=True)
    xn = (x * lax.rsqrt(ms + NORM_EPS) * g_ref[...]).astype(BF16)
    xn_ref[...] = xn
    acc = jnp.dot(xn, w_ref[...], preferred_element_type=F32)
    cos2 = cos_ref[...]
    sin2 = sin_ref[...]
    for h in range(acc.shape[1] // RET_HEAD_DIM):
        sl = slice(h * RET_HEAD_DIM, (h + 1) * RET_HEAD_DIM)
        o_ref[:, sl] = _rotary(acc[:, sl], cos2, sin2).astype(o_ref.dtype)


def _norm_qk_proj(x, g, w, cos2, sin2, *, n_cols, tm):
    M, K = x.shape
    d = RET_HEAD_DIM
    return pl.pallas_call(
        _norm_qk_kernel,
        grid=(M // tm,),
        in_specs=[pl.BlockSpec((tm, K), lambda i: (i, 0)),
                  pl.BlockSpec((1, K), lambda i: (0, 0)),
                  pl.BlockSpec((K, n_cols), lambda i: (0, 0)),
                  pl.BlockSpec((tm, d), lambda i: (i, 0)),
                  pl.BlockSpec((tm, d), lambda i: (i, 0))],
        out_specs=[pl.BlockSpec((tm, n_cols), lambda i: (i, 0)),
                   pl.BlockSpec((tm, K), lambda i: (i, 0))],
        out_shape=[jax.ShapeDtypeStruct((M, n_cols), BF16), jax.ShapeDtypeStruct((M, K), BF16)],
        compiler_params=_params(("parallel",)),
        name="norm_in_proj_qk",
    )(x, g.reshape(1, K), w, cos2, sin2)


def _mm_kernel(x_ref, w_ref, o_ref, *, epilogue):
    acc = jnp.dot(x_ref[...], w_ref[...], preferred_element_type=F32)
    if epilogue == "sigmoid":
        acc = jax.nn.sigmoid(acc)
        o_ref[...] = acc.astype(o_ref.dtype)
    elif epilogue == "silu_upper_half":
        half = acc.shape[1] // 2
        gate = acc[:, half:]
        o_ref[:, :half] = acc[:, :half].astype(o_ref.dtype)
        o_ref[:, half:] = (gate * jax.nn.sigmoid(gate)).astype(o_ref.dtype)
    else:
        o_ref[...] = acc.astype(o_ref.dtype)


def _matmul(x, w, *, name, epilogue="none", col0=0, n_cols=None, out_dtype, tm, tn):
    M, K = x.shape
    n_cols = w.shape[1] - col0 if n_cols is None else n_cols
    c0 = col0 // tn
    return pl.pallas_call(
        functools.partial(_mm_kernel, epilogue=epilogue),
        grid=(n_cols // tn, M // tm),
        in_specs=[pl.BlockSpec((tm, K), lambda j, i: (i, 0)),
                  pl.BlockSpec((K, tn), lambda j, i: (0, c0 + j))],
        out_specs=pl.BlockSpec((tm, tn), lambda j, i: (i, j)),
        out_shape=jax.ShapeDtypeStruct((M, n_cols), out_dtype),
        compiler_params=_params(("parallel", "arbitrary")),
        name=name,
    )(x, w)


def _mm_shift_kernel(x_ref, w_ref, mu_ref, o_ref, carry_ref, *, tiles_per_seq):
    i = pl.program_id(1)

    @pl.when(i % tiles_per_seq == 0)
    def _():
        carry_ref[...] = jnp.zeros_like(carry_ref)

    z = jnp.dot(x_ref[...], w_ref[...], preferred_element_type=F32)
    n = z.shape[0]
    first = lax.broadcasted_iota(jnp.int32, z.shape, 0) == 0
    z_prev = jnp.where(first, carry_ref[...], pltpu.roll(z, 1, axis=0))
    carry_ref[...] = z[n - 1:n, :]
    o_ref[...] = (z + mu_ref[...] * (z_prev - z)).astype(o_ref.dtype)


def _matmul_shift(x, w, mu, *, seq_len, name, tm, tn):
    M, K = x.shape
    N = w.shape[1]
    return pl.pallas_call(
        functools.partial(_mm_shift_kernel, tiles_per_seq=seq_len // tm),
        grid=(N // tn, M // tm),
        in_specs=[pl.BlockSpec((tm, K), lambda j, i: (i, 0)),
                  pl.BlockSpec((K, tn), lambda j, i: (0, j)),
                  pl.BlockSpec((1, tn), lambda j, i: (0, j))],
        out_specs=pl.BlockSpec((tm, tn), lambda j, i: (i, j)),
        out_shape=jax.ShapeDtypeStruct((M, N), F32),
        scratch_shapes=[pltpu.VMEM((1, tn), F32)],
        compiler_params=_params(("arbitrary", "arbitrary")),
        name=name,
    )(x, w, mu.reshape(1, N))


def _mlp_kernel(*refs, final_norm, cast_splits):
    n_cast_in = len(cast_splits)
    n_cast_out = sum(len(s) - 1 for s in cast_splits)
    n_in = 5 if final_norm else 4
    x_ref, g_ref, wu_ref, wd_ref = refs[:4]
    fg_ref = refs[4] if final_norm else None
    cast_in = refs[n_in:n_in + n_cast_in]
    o_ref = refs[n_in + n_cast_in]
    cast_out = refs[n_in + n_cast_in + 1:n_in + n_cast_in + 1 + n_cast_out]
    xn_ref = refs[-1]
    j = pl.program_id(1)

    k = 0
    for w_ref, splits in zip(cast_in, cast_splits):
        for lo, hi in zip(splits[:-1], splits[1:]):
            cast_out[k][...] = w_ref[:, lo:hi].astype(BF16)
            k += 1

    @pl.when(j == 0)
    def _():
        x = x_ref[...]
        ms = jnp.mean(x * x, axis=-1, keepdims=True)
        xn_ref[...] = (x * lax.rsqrt(ms + NORM_EPS) * g_ref[...]).astype(BF16)
        o_ref[...] = x

    h = jnp.dot(xn_ref[...], wu_ref[...], preferred_element_type=F32)
    h = jnp.square(jnp.maximum(h, 0.0)).astype(BF16)
    o_ref[...] += jnp.dot(h, wd_ref[...], preferred_element_type=F32)

    if final_norm:
        @pl.when(j == pl.num_programs(1) - 1)
        def _():
            y = o_ref[...]
            ms = jnp.mean(y * y, axis=-1, keepdims=True)
            o_ref[...] = y * lax.rsqrt(ms + NORM_EPS) * fg_ref[...]


def _mlp(x, g, w_up, w_down, final_g=None, cast_next=(), *, tm, tf):
    M, D = x.shape
    F = w_up.shape[1]
    final_norm = final_g is not None
    ni, nj = M // tm, F // tf
    vec = pl.BlockSpec((1, D), lambda i, j: (0, 0))
    in_specs = [pl.BlockSpec((tm, D), lambda i, j: (i, 0)), vec,
                pl.BlockSpec((D, tf), lambda i, j: (0, j)),
                pl.BlockSpec((tf, D), lambda i, j: (j, 0))]
    args = [x, g.reshape(1, D), w_up, w_down]
    if final_norm:
        in_specs.append(vec)
        args.append(final_g.reshape(1, D))
    out_specs = [pl.BlockSpec((tm, D), lambda i, j: (i, 0))]
    out_shape = [jax.ShapeDtypeStruct((M, D), F32)]
    cast_splits = []
    for w, layer, splits in cast_next:
        _, R, N = w.shape
        rows = max(BF16_SUBLANES, R // (ni * nj))
        repeat = rows * ni * nj // R
        slab = lambda i, j, layer=layer, repeat=repeat: (layer, (i * nj + j) // repeat, 0)
        in_specs.append(pl.BlockSpec((None, rows, N), slab))
        args.append(w)
        for lo, hi in zip(splits[:-1], splits[1:]):
            out_specs.append(pl.BlockSpec((rows, hi - lo),
                                          lambda i, j, repeat=repeat: ((i * nj + j) // repeat, 0)))
            out_shape.append(jax.ShapeDtypeStruct((R, hi - lo), BF16))
        cast_splits.append(tuple(splits))
    outs = pl.pallas_call(
        functools.partial(_mlp_kernel, final_norm=final_norm, cast_splits=tuple(cast_splits)),
        grid=(ni, nj),
        in_specs=in_specs,
        out_specs=out_specs,
        out_shape=out_shape,
        scratch_shapes=[pltpu.VMEM((tm, D), BF16)],
        compiler_params=_params(("parallel", "arbitrary")),
        name="mlp_final" if final_norm else "mlp",
    )(*args)
    return outs[0], outs[1:]


def _merge_out_kernel(ya_ref, yb_ref, wa_ref, wb_ref, ga_ref, gb_ref, wo_ref, x_ref, o_ref):
    D = o_ref.shape[1]
    n_split = 2
    width = D // n_split
    ya = ya_ref[...]
    yb = yb_ref[...]
    acc = x_ref[...]
    for c in range(n_split):
        cs = slice(c * width, (c + 1) * width)
        a = jnp.dot(ya, wa_ref[:, cs], preferred_element_type=F32)
        b = jnp.dot(yb, wb_ref[:, cs], preferred_element_type=F32)
        m = (ga_ref[:, cs].astype(F32) * a + gb_ref[:, cs].astype(F32) * b).astype(BF16)
        acc = acc + jnp.dot(m, wo_ref[cs, :], preferred_element_type=F32)
    o_ref[...] = acc


def _merge_out(ya, yb, wa, wb, gates, wo, x, *, tm):
    M, K = ya.shape
    D = wo.shape[1]
    resident = lambda a: pl.BlockSpec(a.shape, lambda i: (0, 0), pipeline_mode=pl.Buffered(1))
    return pl.pallas_call(
        _merge_out_kernel,
        grid=(M // tm,),
        in_specs=[
            pl.BlockSpec((tm, K), lambda i: (i, 0)),
            pl.BlockSpec((tm, K), lambda i: (i, 0)),
            resident(wa), resident(wb),
            pl.BlockSpec((tm, D), lambda i: (i, 0)),
            pl.BlockSpec((tm, D), lambda i: (i, 1)),
            resident(wo),
            pl.BlockSpec((tm, D), lambda i: (i, 0)),
        ],
        out_specs=pl.BlockSpec((tm, D), lambda i: (i, 0)),
        out_shape=jax.ShapeDtypeStruct((M, D), F32),
        compiler_params=_params(("parallel",)),
        name="merge_out_proj",
    )(ya, yb, wa, wb, gates, gates, wo, x)


def _rope_kernel(pos_ref, cos_ref, sin_ref):
    half = RET_HEAD_DIM // 2
    pos = pos_ref[...].astype(F32)
    lane = lax.broadcasted_iota(jnp.int32, (1, RET_HEAD_DIM), 1)
    j = jnp.where(lane < half, lane, lane - half).astype(F32)
    inv_freq = jnp.exp(j * (-math.log(ROPE_BASE) / half))
    ang = pos * inv_freq
    cos_ref[...] = jnp.cos(ang)
    sin_ref[...] = jnp.where(lane < half, -1.0, 1.0) * jnp.sin(ang)


def _rope_tables(positions, *, tm):
    M = positions.size
    pos = positions.reshape(M, 1)
    return pl.pallas_call(
        _rope_kernel,
        grid=(M // tm,),
        in_specs=[pl.BlockSpec((tm, 1), lambda i: (i, 0))],
        out_specs=[pl.BlockSpec((tm, RET_HEAD_DIM), lambda i: (i, 0))] * 2,
        out_shape=[jax.ShapeDtypeStruct((M, RET_HEAD_DIM), F32)] * 2,
        compiler_params=_params(("parallel",)),
        name="rope_tables",
    )(pos)


def _retention_kernel(q_ref, k_ref, v_ref, gate_ref, gng_ref, gnb_ref,
                      o_ref, state_ref, inner_ref, qdec_ref, kdec_ref):
    C = q_ref.shape[1]
    d = RET_HEAD_DIM
    log_gammas = [math.log1p(-(2.0 ** (-5.0 - h))) for h in range(RET_HEADS)]

    @pl.when(pl.program_id(1) == 0)
    def _():
        state_ref[...] = jnp.zeros_like(state_ref)
        row = lax.broadcasted_iota(jnp.int32, (C, C), 0)
        col = lax.broadcasted_iota(jnp.int32, (C, C), 1)
        diff = (row - col).astype(F32)
        rowd = lax.broadcasted_iota(jnp.int32, (C, d), 0).astype(F32)
        scale = d ** -0.5
        for h, lg in enumerate(log_gammas):
            inner_ref[h] = jnp.where(row >= col, jnp.exp(jnp.maximum(diff, 0.0) * lg) * scale, 0.0)
            qdec_ref[h] = jnp.exp((rowd + 1.0) * lg)
            kdec_ref[h] = jnp.exp((C - 1.0 - rowd) * lg) * scale

    heads = range(RET_HEADS)
    sls = [slice(h * d, (h + 1) * d) for h in heads]
    qb = [q_ref[0, :, sl] for sl in sls]
    kb = [k_ref[0, :, sl] for sl in sls]
    vb = [v_ref[0, :, sl] for sl in sls]
    scores = [lax.dot_general(qb[h], kb[h], _NT, preferred_element_type=F32) for h in heads]
    state = [state_ref[h] for h in heads]
    cross = [_bdot(qb[h], state[h]) * qdec_ref[h] for h in heads]
    inner = [_bdot(scores[h] * inner_ref[h], vb[h]) for h in heads]
    for h in heads:
        state_ref[h] = math.exp(C * log_gammas[h]) * state[h] + lax.dot_general(
            (kb[h].astype(F32) * kdec_ref[h]).astype(BF16), vb[h], _TN,
            preferred_element_type=F32)
    for h, sl in zip(heads, sls):
        y = inner[h] + cross[h]
        mean = jnp.mean(y, axis=-1, keepdims=True)
        yc = y - mean
        var = jnp.mean(yc * yc, axis=-1, keepdims=True)
        yn = yc * lax.rsqrt(var + RET_GN_EPS) * gng_ref[:, sl] + gnb_ref[:, sl]
        o_ref[0, :, sl] = (gate_ref[0, :, sl].astype(F32) * yn).astype(o_ref.dtype)


def _retention(qk, vg, gn_g, gn_b):
    B, S, _ = qk.shape
    W = RET_HEADS * RET_HEAD_DIM
    C = RET_CHUNK
    zspec = lambda off: pl.BlockSpec((1, C, W), lambda b, n: (b, n, off))
    pspec = pl.BlockSpec((1, W), lambda b, n: (0, 0))
    return pl.pallas_call(
        _retention_kernel,
        grid=(B, S // C),
        in_specs=[zspec(0), zspec(1), zspec(0), zspec(1), pspec, pspec],
        out_specs=pl.BlockSpec((1, C, W), lambda b, n: (b, n, 0)),
        out_shape=jax.ShapeDtypeStruct((B, S, W), BF16),
        scratch_shapes=[pltpu.VMEM((RET_HEADS, RET_HEAD_DIM, RET_HEAD_DIM), F32),
                        pltpu.VMEM((RET_HEADS, C, C), F32),
                        pltpu.VMEM((RET_HEADS, C, RET_HEAD_DIM), F32),
                        pltpu.VMEM((RET_HEADS, C, RET_HEAD_DIM), F32)],
        compiler_params=_params(("parallel", "arbitrary")),
        name="retention",
    )(qk, qk, vg, vg, gn_g.reshape(1, W), gn_b.reshape(1, W))


def _split2(x):
    hi = x.astype(BF16)
    return hi, (x - hi.astype(F32)).astype(BF16)


def _rwkv_staged_kernel(r_ref, k_ref, v_ref, lo_ref,
                        w0_ref, wup_ref, a0_ref, aup_ref, gup_ref, kk_ref, ka_ref, rk_ref,
                        lng_ref, lnb_ref, o_ref, state_ref):
    R = r_ref.shape[1]
    W = r_ref.shape[2]
    C = RWKV_CHUNK
    hd = RWKV_HEAD_DIM
    P = 2 * hd
    assert P == LANES and 2 * C == P and R % C == 0
    chunks = range(R // C)
    rows = [slice(c * C, (c + 1) * C) for c in chunks]
    pairs = range(W // P)
    sls = [slice(p * P, (p + 1) * P) for p in pairs]

    @pl.when(pl.program_id(1) == 0)
    def _():
        state_ref[...] = jnp.zeros_like(state_ref)

    lo = lo_ref[0]
    tw = jnp.tanh(lo[:, 0:LORA_W]).astype(BF16)
    za = lo[:, LORA_W:LORA_W + LORA_A].astype(BF16)
    sg = jax.nn.sigmoid(lo[:, LORA_W + LORA_A:]).astype(BF16)

    r = r_ref[0]
    k = k_ref[0]
    v = v_ref[0]
    logdec = -math.exp(-0.5) * jax.nn.sigmoid(
        w0_ref[...] + jnp.dot(tw, wup_ref[...], preferred_element_type=F32))
    a = jax.nn.sigmoid(a0_ref[...] + jnp.dot(za, aup_ref[...], preferred_element_type=F32))
    g = jnp.dot(sg, gup_ref[...], preferred_element_type=F32)
    kk_raw = k * kk_ref[...]
    k = k * (1.0 + (a - 1.0) * ka_ref[...])

    tri = jnp.where(lax.broadcasted_iota(jnp.int32, (C, C), 0)
                    >= lax.broadcasted_iota(jnp.int32, (C, C), 1), 1.0, 0.0).astype(BF16)
    ld_hi = logdec.astype(BF16)
    ld_mid, ld_lo = _split2(logdec - ld_hi.astype(F32))
    cum = [jnp.dot(tri, ld_hi[rc], preferred_element_type=F32)
           + jnp.dot(tri, ld_mid[rc], preferred_element_type=F32)
           + jnp.dot(tri, ld_lo[rc], preferred_element_type=F32) for rc in rows]

    row = lax.broadcasted_iota(jnp.int32, (C, P), 0)
    lane = lax.broadcasted_iota(jnp.int32, (C, P), 1)
    head0 = lane < hd
    cj = jnp.where(head0, lane, lane - hd)
    strict = row > cj
    incl = row >= cj
    eye2 = jnp.where(row == cj, 1.0, 0.0)
    r2 = lax.broadcasted_iota(jnp.int32, (P, P), 0)
    l2 = lax.broadcasted_iota(jnp.int32, (P, P), 1)
    bdmask = (r2 < hd) == (l2 < hd)
    zero = jnp.zeros((), BF16)

    def blockdiag(x):
        return jnp.where(bdmask, jnp.concatenate([x, x], axis=0), jnp.zeros((), x.dtype))

    def segsum(x):
        h0 = lax.broadcasted_iota(jnp.int32, x.shape, 1) < hd
        s0 = jnp.sum(jnp.where(h0, x, 0.0), axis=-1, keepdims=True)
        s1 = jnp.sum(jnp.where(h0, 0.0, x), axis=-1, keepdims=True)
        return jnp.where(h0, s0, s1)

    kk_sq = [segsum(jnp.square(kk_raw[:, sl])) for sl in sls]
    bonus = [segsum(r[:, sl] * k[:, sl] * rk_ref[:, sl]) for sl in sls]

    lhs, lv, m_rb, tb, bk_end, dec_chunk, vb = ({} for _ in range(7))

    def independent(group):
        keys = [(c, p) for c in group for p in pairs]
        gram = {}
        for q in keys:
            rc, sl = rows[q[0]], sls[q[1]]
            kk = kk_raw[rc, sl] * lax.rsqrt(jnp.maximum(kk_sq[q[1]][rc], 1e-24))
            cum_p = cum[q[0]][:, sl]
            cum_last = cum_p[C - 1:C, :]
            dec_to_end = jnp.exp(cum_last - cum_p)
            inv_p = jnp.exp(-cum_p)
            kka = kk * a[rc, sl]
            rt = r[rc, sl] * jnp.exp(cum_p)
            kt = (k[rc, sl] * inv_p).astype(BF16)
            at = -kk * jnp.exp(cum_p - logdec[rc, sl])
            bt = (kka * inv_p).astype(BF16)
            bk_end[q] = jnp.concatenate([kka * dec_to_end, k[rc, sl] * dec_to_end],
                                        axis=0).astype(BF16)
            dec_chunk[q] = jnp.exp(cum_last)
            vb[q] = v[rc, sl].astype(BF16)
            lhs[q] = jnp.concatenate([at, rt], axis=0).astype(BF16)
            rhs = jnp.concatenate([jnp.where(head0, bt, zero), jnp.where(head0, zero, bt),
                                   jnp.where(head0, kt, zero), jnp.where(head0, zero, kt)], axis=0)
            gram[q] = lax.dot_general(lhs[q], rhs, _NT, preferred_element_type=F32)
        yield

        l_ab = {q: jnp.where(strict, gram[q][:C, :P], 0.0) for q in keys}
        for q in keys:
            l_akrk = jnp.concatenate([jnp.where(strict, gram[q][:C, P:], 0.0),
                                      jnp.where(incl, gram[q][C:, P:], 0.0)], axis=0)
            lv[q] = jnp.dot(l_akrk.astype(BF16), blockdiag(vb[q]), preferred_element_type=F32)
            m_rb[q] = jnp.where(incl, gram[q][C:, :P], 0.0).astype(BF16)
        yield

        t = {q: eye2 + l_ab[q] for q in keys}
        m = {q: _bdot(l_ab[q], blockdiag(l_ab[q].astype(BF16))) for q in keys}
        yield
        n_sq = int(math.log2(C)) - 1
        for step in range(n_sq):
            for q in keys:
                mb = m[q].astype(BF16)
                if step < n_sq - 1:
                    both = jnp.concatenate([blockdiag(t[q].astype(BF16)), blockdiag(mb)], axis=1)
                    prod = jnp.dot(mb, both, preferred_element_type=F32)
                    t[q] = t[q] + prod[:, :P]
                    m[q] = prod[:, P:]
                else:
                    t[q] = t[q] + jnp.dot(mb, blockdiag(t[q].astype(BF16)),
                                          preferred_element_type=F32)
            yield
        for q in keys:
            tb[q] = t[q].astype(BF16)

    state = [state_ref[p] for p in pairs]

    def dependent(group):
        for c in group:
            rc = rows[c]
            ar = [lax.dot_general(lhs[c, p], state[p].astype(BF16), _NT,
                                  preferred_element_type=F32) for p in pairs]
            yield
            ub = [jnp.dot(tb[c, p], blockdiag((ar[p][:C] + lv[c, p][:C]).astype(BF16)),
                          preferred_element_type=F32).astype(BF16) for p in pairs]
            yield
            y = [ar[p][C:] + lv[c, p][C:]
                 + jnp.dot(m_rb[c, p], blockdiag(ub[p]), preferred_element_type=F32) for p in pairs]
            for p in pairs:
                upd = lax.dot_general(jnp.concatenate([ub[p], vb[c, p]], axis=0), bk_end[c, p],
                                      _TN, preferred_element_type=F32)
                state[p] = state[p] * dec_chunk[c, p] + jnp.where(bdmask, upd, 0.0)
            yield
            yc = [y[p] - segsum(y[p]) * (1.0 / hd) for p in pairs]
            var = [segsum(jnp.square(yc[p])) * (1.0 / hd) for p in pairs]
            for p, sl in zip(pairs, sls):
                yn = yc[p] * lax.rsqrt(var[p] + RWKV_LN_EPS) * lng_ref[:, sl] + lnb_ref[:, sl]
                o_ref[0, rc, sl] = ((yn + bonus[p][rc] * v[rc, sl]) * g[rc, sl]).astype(o_ref.dtype)
            yield

    def run(*stages):
        stages = list(stages)
        while stages:
            for stage in tuple(stages):
                if next(stage, stages) is stages:
                    stages.remove(stage)

    groups = [list(chunks[i:i + RWKV_GROUP]) for i in range(0, len(chunks), RWKV_GROUP)]
    run(independent(groups[0]))
    for gi, group in enumerate(groups):
        ahead = [independent(groups[gi + 1])] if gi + 1 < len(groups) else []
        run(dependent(group), *ahead)
    for p in pairs:
        state_ref[p] = state[p]


def _rwkv(z, w0, w_up, a0, a_up, g_up, k_k, k_a, r_k, ln_g, ln_b):
    B, S, _ = z.shape
    W = w0.shape[-1]
    C = RWKV_ROWS
    LO = LORA_W + LORA_A + LORA_G
    P = 2 * RWKV_HEAD_DIM
    lo0 = 3 * W // LO
    zspec = lambda off: pl.BlockSpec((1, C, W), lambda b, n: (b, n, off))
    full = lambda a: pl.BlockSpec(a.shape, lambda b, n: (0,) * a.ndim)
    row = lambda a: a.reshape(1, -1)
    params = [row(w0), w_up.astype(BF16), row(a0), a_up.astype(BF16), g_up.astype(BF16),
              row(k_k), row(k_a), row(r_k), row(ln_g), row(ln_b)]
    return pl.pallas_call(
        _rwkv_staged_kernel,
        grid=(B, S // C),
        in_specs=[zspec(0), zspec(1), zspec(2),
                  pl.BlockSpec((1, C, LO), lambda b, n: (b, n, lo0))] + [full(a) for a in params],
        out_specs=pl.BlockSpec((1, C, W), lambda b, n: (b, n, 0)),
        out_shape=jax.ShapeDtypeStruct((B, S, W), BF16),
        scratch_shapes=[pltpu.VMEM((W // P, P, P), F32)],
        compiler_params=_params(("parallel", "arbitrary")),
        name="rwkv7",
    )(z, z, z, z, *params)


def kernel(x, positions, norm1_g, w_in, ret_gn_g, ret_gn_b, rwkv_mu, rwkv_w0, rwkv_w_up, rwkv_a0, rwkv_a_up, rwkv_g_up, rwkv_k_k, rwkv_k_a, rwkv_r_k, rwkv_ln_g, rwkv_ln_b, w_branch_a, w_branch_b, w_out, norm2_g, mlp_up, mlp_down, final_g):
    B, S, D = x.shape
    M = B * S
    depth = w_in.shape[0]
    ret_w = ret_gn_g.shape[-1]
    rwkv_w = rwkv_w0.shape[-1]
    ret_cols = 4 * ret_w
    rwkv_cols = rwkv_mu.shape[-1]
    cos2, sin2 = _rope_tables(positions, tm=2048)

    in_splits = (0, ret_cols, ret_cols + rwkv_cols, w_in.shape[-1])
    w0 = w_in[0]
    wts = [w0[:, in_splits[i]:in_splits[i + 1]].astype(BF16) for i in range(3)]
    wts += [t[0].astype(BF16) for t in (w_branch_a, w_branch_b, w_out, mlp_up, mlp_down)]

    xf = x.reshape(M, D)
    for l in range(depth):
        w_ret, w_rwkv, w_gate, w_a, w_b, w_o, w_up, w_down = wts
        z_qk, xn = _norm_qk_proj(xf, norm1_g[l], w_ret, cos2, sin2, n_cols=2 * ret_w, tm=512)
        z_vg = _matmul(xn, w_ret, name="in_proj_vg", epilogue="silu_upper_half",
                       col0=2 * ret_w, out_dtype=BF16, tm=1024, tn=2 * ret_w)
        z_gate = _matmul(xn, w_gate, name="in_proj_gate", epilogue="sigmoid", out_dtype=BF16,
                         tm=1024, tn=2048)
        z_rwkv = _matmul_shift(xn, w_rwkv, rwkv_mu[l], seq_len=S, name="in_proj_rwkv",
                               tm=1024, tn=rwkv_cols // 2)
        y_ret = _retention(z_qk.reshape(B, S, -1), z_vg.reshape(B, S, -1),
                           ret_gn_g[l], ret_gn_b[l])
        y_rwkv = _rwkv(z_rwkv.reshape(B, S, -1), rwkv_w0[l], rwkv_w_up[l],
                       rwkv_a0[l], rwkv_a_up[l], rwkv_g_up[l], rwkv_k_k[l], rwkv_k_a[l],
                       rwkv_r_k[l], rwkv_ln_g[l], rwkv_ln_b[l])
        xf = _merge_out(y_ret.reshape(M, ret_w), y_rwkv.reshape(M, rwkv_w),
                        w_a, w_b, z_gate, w_o, xf, tm=512)
        if l + 1 < depth:
            full = lambda t: (t, l + 1, (0, t.shape[-1]))
            cast_next = [(w_in, l + 1, in_splits), full(w_branch_a), full(w_branch_b),
                         full(w_out), full(mlp_up), full(mlp_down)]
            xf, wts = _mlp(xf, norm2_g[l], w_up, w_down, cast_next=cast_next, tm=1024, tf=512)
        else:
            xf, _ = _mlp(xf, norm2_g[l], w_up, w_down, final_g, tm=1024, tf=512)
    return xf.reshape(B, S, D)
```

```python
import functools
import math

import jax
import jax.numpy as jnp
from jax import lax
from jax.experimental import pallas as pl
from jax.experimental.pallas import tpu as pltpu

F32 = jnp.float32
BF16 = jnp.bfloat16

RET_HEADS = 8
RET_HEAD_DIM = 128
RET_GN_EPS = 1e-5
ROPE_BASE = 10000.0
RWKV_HEAD_DIM = 64
RWKV_LN_EPS = 64e-5
NORM_EPS = 1e-6
LORA_W = 64
LORA_A = 64
LORA_G = 128

LANES = 128
BF16_SUBLANES = 16
V7X_VMEM_LIMIT_BYTES = 56 * 1024 * 1024

RET_CHUNK = 128
RET_ROWS = 256
RWKV_CHUNK = 64
RWKV_ROWS = 256
RWKV_GROUP = 2

_NT = (((1,), (1,)), ((), ()))
_TN = (((0,), (0,)), ((), ()))


def _params(semantics):
    return pltpu.CompilerParams(dimension_semantics=semantics,
                                vmem_limit_bytes=V7X_VMEM_LIMIT_BYTES)


def _bdot(a, b):
    return jnp.dot(a.astype(BF16), b.astype(BF16), preferred_element_type=F32)


def _side_cast_specs(casts, n_steps, step_of):
    in_specs, args, out_specs, out_shape, all_splits = [], [], [], [], []
    for w, layer, splits in casts:
        _, R, N = w.shape
        rows = max(BF16_SUBLANES, R // n_steps)
        repeat = rows * n_steps // R
        in_specs.append(pl.BlockSpec(
            (None, rows, N),
            lambda *g, layer=layer, repeat=repeat: (layer, step_of(*g) // repeat, 0)))
        args.append(w)
        for lo, hi in zip(splits[:-1], splits[1:]):
            out_specs.append(pl.BlockSpec(
                (rows, hi - lo), lambda *g, repeat=repeat: (step_of(*g) // repeat, 0)))
            out_shape.append(jax.ShapeDtypeStruct((R, hi - lo), BF16))
        all_splits.append(tuple(splits))
    return in_specs, args, out_specs, out_shape, tuple(all_splits)


def _side_cast(cast_in, cast_out, cast_splits):
    k = 0
    for w_ref, splits in zip(cast_in, cast_splits):
        for lo, hi in zip(splits[:-1], splits[1:]):
            cast_out[k][...] = w_ref[:, lo:hi].astype(BF16)
            k += 1


def _rotary(t, cos2, sin2):
    return t * cos2 + pltpu.roll(t, RET_HEAD_DIM // 2, axis=1) * sin2


def _norm_qk_kernel(x_ref, g_ref, w_ref, cos_ref, sin_ref, o_ref, xn_ref):
    x = x_ref[...]
    ms = jnp.mean(x * x, axis=-1, keepdims=True)
    xn = (x * lax.rsqrt(ms + NORM_EPS) * g_ref[...]).astype(BF16)
    xn_ref[...] = xn
    acc = jnp.dot(xn, w_ref[...], preferred_element_type=F32)
    cos2 = cos_ref[...]
    sin2 = sin_ref[...]
    for h in range(acc.shape[1] // RET_HEAD_DIM):
        sl = slice(h * RET_HEAD_DIM, (h + 1) * RET_HEAD_DIM)
        o_ref[:, sl] = _rotary(acc[:, sl], cos2, sin2).astype(o_ref.dtype)


def _norm_qk_proj(x, g, w, cos2, sin2, *, n_cols, tm):
    M, K = x.shape
    d = RET_HEAD_DIM
    return pl.pallas_call(
        _norm_qk_kernel,
        grid=(M // tm,),
        in_specs=[pl.BlockSpec((tm, K), lambda i: (i, 0)),
                  pl.BlockSpec((1, K), lambda i: (0, 0)),
                  pl.BlockSpec((K, n_cols), lambda i: (0, 0)),
                  pl.BlockSpec((tm, d), lambda i: (i, 0)),
                  pl.BlockSpec((tm, d), lambda i: (i, 0))],
        out_specs=[pl.BlockSpec((tm, n_cols), lambda i: (i, 0)),
                   pl.BlockSpec((tm, K), lambda i: (i, 0))],
        out_shape=[jax.ShapeDtypeStruct((M, n_cols), BF16), jax.ShapeDtypeStruct((M, K), BF16)],
        compiler_params=_params(("parallel",)),
        name="norm_in_proj_qk",
    )(x, g.reshape(1, K), w, cos2, sin2)


def _mm_kernel(*refs, epilogue, cast_splits):
    n_cast_in = len(cast_splits)
    x_ref, w_ref = refs[:2]
    cast_in = refs[2:2 + n_cast_in]
    o_ref = refs[2 + n_cast_in]
    cast_out = refs[3 + n_cast_in:]
    _side_cast(cast_in, cast_out, cast_splits)
    acc = jnp.dot(x_ref[...], w_ref[...], preferred_element_type=F32)
    if epilogue == "sigmoid":
        acc = jax.nn.sigmoid(acc)
        o_ref[...] = acc.astype(o_ref.dtype)
    elif epilogue == "silu_upper_half":
        half = acc.shape[1] // 2
        gate = acc[:, half:]
        o_ref[:, :half] = acc[:, :half].astype(o_ref.dtype)
        o_ref[:, half:] = (gate * jax.nn.sigmoid(gate)).astype(o_ref.dtype)
    else:
        o_ref[...] = acc.astype(o_ref.dtype)


def _matmul(x, w, *, name, epilogue="none", col0=0, n_cols=None, casts=(), out_dtype, tm, tn):
    M, K = x.shape
    n_cols = w.shape[1] - col0 if n_cols is None else n_cols
    c0 = col0 // tn
    nj, ni = n_cols // tn, M // tm
    c_in, c_args, c_out, c_shape, cast_splits = _side_cast_specs(
        casts, ni * nj, lambda j, i: j * ni + i)
    outs = pl.pallas_call(
        functools.partial(_mm_kernel, epilogue=epilogue, cast_splits=cast_splits),
        grid=(nj, ni),
        in_specs=[pl.BlockSpec((tm, K), lambda j, i: (i, 0)),
                  pl.BlockSpec((K, tn), lambda j, i: (0, c0 + j))] + c_in,
        out_specs=[pl.BlockSpec((tm, tn), lambda j, i: (i, j))] + c_out,
        out_shape=[jax.ShapeDtypeStruct((M, n_cols), out_dtype)] + c_shape,
        compiler_params=_params(("parallel", "arbitrary")),
        name=name,
    )(x, w, *c_args)
    return outs[0], outs[1:]


def _mm_shift_kernel(x_ref, w_ref, mu_ref, o_ref, carry_ref, *, tiles_per_seq):
    i = pl.program_id(1)

    @pl.when(i % tiles_per_seq == 0)
    def _():
        carry_ref[...] = jnp.zeros_like(carry_ref)

    z = jnp.dot(x_ref[...], w_ref[...], preferred_element_type=F32)
    n = z.shape[0]
    first = lax.broadcasted_iota(jnp.int32, z.shape, 0) == 0
    z_prev = jnp.where(first, carry_ref[...], pltpu.roll(z, 1, axis=0))
    carry_ref[...] = z[n - 1:n, :]
    o_ref[...] = (z + mu_ref[...] * (z_prev - z)).astype(o_ref.dtype)


def _matmul_shift(x, w, mu, *, seq_len, name, tm, tn):
    M, K = x.shape
    N = w.shape[1]
    return pl.pallas_call(
        functools.partial(_mm_shift_kernel, tiles_per_seq=seq_len // tm),
        grid=(N // tn, M // tm),
        in_specs=[pl.BlockSpec((tm, K), lambda j, i: (i, 0)),
                  pl.BlockSpec((K, tn), lambda j, i: (0, j)),
                  pl.BlockSpec((1, tn), lambda j, i: (0, j))],
        out_specs=pl.BlockSpec((tm, tn), lambda j, i: (i, j)),
        out_shape=jax.ShapeDtypeStruct((M, N), F32),
        scratch_shapes=[pltpu.VMEM((1, tn), F32)],
        compiler_params=_params(("arbitrary", "arbitrary")),
        name=name,
    )(x, w, mu.reshape(1, N))


def _mlp_kernel(*refs, final_norm, cast_splits):
    n_cast_in = len(cast_splits)
    n_in = 5 if final_norm else 4
    x_ref, g_ref, wu_ref, wd_ref = refs[:4]
    fg_ref = refs[4] if final_norm else None
    cast_in = refs[n_in:n_in + n_cast_in]
    o_ref = refs[n_in + n_cast_in]
    cast_out = refs[n_in + n_cast_in + 1:-1]
    xn_ref = refs[-1]
    j = pl.program_id(1)

    @pl.when(j == 0)
    def _():
        x = x_ref[...]
        ms = jnp.mean(x * x, axis=-1, keepdims=True)
        xn_ref[...] = (x * lax.rsqrt(ms + NORM_EPS) * g_ref[...]).astype(BF16)
        o_ref[...] = x

    _side_cast(cast_in, cast_out, cast_splits)
    h = jnp.dot(xn_ref[...], wu_ref[...], preferred_element_type=F32)
    h = jnp.square(jnp.maximum(h, 0.0)).astype(BF16)
    o_ref[...] += jnp.dot(h, wd_ref[...], preferred_element_type=F32)

    if final_norm:
        @pl.when(j == pl.num_programs(1) - 1)
        def _():
            y = o_ref[...]
            ms = jnp.mean(y * y, axis=-1, keepdims=True)
            o_ref[...] = y * lax.rsqrt(ms + NORM_EPS) * fg_ref[...]


def _mlp(x, g, w_up, w_down, final_g=None, cast_next=(), *, tm, tf):
    M, D = x.shape
    F = w_up.shape[1]
    final_norm = final_g is not None
    ni, nj = M // tm, F // tf
    vec = pl.BlockSpec((1, D), lambda i, j: (0, 0))
    in_specs = [pl.BlockSpec((tm, D), lambda i, j: (i, 0)), vec,
                pl.BlockSpec((D, tf), lambda i, j: (0, j)),
                pl.BlockSpec((tf, D), lambda i, j: (j, 0))]
    args = [x, g.reshape(1, D), w_up, w_down]
    if final_norm:
        in_specs.append(vec)
        args.append(final_g.reshape(1, D))
    c_in, c_args, c_out, c_shape, cast_splits = _side_cast_specs(
        cast_next, ni * nj, lambda i, j: i * nj + j)
    outs = pl.pallas_call(
        functools.partial(_mlp_kernel, final_norm=final_norm, cast_splits=cast_splits),
        grid=(ni, nj),
        in_specs=in_specs + c_in,
        out_specs=[pl.BlockSpec((tm, D), lambda i, j: (i, 0))] + c_out,
        out_shape=[jax.ShapeDtypeStruct((M, D), F32)] + c_shape,
        scratch_shapes=[pltpu.VMEM((tm, D), BF16)],
        compiler_params=_params(("parallel", "arbitrary")),
        name="mlp_final" if final_norm else "mlp",
    )(*args, *c_args)
    return outs[0], outs[1:]


def _merge_out_kernel(ya_ref, yb_ref, wa_ref, wb_ref, ga_ref, gb_ref, wo_ref, x_ref, o_ref):
    D = o_ref.shape[1]
    n_split = 2
    width = D // n_split
    ya = ya_ref[...]
    yb = yb_ref[...]
    acc = x_ref[...]
    for c in range(n_split):
        cs = slice(c * width, (c + 1) * width)
        a = jnp.dot(ya, wa_ref[:, cs], preferred_element_type=F32)
        b = jnp.dot(yb, wb_ref[:, cs], preferred_element_type=F32)
        m = (ga_ref[:, cs].astype(F32) * a + gb_ref[:, cs].astype(F32) * b).astype(BF16)
        acc = acc + jnp.dot(m, wo_ref[cs, :], preferred_element_type=F32)
    o_ref[...] = acc


def _merge_out(ya, yb, wa, wb, gates, wo, x, *, tm):
    M, K = ya.shape
    D = wo.shape[1]
    resident = lambda a: pl.BlockSpec(a.shape, lambda i: (0, 0), pipeline_mode=pl.Buffered(1))
    return pl.pallas_call(
        _merge_out_kernel,
        grid=(M // tm,),
        in_specs=[
            pl.BlockSpec((tm, K), lambda i: (i, 0)),
            pl.BlockSpec((tm, K), lambda i: (i, 0)),
            resident(wa), resident(wb),
            pl.BlockSpec((tm, D), lambda i: (i, 0)),
            pl.BlockSpec((tm, D), lambda i: (i, 1)),
            resident(wo),
            pl.BlockSpec((tm, D), lambda i: (i, 0)),
        ],
        out_specs=pl.BlockSpec((tm, D), lambda i: (i, 0)),
        out_shape=jax.ShapeDtypeStruct((M, D), F32),
        compiler_params=_params(("parallel",)),
        name="merge_out_proj",
    )(ya, yb, wa, wb, gates, gates, wo, x)


def _rope_kernel(pos_ref, cos_ref, sin_ref):
    half = RET_HEAD_DIM // 2
    pos = pos_ref[...].astype(F32)
    lane = lax.broadcasted_iota(jnp.int32, (1, RET_HEAD_DIM), 1)
    j = jnp.where(lane < half, lane, lane - half).astype(F32)
    inv_freq = jnp.exp(j * (-math.log(ROPE_BASE) / half))
    ang = pos * inv_freq
    cos_ref[...] = jnp.cos(ang)
    sin_ref[...] = jnp.where(lane < half, -1.0, 1.0) * jnp.sin(ang)


def _rope_tables(positions, *, tm):
    M = positions.size
    pos = positions.reshape(M, 1)
    return pl.pallas_call(
        _rope_kernel,
        grid=(M // tm,),
        in_specs=[pl.BlockSpec((tm, 1), lambda i: (i, 0))],
        out_specs=[pl.BlockSpec((tm, RET_HEAD_DIM), lambda i: (i, 0))] * 2,
        out_shape=[jax.ShapeDtypeStruct((M, RET_HEAD_DIM), F32)] * 2,
        compiler_params=_params(("parallel",)),
        name="rope_tables",
    )(pos)


def _retention_kernel(q_ref, k_ref, v_ref, gate_ref, gng_ref, gnb_ref,
                      o_ref, state_ref, inner_ref, qdec_ref, kdec_ref):
    C = RET_CHUNK
    d = RET_HEAD_DIM
    chunks = [slice(c * C, (c + 1) * C) for c in range(q_ref.shape[1] // C)]
    log_gammas = [math.log1p(-(2.0 ** (-5.0 - h))) for h in range(RET_HEADS)]

    @pl.when(pl.program_id(1) == 0)
    def _():
        state_ref[...] = jnp.zeros_like(state_ref)
        row = lax.broadcasted_iota(jnp.int32, (C, C), 0)
        col = lax.broadcasted_iota(jnp.int32, (C, C), 1)
        diff = (row - col).astype(F32)
        rowd = lax.broadcasted_iota(jnp.int32, (C, d), 0).astype(F32)
        scale = d ** -0.5
        for h, lg in enumerate(log_gammas):
            inner_ref[h] = jnp.where(row >= col, jnp.exp(jnp.maximum(diff, 0.0) * lg) * scale, 0.0)
            qdec_ref[h] = jnp.exp((rowd + 1.0) * lg)
            kdec_ref[h] = jnp.exp((C - 1.0 - rowd) * lg) * scale

    heads = range(RET_HEADS)
    sls = [slice(h * d, (h + 1) * d) for h in heads]
    keys = [(rc, h) for rc in chunks for h in heads]
    qb = {(rc, h): q_ref[0, rc, sls[h]] for rc, h in keys}
    kb = {(rc, h): k_ref[0, rc, sls[h]] for rc, h in keys}
    vb = {(rc, h): v_ref[0, rc, sls[h]] for rc, h in keys}
    scores = {q: lax.dot_general(qb[q], kb[q], _NT, preferred_element_type=F32) for q in keys}
    inner = {q: _bdot(scores[q] * inner_ref[q[1]], vb[q]) for q in keys}
    kv = {q: lax.dot_general((kb[q].astype(F32) * kdec_ref[q[1]]).astype(BF16), vb[q], _TN,
                             preferred_element_type=F32) for q in keys}
    state = [state_ref[h] for h in heads]
    cross = {}
    for rc in chunks:
        for h in heads:
            cross[rc, h] = _bdot(qb[rc, h], state[h]) * qdec_ref[h]
            state[h] = math.exp(C * log_gammas[h]) * state[h] + kv[rc, h]
    for h in heads:
        state_ref[h] = state[h]
    for rc, h in keys:
        sl = sls[h]
        y = inner[rc, h] + cross[rc, h]
        mean = jnp.mean(y, axis=-1, keepdims=True)
        yc = y - mean
        var = jnp.mean(yc * yc, axis=-1, keepdims=True)
        yn = yc * lax.rsqrt(var + RET_GN_EPS) * gng_ref[:, sl] + gnb_ref[:, sl]
        o_ref[0, rc, sl] = (gate_ref[0, rc, sl].astype(F32) * yn).astype(o_ref.dtype)


def _retention(qk, vg, gn_g, gn_b):
    B, S, _ = qk.shape
    W = RET_HEADS * RET_HEAD_DIM
    C = RET_CHUNK
    R = RET_ROWS
    zspec = lambda off: pl.BlockSpec((1, R, W), lambda b, n: (b, n, off))
    pspec = pl.BlockSpec((1, W), lambda b, n: (0, 0))
    return pl.pallas_call(
        _retention_kernel,
        grid=(B, S // R),
        in_specs=[zspec(0), zspec(1), zspec(0), zspec(1), pspec, pspec],
        out_specs=pl.BlockSpec((1, R, W), lambda b, n: (b, n, 0)),
        out_shape=jax.ShapeDtypeStruct((B, S, W), BF16),
        scratch_shapes=[pltpu.VMEM((RET_HEADS, RET_HEAD_DIM, RET_HEAD_DIM), F32),
                        pltpu.VMEM((RET_HEADS, C, C), F32),
                        pltpu.VMEM((RET_HEADS, C, RET_HEAD_DIM), F32),
                        pltpu.VMEM((RET_HEADS, C, RET_HEAD_DIM), F32)],
        compiler_params=_params(("parallel", "arbitrary")),
        name="retention",
    )(qk, qk, vg, vg, gn_g.reshape(1, W), gn_b.reshape(1, W))


def _split2(x):
    hi = x.astype(BF16)
    return hi, (x - hi.astype(F32)).astype(BF16)


def _rwkv_staged_kernel(r_ref, k_ref, v_ref, lo_ref,
                        w0_ref, wup_ref, a0_ref, aup_ref, gup_ref, kk_ref, ka_ref, rk_ref,
                        lng_ref, lnb_ref, o_ref, state_ref):
    R = r_ref.shape[1]
    W = r_ref.shape[2]
    C = RWKV_CHUNK
    hd = RWKV_HEAD_DIM
    P = 2 * hd
    assert P == LANES and 2 * C == P and R % C == 0
    chunks = range(R // C)
    rows = [slice(c * C, (c + 1) * C) for c in chunks]
    pairs = range(W // P)
    sls = [slice(p * P, (p + 1) * P) for p in pairs]

    @pl.when(pl.program_id(1) == 0)
    def _():
        state_ref[...] = jnp.zeros_like(state_ref)

    lo = lo_ref[0]
    tw = jnp.tanh(lo[:, 0:LORA_W]).astype(BF16)
    za = lo[:, LORA_W:LORA_W + LORA_A].astype(BF16)
    sg = jax.nn.sigmoid(lo[:, LORA_W + LORA_A:]).astype(BF16)

    r = r_ref[0]
    k = k_ref[0]
    v = v_ref[0]
    logdec = -math.exp(-0.5) * jax.nn.sigmoid(
        w0_ref[...] + jnp.dot(tw, wup_ref[...], preferred_element_type=F32))
    a = jax.nn.sigmoid(a0_ref[...] + jnp.dot(za, aup_ref[...], preferred_element_type=F32))
    g = jnp.dot(sg, gup_ref[...], preferred_element_type=F32)
    kk_raw = k * kk_ref[...]
    k = k * (1.0 + (a - 1.0) * ka_ref[...])

    tri = jnp.where(lax.broadcasted_iota(jnp.int32, (C, C), 0)
                    >= lax.broadcasted_iota(jnp.int32, (C, C), 1), 1.0, 0.0).astype(BF16)
    ld_hi = logdec.astype(BF16)
    ld_mid, ld_lo = _split2(logdec - ld_hi.astype(F32))
    cum = [jnp.dot(tri, ld_hi[rc], preferred_element_type=F32)
           + jnp.dot(tri, ld_mid[rc], preferred_element_type=F32)
           + jnp.dot(tri, ld_lo[rc], preferred_element_type=F32) for rc in rows]

    row = lax.broadcasted_iota(jnp.int32, (C, P), 0)
    lane = lax.broadcasted_iota(jnp.int32, (C, P), 1)
    head0 = lane < hd
    cj = jnp.where(head0, lane, lane - hd)
    strict = row > cj
    incl = row >= cj
    eye2 = jnp.where(row == cj, 1.0, 0.0)
    r2 = lax.broadcasted_iota(jnp.int32, (P, P), 0)
    l2 = lax.broadcasted_iota(jnp.int32, (P, P), 1)
    bdmask = (r2 < hd) == (l2 < hd)
    zero = jnp.zeros((), BF16)

    def blockdiag(x):
        return jnp.where(bdmask, jnp.concatenate([x, x], axis=0), jnp.zeros((), x.dtype))

    def segsum(x):
        h0 = lax.broadcasted_iota(jnp.int32, x.shape, 1) < hd
        s0 = jnp.sum(jnp.where(h0, x, 0.0), axis=-1, keepdims=True)
        s1 = jnp.sum(jnp.where(h0, 0.0, x), axis=-1, keepdims=True)
        return jnp.where(h0, s0, s1)

    kk_sq = [segsum(jnp.square(kk_raw[:, sl])) for sl in sls]
    bonus = [segsum(r[:, sl] * k[:, sl] * rk_ref[:, sl]) for sl in sls]

    lhs, lv, m_rb, tb, bk_end, dec_chunk, vb = ({} for _ in range(7))

    def independent(group):
        keys = [(c, p) for c in group for p in pairs]
        gram = {}
        for q in keys:
            rc, sl = rows[q[0]], sls[q[1]]
            kk = kk_raw[rc, sl] * lax.rsqrt(jnp.maximum(kk_sq[q[1]][rc], 1e-24))
            cum_p = cum[q[0]][:, sl]
            cum_last = cum_p[C - 1:C, :]
            dec_to_end = jnp.exp(cum_last - cum_p)
            inv_p = jnp.exp(-cum_p)
            kka = kk * a[rc, sl]
            rt = r[rc, sl] * jnp.exp(cum_p)
            kt = (k[rc, sl] * inv_p).astype(BF16)
            at = -kk * jnp.exp(cum_p - logdec[rc, sl])
            bt = (kka * inv_p).astype(BF16)
            bk_end[q] = jnp.concatenate([kka * dec_to_end, k[rc, sl] * dec_to_end],
                                        axis=0).astype(BF16)
            dec_chunk[q] = jnp.exp(cum_last)
            vb[q] = v[rc, sl].astype(BF16)
            lhs[q] = jnp.concatenate([at, rt], axis=0).astype(BF16)
            rhs = jnp.concatenate([jnp.where(head0, bt, zero), jnp.where(head0, zero, bt),
                                   jnp.where(head0, kt, zero), jnp.where(head0, zero, kt)], axis=0)
            gram[q] = lax.dot_general(lhs[q], rhs, _NT, preferred_element_type=F32)
        yield

        l_ab = {q: jnp.where(strict, gram[q][:C, :P], 0.0) for q in keys}
        for q in keys:
            l_akrk = jnp.concatenate([jnp.where(strict, gram[q][:C, P:], 0.0),
                                      jnp.where(incl, gram[q][C:, P:], 0.0)], axis=0)
            lv[q] = jnp.dot(l_akrk.astype(BF16), blockdiag(vb[q]), preferred_element_type=F32)
            m_rb[q] = jnp.where(incl, gram[q][C:, :P], 0.0).astype(BF16)
        yield

        t = {q: eye2 + l_ab[q] for q in keys}
        m = {q: _bdot(l_ab[q], blockdiag(l_ab[q].astype(BF16))) for q in keys}
        yield
        n_sq = int(math.log2(C)) - 1
        for step in range(n_sq):
            for q in keys:
                mb = m[q].astype(BF16)
                if step < n_sq - 1:
                    both = jnp.concatenate([blockdiag(t[q].astype(BF16)), blockdiag(mb)], axis=1)
                    prod = jnp.dot(mb, both, preferred_element_type=F32)
                    t[q] = t[q] + prod[:, :P]
                    m[q] = prod[:, P:]
                else:
                    t[q] = t[q] + jnp.dot(mb, blockdiag(t[q].astype(BF16)),
                                          preferred_element_type=F32)
            yield
        for q in keys:
            tb[q] = t[q].astype(BF16)

    state = [state_ref[p] for p in pairs]

    def dependent(group):
        for c in group:
            rc = rows[c]
            ar = [lax.dot_general(lhs[c, p], state[p].astype(BF16), _NT,
                                  preferred_element_type=F32) for p in pairs]
            yield
            ub = [jnp.dot(tb[c, p], blockdiag((ar[p][:C] + lv[c, p][:C]).astype(BF16)),
                          preferred_element_type=F32).astype(BF16) for p in pairs]
            yield
            y = [ar[p][C:] + lv[c, p][C:]
                 + jnp.dot(m_rb[c, p], blockdiag(ub[p]), preferred_element_type=F32) for p in pairs]
            for p in pairs:
                upd = lax.dot_general(jnp.concatenate([ub[p], vb[c, p]], axis=0), bk_end[c, p],
                                      _TN, preferred_element_type=F32)
                state[p] = state[p] * dec_chunk[c, p] + jnp.where(bdmask, upd, 0.0)
            yield
            yc = [y[p] - segsum(y[p]) * (1.0 / hd) for p in pairs]
            var = [segsum(jnp.square(yc[p])) * (1.0 / hd) for p in pairs]
            for p, sl in zip(pairs, sls):
                yn = yc[p] * lax.rsqrt(var[p] + RWKV_LN_EPS) * lng_ref[:, sl] + lnb_ref[:, sl]
                o_ref[0, rc, sl] = ((yn + bonus[p][rc] * v[rc, sl]) * g[rc, sl]).astype(o_ref.dtype)
            yield

    def run(*stages):
        stages = list(stages)
        while stages:
            for stage in tuple(stages):
                if next(stage, stages) is stages:
                    stages.remove(stage)

    groups = [list(chunks[i:i + RWKV_GROUP]) for i in range(0, len(chunks), RWKV_GROUP)]
    run(independent(groups[0]))
    for gi, group in enumerate(groups):
        ahead = [independent(groups[gi + 1])] if gi + 1 < len(groups) else []
        run(dependent(group), *ahead)
    for p in pairs:
        state_ref[p] = state[p]


def _rwkv(z, w0, w_up, a0, a_up, g_up, k_k, k_a, r_k, ln_g, ln_b):
    B, S, _ = z.shape
    W = w0.shape[-1]
    C = RWKV_ROWS
    LO = LORA_W + LORA_A + LORA_G
    P = 2 * RWKV_HEAD_DIM
    lo0 = 3 * W // LO
    zspec = lambda off: pl.BlockSpec((1, C, W), lambda b, n: (b, n, off))
    full = lambda a: pl.BlockSpec(a.shape, lambda b, n: (0,) * a.ndim)
    row = lambda a: a.reshape(1, -1)
    params = [row(w0), w_up.astype(BF16), row(a0), a_up.astype(BF16), g_up.astype(BF16),
              row(k_k), row(k_a), row(r_k), row(ln_g), row(ln_b)]
    return pl.pallas_call(
        _rwkv_staged_kernel,
        grid=(B, S // C),
        in_specs=[zspec(0), zspec(1), zspec(2),
                  pl.BlockSpec((1, C, LO), lambda b, n: (b, n, lo0))] + [full(a) for a in params],
        out_specs=pl.BlockSpec((1, C, W), lambda b, n: (b, n, 0)),
        out_shape=jax.ShapeDtypeStruct((B, S, W), BF16),
        scratch_shapes=[pltpu.VMEM((W // P, P, P), F32)],
        compiler_params=_params(("parallel", "arbitrary")),
        name="rwkv7",
    )(z, z, z, z, *params)


def kernel(x, positions, norm1_g, w_in, ret_gn_g, ret_gn_b, rwkv_mu, rwkv_w0, rwkv_w_up, rwkv_a0, rwkv_a_up, rwkv_g_up, rwkv_k_k, rwkv_k_a, rwkv_r_k, rwkv_ln_g, rwkv_ln_b, w_branch_a, w_branch_b, w_out, norm2_g, mlp_up, mlp_down, final_g):
    B, S, D = x.shape
    M = B * S
    depth = w_in.shape[0]
    ret_w = ret_gn_g.shape[-1]
    rwkv_w = rwkv_w0.shape[-1]
    ret_cols = 4 * ret_w
    rwkv_cols = rwkv_mu.shape[-1]
    cos2, sin2 = _rope_tables(positions, tm=2048)

    in_splits = (0, ret_cols, ret_cols + rwkv_cols, w_in.shape[-1])
    w_in0 = w_in[0]
    w_ret, w_rwkv, w_gate = [w_in0[:, in_splits[i]:in_splits[i + 1]].astype(BF16) for i in range(3)]
    full = lambda t, layer: (t, layer, (0, t.shape[-1]))
    later = (w_branch_a, w_branch_b, w_out, mlp_up, mlp_down)

    xf = x.reshape(M, D)
    for l in range(depth):
        z_qk, xn = _norm_qk_proj(xf, norm1_g[l], w_ret, cos2, sin2, n_cols=2 * ret_w, tm=512)
        z_vg, _ = _matmul(xn, w_ret, name="in_proj_vg", epilogue="silu_upper_half",
                          col0=2 * ret_w, out_dtype=BF16, tm=1024, tn=2 * ret_w)
        z_gate, cast0 = _matmul(xn, w_gate, name="in_proj_gate", epilogue="sigmoid",
                                casts=[full(t, 0) for t in later] if l == 0 else (),
                                out_dtype=BF16, tm=1024, tn=2048)
        if l == 0:
            w_a, w_b, w_o, w_up, w_down = cast0
        z_rwkv = _matmul_shift(xn, w_rwkv, rwkv_mu[l], seq_len=S, name="in_proj_rwkv",
                               tm=1024, tn=rwkv_cols // 2)
        y_ret = _retention(z_qk.reshape(B, S, -1), z_vg.reshape(B, S, -1),
                           ret_gn_g[l], ret_gn_b[l])
        y_rwkv = _rwkv(z_rwkv.reshape(B, S, -1), rwkv_w0[l], rwkv_w_up[l],
                       rwkv_a0[l], rwkv_a_up[l], rwkv_g_up[l], rwkv_k_k[l], rwkv_k_a[l],
                       rwkv_r_k[l], rwkv_ln_g[l], rwkv_ln_b[l])
        xf = _merge_out(y_ret.reshape(M, ret_w), y_rwkv.reshape(M, rwkv_w),
                        w_a, w_b, z_gate, w_o, xf, tm=512)
        if l + 1 < depth:
            cast_next = [(w_in, l + 1, in_splits)] + [full(t, l + 1) for t in later]
            xf, wts = _mlp(xf, norm2_g[l], w_up, w_down, cast_next=cast_next, tm=1024, tf=512)
            w_ret, w_rwkv, w_gate, w_a, w_b, w_o, w_up, w_down = wts
        else:
            xf, _ = _mlp(xf, norm2_g[l], w_up, w_down, final_g, tm=1024, tf=512)
    return xf.reshape(B, S, D)
```

```python
import functools
import math

import jax
import jax.numpy as jnp
from jax import lax
from jax.experimental import pallas as pl
from jax.experimental.pallas import tpu as pltpu

F32 = jnp.float32
BF16 = jnp.bfloat16

RET_HEADS = 8
RET_HEAD_DIM = 128
RET_GN_EPS = 1e-5
ROPE_BASE = 10000.0
RWKV_HEAD_DIM = 64
RWKV_LN_EPS = 64e-5
NORM_EPS = 1e-6
LORA_W = 64
LORA_A = 64
LORA_G = 128

LANES = 128
BF16_SUBLANES = 16
V7X_VMEM_LIMIT_BYTES = 56 * 1024 * 1024

ROPE_ROWS = 2048
QK_PROJ_ROWS = 512
PROJ_ROWS = 1024
PROJ_COLS = 2048
MERGE_ROWS = 512
MLP_ROWS = 1024
MLP_HIDDEN = 512
RET_CHUNK = 128
RET_ROWS = 256
RWKV_CHUNK = 64
RWKV_ROWS = 512
RWKV_GROUP = 2

_NT = (((1,), (1,)), ((), ()))
_TN = (((0,), (0,)), ((), ()))


def _params(semantics):
    return pltpu.CompilerParams(dimension_semantics=semantics,
                                vmem_limit_bytes=V7X_VMEM_LIMIT_BYTES)


def _bdot(a, b):
    return jnp.dot(a.astype(BF16), b.astype(BF16), preferred_element_type=F32)


def _side_cast_specs(casts, n_steps, step_of):
    in_specs, args, out_specs, out_shape, all_splits = [], [], [], [], []
    for w, layer, splits in casts:
        _, R, N = w.shape
        rows = max(BF16_SUBLANES, R // n_steps)
        repeat = rows * n_steps // R
        in_specs.append(pl.BlockSpec(
            (None, rows, N),
            lambda *g, layer=layer, repeat=repeat: (layer, step_of(*g) // repeat, 0)))
        args.append(w)
        for lo, hi in zip(splits[:-1], splits[1:]):
            out_specs.append(pl.BlockSpec(
                (rows, hi - lo), lambda *g, repeat=repeat: (step_of(*g) // repeat, 0)))
            out_shape.append(jax.ShapeDtypeStruct((R, hi - lo), BF16))
        all_splits.append(tuple(splits))
    return in_specs, args, out_specs, out_shape, tuple(all_splits)


def _side_cast(cast_in, cast_out, cast_splits):
    k = 0
    for w_ref, splits in zip(cast_in, cast_splits):
        for lo, hi in zip(splits[:-1], splits[1:]):
            cast_out[k][...] = w_ref[:, lo:hi].astype(BF16)
            k += 1


def _rotary(t, cos2, sin2):
    return t * cos2 + pltpu.roll(t, RET_HEAD_DIM // 2, axis=1) * sin2


def _norm_qk_kernel(x_ref, g_ref, w_ref, cos_ref, sin_ref, o_ref, xn_ref):
    x = x_ref[...]
    ms = jnp.mean(x * x, axis=-1, keepdims=True)
    xn = (x * lax.rsqrt(ms + NORM_EPS) * g_ref[...]).astype(BF16)
    xn_ref[...] = xn
    acc = jnp.dot(xn, w_ref[...], preferred_element_type=F32)
    cos2 = cos_ref[...]
    sin2 = sin_ref[...]
    for h in range(acc.shape[1] // RET_HEAD_DIM):
        sl = slice(h * RET_HEAD_DIM, (h + 1) * RET_HEAD_DIM)
        o_ref[:, sl] = _rotary(acc[:, sl], cos2, sin2).astype(o_ref.dtype)


def _norm_qk_proj(x, g, w, cos2, sin2, *, n_cols, tm):
    M, K = x.shape
    d = RET_HEAD_DIM
    return pl.pallas_call(
        _norm_qk_kernel,
        grid=(M // tm,),
        in_specs=[pl.BlockSpec((tm, K), lambda i: (i, 0)),
                  pl.BlockSpec((1, K), lambda i: (0, 0)),
                  pl.BlockSpec((K, n_cols), lambda i: (0, 0)),
                  pl.BlockSpec((tm, d), lambda i: (i, 0)),
                  pl.BlockSpec((tm, d), lambda i: (i, 0))],
        out_specs=[pl.BlockSpec((tm, n_cols), lambda i: (i, 0)),
                   pl.BlockSpec((tm, K), lambda i: (i, 0))],
        out_shape=[jax.ShapeDtypeStruct((M, n_cols), BF16), jax.ShapeDtypeStruct((M, K), BF16)],
        compiler_params=_params(("parallel",)),
        name="norm_in_proj_qk",
    )(x, g.reshape(1, K), w, cos2, sin2)


def _mm_kernel(*refs, epilogue, cast_splits):
    n_cast_in = len(cast_splits)
    x_ref, w_ref = refs[:2]
    cast_in = refs[2:2 + n_cast_in]
    o_ref = refs[2 + n_cast_in]
    cast_out = refs[3 + n_cast_in:]
    _side_cast(cast_in, cast_out, cast_splits)
    acc = jnp.dot(x_ref[...], w_ref[...], preferred_element_type=F32)
    if epilogue == "sigmoid":
        acc = jax.nn.sigmoid(acc)
        o_ref[...] = acc.astype(o_ref.dtype)
    elif epilogue == "silu_upper_half":
        half = acc.shape[1] // 2
        gate = acc[:, half:]
        o_ref[:, :half] = acc[:, :half].astype(o_ref.dtype)
        o_ref[:, half:] = (gate * jax.nn.sigmoid(gate)).astype(o_ref.dtype)
    else:
        o_ref[...] = acc.astype(o_ref.dtype)


def _matmul(x, w, *, name, epilogue="none", col0=0, n_cols=None, casts=(), out_dtype, tm, tn):
    M, K = x.shape
    n_cols = w.shape[1] - col0 if n_cols is None else n_cols
    c0 = col0 // tn
    nj, ni = n_cols // tn, M // tm
    c_in, c_args, c_out, c_shape, cast_splits = _side_cast_specs(
        casts, ni * nj, lambda j, i: j * ni + i)
    outs = pl.pallas_call(
        functools.partial(_mm_kernel, epilogue=epilogue, cast_splits=cast_splits),
        grid=(nj, ni),
        in_specs=[pl.BlockSpec((tm, K), lambda j, i: (i, 0)),
                  pl.BlockSpec((K, tn), lambda j, i: (0, c0 + j))] + c_in,
        out_specs=[pl.BlockSpec((tm, tn), lambda j, i: (i, j))] + c_out,
        out_shape=[jax.ShapeDtypeStruct((M, n_cols), out_dtype)] + c_shape,
        compiler_params=_params(("parallel", "arbitrary")),
        name=name,
    )(x, w, *c_args)
    return outs[0], outs[1:]


def _mm_shift_kernel(x_ref, w_ref, mu_ref, o_ref, carry_ref, *, tiles_per_seq):
    i = pl.program_id(1)

    @pl.when(i % tiles_per_seq == 0)
    def _():
        carry_ref[...] = jnp.zeros_like(carry_ref)

    z = jnp.dot(x_ref[...], w_ref[...], preferred_element_type=F32)
    n = z.shape[0]
    first = lax.broadcasted_iota(jnp.int32, z.shape, 0) == 0
    z_prev = jnp.where(first, carry_ref[...], pltpu.roll(z, 1, axis=0))
    carry_ref[...] = z[n - 1:n, :]
    o_ref[...] = (z + mu_ref[...] * (z_prev - z)).astype(o_ref.dtype)


def _matmul_shift(x, w, mu, *, seq_len, name, tm, tn):
    M, K = x.shape
    N = w.shape[1]
    assert seq_len % tm == 0
    return pl.pallas_call(
        functools.partial(_mm_shift_kernel, tiles_per_seq=seq_len // tm),
        grid=(N // tn, M // tm),
        in_specs=[pl.BlockSpec((tm, K), lambda j, i: (i, 0)),
                  pl.BlockSpec((K, tn), lambda j, i: (0, j)),
                  pl.BlockSpec((1, tn), lambda j, i: (0, j))],
        out_specs=pl.BlockSpec((tm, tn), lambda j, i: (i, j)),
        out_shape=jax.ShapeDtypeStruct((M, N), F32),
        scratch_shapes=[pltpu.VMEM((1, tn), F32)],
        compiler_params=_params(("arbitrary", "arbitrary")),
        name=name,
    )(x, w, mu.reshape(1, N))


def _mlp_kernel(*refs, final_norm, cast_splits):
    n_cast_in = len(cast_splits)
    n_in = 5 if final_norm else 4
    x_ref, g_ref, wu_ref, wd_ref = refs[:4]
    fg_ref = refs[4] if final_norm else None
    cast_in = refs[n_in:n_in + n_cast_in]
    o_ref = refs[n_in + n_cast_in]
    cast_out = refs[n_in + n_cast_in + 1:-1]
    xn_ref = refs[-1]
    j = pl.program_id(1)

    @pl.when(j == 0)
    def _():
        x = x_ref[...]
        ms = jnp.mean(x * x, axis=-1, keepdims=True)
        xn_ref[...] = (x * lax.rsqrt(ms + NORM_EPS) * g_ref[...]).astype(BF16)
        o_ref[...] = x

    _side_cast(cast_in, cast_out, cast_splits)
    h = jnp.dot(xn_ref[...], wu_ref[...], preferred_element_type=F32)
    h = jnp.square(jnp.maximum(h, 0.0)).astype(BF16)
    o_ref[...] += jnp.dot(h, wd_ref[...], preferred_element_type=F32)

    if final_norm:
        @pl.when(j == pl.num_programs(1) - 1)
        def _():
            y = o_ref[...]
            ms = jnp.mean(y * y, axis=-1, keepdims=True)
            o_ref[...] = y * lax.rsqrt(ms + NORM_EPS) * fg_ref[...]


def _mlp(x, g, w_up, w_down, final_g=None, cast_next=(), *, tm, tf):
    M, D = x.shape
    F = w_up.shape[1]
    final_norm = final_g is not None
    ni, nj = M // tm, F // tf
    vec = pl.BlockSpec((1, D), lambda i, j: (0, 0))
    in_specs = [pl.BlockSpec((tm, D), lambda i, j: (i, 0)), vec,
                pl.BlockSpec((D, tf), lambda i, j: (0, j)),
                pl.BlockSpec((tf, D), lambda i, j: (j, 0))]
    args = [x, g.reshape(1, D), w_up, w_down]
    if final_norm:
        in_specs.append(vec)
        args.append(final_g.reshape(1, D))
    c_in, c_args, c_out, c_shape, cast_splits = _side_cast_specs(
        cast_next, ni * nj, lambda i, j: i * nj + j)
    outs = pl.pallas_call(
        functools.partial(_mlp_kernel, final_norm=final_norm, cast_splits=cast_splits),
        grid=(ni, nj),
        in_specs=in_specs + c_in,
        out_specs=[pl.BlockSpec((tm, D), lambda i, j: (i, 0))] + c_out,
        out_shape=[jax.ShapeDtypeStruct((M, D), F32)] + c_shape,
        scratch_shapes=[pltpu.VMEM((tm, D), BF16)],
        compiler_params=_params(("parallel", "arbitrary")),
        name="mlp_final" if final_norm else "mlp",
    )(*args, *c_args)
    return outs[0], outs[1:]


def _merge_out_kernel(ya_ref, yb_ref, wa_ref, wb_ref, ga_ref, gb_ref, wo_ref, x_ref, o_ref):
    D = o_ref.shape[1]
    n_split = 2
    width = D // n_split
    ya = ya_ref[...]
    yb = yb_ref[...]
    acc = x_ref[...]
    for c in range(n_split):
        cs = slice(c * width, (c + 1) * width)
        a = jnp.dot(ya, wa_ref[:, cs], preferred_element_type=F32)
        b = jnp.dot(yb, wb_ref[:, cs], preferred_element_type=F32)
        m = (ga_ref[:, cs].astype(F32) * a + gb_ref[:, cs].astype(F32) * b).astype(BF16)
        acc = acc + jnp.dot(m, wo_ref[cs, :], preferred_element_type=F32)
    o_ref[...] = acc


def _merge_out(ya, yb, wa, wb, gates, wo, x, *, tm):
    M, K = ya.shape
    D = wo.shape[1]
    resident = lambda a: pl.BlockSpec(a.shape, lambda i: (0, 0), pipeline_mode=pl.Buffered(1))
    return pl.pallas_call(
        _merge_out_kernel,
        grid=(M // tm,),
        in_specs=[
            pl.BlockSpec((tm, K), lambda i: (i, 0)),
            pl.BlockSpec((tm, K), lambda i: (i, 0)),
            resident(wa), resident(wb),
            pl.BlockSpec((tm, D), lambda i: (i, 0)),
            pl.BlockSpec((tm, D), lambda i: (i, 1)),
            resident(wo),
            pl.BlockSpec((tm, D), lambda i: (i, 0)),
        ],
        out_specs=pl.BlockSpec((tm, D), lambda i: (i, 0)),
        out_shape=jax.ShapeDtypeStruct((M, D), F32),
        compiler_params=_params(("parallel",)),
        name="merge_out_proj",
    )(ya, yb, wa, wb, gates, gates, wo, x)


def _rope_kernel(pos_ref, cos_ref, sin_ref):
    half = RET_HEAD_DIM // 2
    pos = pos_ref[...].astype(F32)
    lane = lax.broadcasted_iota(jnp.int32, (1, RET_HEAD_DIM), 1)
    j = jnp.where(lane < half, lane, lane - half).astype(F32)
    inv_freq = jnp.exp(j * (-math.log(ROPE_BASE) / half))
    ang = pos * inv_freq
    cos_ref[...] = jnp.cos(ang)
    sin_ref[...] = jnp.where(lane < half, -1.0, 1.0) * jnp.sin(ang)


def _rope_tables(positions, *, tm):
    M = positions.size
    pos = positions.reshape(M, 1)
    return pl.pallas_call(
        _rope_kernel,
        grid=(M // tm,),
        in_specs=[pl.BlockSpec((tm, 1), lambda i: (i, 0))],
        out_specs=[pl.BlockSpec((tm, RET_HEAD_DIM), lambda i: (i, 0))] * 2,
        out_shape=[jax.ShapeDtypeStruct((M, RET_HEAD_DIM), F32)] * 2,
        compiler_params=_params(("parallel",)),
        name="rope_tables",
    )(pos)


def _retention_kernel(q_ref, k_ref, v_ref, gate_ref, gng_ref, gnb_ref,
                      o_ref, state_ref, inner_ref, qdec_ref, kdec_ref):
    C = RET_CHUNK
    d = RET_HEAD_DIM
    chunks = [slice(c * C, (c + 1) * C) for c in range(q_ref.shape[1] // C)]
    log_gammas = [math.log1p(-(2.0 ** (-5.0 - h))) for h in range(RET_HEADS)]

    @pl.when(pl.program_id(1) == 0)
    def _():
        state_ref[...] = jnp.zeros_like(state_ref)
        row = lax.broadcasted_iota(jnp.int32, (C, C), 0)
        col = lax.broadcasted_iota(jnp.int32, (C, C), 1)
        diff = (row - col).astype(F32)
        rowd = lax.broadcasted_iota(jnp.int32, (C, d), 0).astype(F32)
        scale = d ** -0.5
        for h, lg in enumerate(log_gammas):
            inner_ref[h] = jnp.where(row >= col, jnp.exp(jnp.maximum(diff, 0.0) * lg) * scale, 0.0)
            qdec_ref[h] = jnp.exp((rowd + 1.0) * lg)
            kdec_ref[h] = jnp.exp((C - 1.0 - rowd) * lg) * scale

    heads = range(RET_HEADS)
    sls = [slice(h * d, (h + 1) * d) for h in heads]
    keys = [(rc, h) for rc in chunks for h in heads]
    qb = {(rc, h): q_ref[0, rc, sls[h]] for rc, h in keys}
    kb = {(rc, h): k_ref[0, rc, sls[h]] for rc, h in keys}
    vb = {(rc, h): v_ref[0, rc, sls[h]] for rc, h in keys}
    scores = {q: lax.dot_general(qb[q], kb[q], _NT, preferred_element_type=F32) for q in keys}
    inner = {q: _bdot(scores[q] * inner_ref[q[1]], vb[q]) for q in keys}
    kv = {q: lax.dot_general((kb[q].astype(F32) * kdec_ref[q[1]]).astype(BF16), vb[q], _TN,
                             preferred_element_type=F32) for q in keys}
    state = [state_ref[h] for h in heads]
    cross = {}
    for rc in chunks:
        for h in heads:
            cross[rc, h] = _bdot(qb[rc, h], state[h]) * qdec_ref[h]
            state[h] = math.exp(C * log_gammas[h]) * state[h] + kv[rc, h]
    for h in heads:
        state_ref[h] = state[h]
    for rc, h in keys:
        sl = sls[h]
        y = inner[rc, h] + cross[rc, h]
        mean = jnp.mean(y, axis=-1, keepdims=True)
        yc = y - mean
        var = jnp.mean(yc * yc, axis=-1, keepdims=True)
        yn = yc * lax.rsqrt(var + RET_GN_EPS) * gng_ref[:, sl] + gnb_ref[:, sl]
        o_ref[0, rc, sl] = (gate_ref[0, rc, sl].astype(F32) * yn).astype(o_ref.dtype)


def _retention(qk, vg, gn_g, gn_b):
    B, S, _ = qk.shape
    W = RET_HEADS * RET_HEAD_DIM
    C = RET_CHUNK
    R = RET_ROWS
    zspec = lambda off: pl.BlockSpec((1, R, W), lambda b, n: (b, n, off))
    pspec = pl.BlockSpec((1, W), lambda b, n: (0, 0))
    return pl.pallas_call(
        _retention_kernel,
        grid=(B, S // R),
        in_specs=[zspec(0), zspec(1), zspec(0), zspec(1), pspec, pspec],
        out_specs=pl.BlockSpec((1, R, W), lambda b, n: (b, n, 0)),
        out_shape=jax.ShapeDtypeStruct((B, S, W), BF16),
        scratch_shapes=[pltpu.VMEM((RET_HEADS, RET_HEAD_DIM, RET_HEAD_DIM), F32),
                        pltpu.VMEM((RET_HEADS, C, C), F32),
                        pltpu.VMEM((RET_HEADS, C, RET_HEAD_DIM), F32),
                        pltpu.VMEM((RET_HEADS, C, RET_HEAD_DIM), F32)],
        compiler_params=_params(("parallel", "arbitrary")),
        name="retention",
    )(qk, qk, vg, vg, gn_g.reshape(1, W), gn_b.reshape(1, W))


def _split2(x):
    hi = x.astype(BF16)
    return hi, (x - hi.astype(F32)).astype(BF16)


def _rwkv_staged_kernel(r_ref, k_ref, v_ref, lo_ref,
                        w0_ref, wup_ref, a0_ref, aup_ref, gup_ref, kk_ref, ka_ref, rk_ref,
                        lng_ref, lnb_ref, o_ref, state_ref):
    R = r_ref.shape[1]
    W = r_ref.shape[2]
    C = RWKV_CHUNK
    hd = RWKV_HEAD_DIM
    P = 2 * hd
    assert P == LANES and 2 * C == P and R % C == 0
    chunks = range(R // C)
    rows = [slice(c * C, (c + 1) * C) for c in chunks]
    pairs = range(W // P)
    sls = [slice(p * P, (p + 1) * P) for p in pairs]

    @pl.when(pl.program_id(1) == 0)
    def _():
        state_ref[...] = jnp.zeros_like(state_ref)

    lo = lo_ref[0]
    tw = jnp.tanh(lo[:, 0:LORA_W]).astype(BF16)
    za = lo[:, LORA_W:LORA_W + LORA_A].astype(BF16)
    sg = jax.nn.sigmoid(lo[:, LORA_W + LORA_A:]).astype(BF16)

    r = r_ref[0]
    k = k_ref[0]
    v = v_ref[0]
    logdec = -math.exp(-0.5) * jax.nn.sigmoid(
        w0_ref[...] + jnp.dot(tw, wup_ref[...], preferred_element_type=F32))
    a = jax.nn.sigmoid(a0_ref[...] + jnp.dot(za, aup_ref[...], preferred_element_type=F32))
    g = jnp.dot(sg, gup_ref[...], preferred_element_type=F32)
    kk_raw = k * kk_ref[...]
    k = k * (1.0 + (a - 1.0) * ka_ref[...])

    tri = jnp.where(lax.broadcasted_iota(jnp.int32, (C, C), 0)
                    >= lax.broadcasted_iota(jnp.int32, (C, C), 1), 1.0, 0.0).astype(BF16)
    ld_hi = logdec.astype(BF16)
    ld_mid, ld_lo = _split2(logdec - ld_hi.astype(F32))
    cum = [jnp.dot(tri, ld_hi[rc], preferred_element_type=F32)
           + jnp.dot(tri, ld_mid[rc], preferred_element_type=F32)
           + jnp.dot(tri, ld_lo[rc], preferred_element_type=F32) for rc in rows]

    row = lax.broadcasted_iota(jnp.int32, (C, P), 0)
    lane = lax.broadcasted_iota(jnp.int32, (C, P), 1)
    head0 = lane < hd
    cj = jnp.where(head0, lane, lane - hd)
    strict = row > cj
    incl = row >= cj
    eye2 = jnp.where(row == cj, 1.0, 0.0)
    r2 = lax.broadcasted_iota(jnp.int32, (P, P), 0)
    l2 = lax.broadcasted_iota(jnp.int32, (P, P), 1)
    bdmask = (r2 < hd) == (l2 < hd)
    zero = jnp.zeros((), BF16)

    def blockdiag(x):
        return jnp.where(bdmask, jnp.concatenate([x, x], axis=0), jnp.zeros((), x.dtype))

    def segsum(x):
        h0 = lax.broadcasted_iota(jnp.int32, x.shape, 1) < hd
        s0 = jnp.sum(jnp.where(h0, x, 0.0), axis=-1, keepdims=True)
        s1 = jnp.sum(jnp.where(h0, 0.0, x), axis=-1, keepdims=True)
        return jnp.where(h0, s0, s1)

    kk_sq = [segsum(jnp.square(kk_raw[:, sl])) for sl in sls]
    bonus = [segsum(r[:, sl] * k[:, sl] * rk_ref[:, sl]) for sl in sls]

    lhs, lv, m_rb, tb, bk_end, dec_chunk, vb = ({} for _ in range(7))

    def independent(group):
        keys = [(c, p) for c in group for p in pairs]
        gram = {}
        for q in keys:
            rc, sl = rows[q[0]], sls[q[1]]
            kk = kk_raw[rc, sl] * lax.rsqrt(jnp.maximum(kk_sq[q[1]][rc], 1e-24))
            cum_p = cum[q[0]][:, sl]
            cum_last = cum_p[C - 1:C, :]
            dec_to_end = jnp.exp(cum_last - cum_p)
            inv_p = jnp.exp(-cum_p)
            kka = kk * a[rc, sl]
            rt = r[rc, sl] * jnp.exp(cum_p)
            kt = (k[rc, sl] * inv_p).astype(BF16)
            at = -kk * jnp.exp(cum_p - logdec[rc, sl])
            bt = (kka * inv_p).astype(BF16)
            bk_end[q] = jnp.concatenate([kka * dec_to_end, k[rc, sl] * dec_to_end],
                                        axis=0).astype(BF16)
            dec_chunk[q] = jnp.exp(cum_last)
            vb[q] = v[rc, sl].astype(BF16)
            lhs[q] = jnp.concatenate([at, rt], axis=0).astype(BF16)
            rhs = jnp.concatenate([jnp.where(head0, bt, zero), jnp.where(head0, zero, bt),
                                   jnp.where(head0, kt, zero), jnp.where(head0, zero, kt)], axis=0)
            gram[q] = lax.dot_general(lhs[q], rhs, _NT, preferred_element_type=F32)
        yield

        l_ab = {q: jnp.where(strict, gram[q][:C, :P], 0.0) for q in keys}
        for q in keys:
            l_akrk = jnp.concatenate([jnp.where(strict, gram[q][:C, P:], 0.0),
                                      jnp.where(incl, gram[q][C:, P:], 0.0)], axis=0)
            lv[q] = jnp.dot(l_akrk.astype(BF16), blockdiag(vb[q]), preferred_element_type=F32)
            m_rb[q] = jnp.where(incl, gram[q][C:, :P], 0.0).astype(BF16)
        yield

        t = {q: eye2 + l_ab[q] for q in keys}
        m = {q: _bdot(l_ab[q], blockdiag(l_ab[q].astype(BF16))) for q in keys}
        yield
        n_sq = int(math.log2(C)) - 1
        for step in range(n_sq):
            for q in keys:
                mb = m[q].astype(BF16)
                if step < n_sq - 1:
                    both = jnp.concatenate([blockdiag(t[q].astype(BF16)), blockdiag(mb)], axis=1)
                    prod = jnp.dot(mb, both, preferred_element_type=F32)
                    t[q] = t[q] + prod[:, :P]
                    m[q] = prod[:, P:]
                else:
                    t[q] = t[q] + jnp.dot(mb, blockdiag(t[q].astype(BF16)),
                                          preferred_element_type=F32)
            yield
        for q in keys:
            tb[q] = t[q].astype(BF16)

    state = [state_ref[p] for p in pairs]

    def dependent(group):
        for c in group:
            rc = rows[c]
            ar = [lax.dot_general(lhs[c, p], state[p].astype(BF16), _NT,
                                  preferred_element_type=F32) for p in pairs]
            yield
            ub = [jnp.dot(tb[c, p], blockdiag((ar[p][:C] + lv[c, p][:C]).astype(BF16)),
                          preferred_element_type=F32).astype(BF16) for p in pairs]
            yield
            y = [ar[p][C:] + lv[c, p][C:]
                 + jnp.dot(m_rb[c, p], blockdiag(ub[p]), preferred_element_type=F32) for p in pairs]
            for p in pairs:
                upd = lax.dot_general(jnp.concatenate([ub[p], vb[c, p]], axis=0), bk_end[c, p],
                                      _TN, preferred_element_type=F32)
                state[p] = state[p] * dec_chunk[c, p] + jnp.where(bdmask, upd, 0.0)
            yield
            yc = [y[p] - segsum(y[p]) * (1.0 / hd) for p in pairs]
            var = [segsum(jnp.square(yc[p])) * (1.0 / hd) for p in pairs]
            for p, sl in zip(pairs, sls):
                yn = yc[p] * lax.rsqrt(var[p] + RWKV_LN_EPS) * lng_ref[:, sl] + lnb_ref[:, sl]
                o_ref[0, rc, sl] = ((yn + bonus[p][rc] * v[rc, sl]) * g[rc, sl]).astype(o_ref.dtype)
            yield

    def run(*stages):
        stages = list(stages)
        while stages:
            for stage in tuple(stages):
                if next(stage, stages) is stages:
                    stages.remove(stage)

    groups = [list(chunks[i:i + RWKV_GROUP]) for i in range(0, len(chunks), RWKV_GROUP)]
    run(independent(groups[0]))
    for gi, group in enumerate(groups):
        ahead = [independent(groups[gi + 1])] if gi + 1 < len(groups) else []
        run(dependent(group), *ahead)
    for p in pairs:
        state_ref[p] = state[p]


def _rwkv(z, w0, w_up, a0, a_up, g_up, k_k, k_a, r_k, ln_g, ln_b):
    B, S, _ = z.shape
    W = w0.shape[-1]
    C = RWKV_ROWS
    LO = LORA_W + LORA_A + LORA_G
    P = 2 * RWKV_HEAD_DIM
    lo0 = 3 * W // LO
    zspec = lambda off: pl.BlockSpec((1, C, W), lambda b, n: (b, n, off))
    full = lambda a: pl.BlockSpec(a.shape, lambda b, n: (0,) * a.ndim)
    row = lambda a: a.reshape(1, -1)
    params = [row(w0), w_up.astype(BF16), row(a0), a_up.astype(BF16), g_up.astype(BF16),
              row(k_k), row(k_a), row(r_k), row(ln_g), row(ln_b)]
    return pl.pallas_call(
        _rwkv_staged_kernel,
        grid=(B, S // C),
        in_specs=[zspec(0), zspec(1), zspec(2),
                  pl.BlockSpec((1, C, LO), lambda b, n: (b, n, lo0))] + [full(a) for a in params],
        out_specs=pl.BlockSpec((1, C, W), lambda b, n: (b, n, 0)),
        out_shape=jax.ShapeDtypeStruct((B, S, W), BF16),
        scratch_shapes=[pltpu.VMEM((W // P, P, P), F32)],
        compiler_params=_params(("parallel", "arbitrary")),
        name="rwkv7",
    )(z, z, z, z, *params)


def kernel(x, positions, norm1_g, w_in, ret_gn_g, ret_gn_b, rwkv_mu, rwkv_w0, rwkv_w_up, rwkv_a0, rwkv_a_up, rwkv_g_up, rwkv_k_k, rwkv_k_a, rwkv_r_k, rwkv_ln_g, rwkv_ln_b, w_branch_a, w_branch_b, w_out, norm2_g, mlp_up, mlp_down, final_g):
    B, S, D = x.shape
    M = B * S
    depth = w_in.shape[0]
    ret_w = ret_gn_g.shape[-1]
    rwkv_w = rwkv_w0.shape[-1]
    ret_cols = 4 * ret_w
    rwkv_cols = rwkv_mu.shape[-1]
    assert ret_w == RET_HEADS * RET_HEAD_DIM and rwkv_w % (2 * RWKV_HEAD_DIM) == 0
    assert rwkv_cols == 3 * rwkv_w + LORA_W + LORA_A + LORA_G
    assert S % PROJ_ROWS == 0 and S % RET_ROWS == 0 and S % RWKV_ROWS == 0
    assert M % ROPE_ROWS == 0 and M % MLP_ROWS == 0 and mlp_up.shape[-1] % MLP_HIDDEN == 0
    cos2, sin2 = _rope_tables(positions, tm=ROPE_ROWS)

    in_splits = (0, ret_cols, ret_cols + rwkv_cols, w_in.shape[-1])
    w_in0 = w_in[0]
    w_ret, w_rwkv, w_gate = [w_in0[:, in_splits[i]:in_splits[i + 1]].astype(BF16) for i in range(3)]
    full = lambda t, layer: (t, layer, (0, t.shape[-1]))
    later = (w_branch_a, w_branch_b, w_out, mlp_up, mlp_down)

    xf = x.reshape(M, D)
    for l in range(depth):
        z_qk, xn = _norm_qk_proj(xf, norm1_g[l], w_ret, cos2, sin2, n_cols=2 * ret_w,
                                 tm=QK_PROJ_ROWS)
        z_vg, _ = _matmul(xn, w_ret, name="in_proj_vg", epilogue="silu_upper_half",
                          col0=2 * ret_w, out_dtype=BF16, tm=PROJ_ROWS, tn=2 * ret_w)
        z_gate, cast0 = _matmul(xn, w_gate, name="in_proj_gate", epilogue="sigmoid",
                                casts=[full(t, 0) for t in later] if l == 0 else (),
                                out_dtype=BF16, tm=PROJ_ROWS, tn=PROJ_COLS)
        if l == 0:
            w_a, w_b, w_o, w_up, w_down = cast0
        z_rwkv = _matmul_shift(xn, w_rwkv, rwkv_mu[l], seq_len=S, name="in_proj_rwkv",
                               tm=PROJ_ROWS, tn=rwkv_cols // 2)
        y_ret = _retention(z_qk.reshape(B, S, -1), z_vg.reshape(B, S, -1),
                           ret_gn_g[l], ret_gn_b[l])
        y_rwkv = _rwkv(z_rwkv.reshape(B, S, -1), rwkv_w0[l], rwkv_w_up[l],
                       rwkv_a0[l], rwkv_a_up[l], rwkv_g_up[l], rwkv_k_k[l], rwkv_k_a[l],
                       rwkv_r_k[l], rwkv_ln_g[l], rwkv_ln_b[l])
        xf = _merge_out(y_ret.reshape(M, ret_w), y_rwkv.reshape(M, rwkv_w),
                        w_a, w_b, z_gate, w_o, xf, tm=MERGE_ROWS)
        if l + 1 < depth:
            cast_next = [(w_in, l + 1, in_splits)] + [full(t, l + 1) for t in later]
            xf, wts = _mlp(xf, norm2_g[l], w_up, w_down, cast_next=cast_next,
                           tm=MLP_ROWS, tf=MLP_HIDDEN)
            w_ret, w_rwkv, w_gate, w_a, w_b, w_o, w_up, w_down = wts
        else:
            xf, _ = _mlp(xf, norm2_g[l], w_up, w_down, final_g, tm=MLP_ROWS, tf=MLP_HIDDEN)
    return xf.reshape(B, S, D)
```

```python
import functools
import math

import jax
import jax.numpy as jnp
from jax import lax
from jax.experimental import pallas as pl
from jax.experimental.pallas import tpu as pltpu

F32 = jnp.float32
BF16 = jnp.bfloat16

RET_HEADS = 8
RET_HEAD_DIM = 128
RET_GN_EPS = 1e-5
ROPE_BASE = 10000.0
RWKV_HEAD_DIM = 64
RWKV_LN_EPS = 64e-5
NORM_EPS = 1e-6
LORA_W = 64
LORA_A = 64
LORA_G = 128

LANES = 128
BF16_SUBLANES = 16
V7X_VMEM_LIMIT_BYTES = 56 * 1024 * 1024

ROPE_ROWS = 2048
QK_PROJ_ROWS = 512
PROJ_ROWS = 1024
PROJ_COLS = 2048
MERGE_ROWS = 512
MLP_ROWS = 1024
MLP_HIDDEN = 512
RET_CHUNK = 128
RET_ROWS = 256
RWKV_CHUNK = 64
RWKV_ROWS = 512
RWKV_GROUP = 2

_NT = (((1,), (1,)), ((), ()))
_TN = (((0,), (0,)), ((), ()))


def _params(semantics):
    return pltpu.CompilerParams(dimension_semantics=semantics,
                                vmem_limit_bytes=V7X_VMEM_LIMIT_BYTES)


def _bdot(a, b):
    return jnp.dot(a.astype(BF16), b.astype(BF16), preferred_element_type=F32)


def _side_cast_specs(casts, n_steps, step_of):
    in_specs, args, out_specs, out_shape, all_splits = [], [], [], [], []
    for w, layer, splits in casts:
        _, R, N = w.shape
        rows = max(BF16_SUBLANES, R // n_steps)
        repeat = rows * n_steps // R
        in_specs.append(pl.BlockSpec(
            (None, rows, N),
            lambda *g, layer=layer, repeat=repeat: (layer, step_of(*g) // repeat, 0)))
        args.append(w)
        for lo, hi in zip(splits[:-1], splits[1:]):
            out_specs.append(pl.BlockSpec(
                (rows, hi - lo), lambda *g, repeat=repeat: (step_of(*g) // repeat, 0)))
            out_shape.append(jax.ShapeDtypeStruct((R, hi - lo), BF16))
        all_splits.append(tuple(splits))
    return in_specs, args, out_specs, out_shape, tuple(all_splits)


def _side_cast(cast_in, cast_out, cast_splits):
    k = 0
    for w_ref, splits in zip(cast_in, cast_splits):
        for lo, hi in zip(splits[:-1], splits[1:]):
            cast_out[k][...] = w_ref[:, lo:hi].astype(BF16)
            k += 1


def _rotary(t, cos2, sin2):
    return t * cos2 + pltpu.roll(t, RET_HEAD_DIM // 2, axis=1) * sin2


def _norm_qk_kernel(x_ref, g_ref, w_ref, cos_ref, sin_ref, o_ref, xn_ref):
    x = x_ref[...]
    ms = jnp.mean(x * x, axis=-1, keepdims=True)
    xn = (x * lax.rsqrt(ms + NORM_EPS) * g_ref[...]).astype(BF16)
    xn_ref[...] = xn
    acc = jnp.dot(xn, w_ref[...], preferred_element_type=F32)
    cos2 = cos_ref[...]
    sin2 = sin_ref[...]
    for h in range(acc.shape[1] // RET_HEAD_DIM):
        sl = slice(h * RET_HEAD_DIM, (h + 1) * RET_HEAD_DIM)
        o_ref[:, sl] = _rotary(acc[:, sl], cos2, sin2).astype(o_ref.dtype)


def _norm_qk_proj(x, g, w, cos2, sin2, *, n_cols, tm):
    M, K = x.shape
    d = RET_HEAD_DIM
    return pl.pallas_call(
        _norm_qk_kernel,
        grid=(M // tm,),
        in_specs=[pl.BlockSpec((tm, K), lambda i: (i, 0)),
                  pl.BlockSpec((1, K), lambda i: (0, 0)),
                  pl.BlockSpec((K, n_cols), lambda i: (0, 0)),
                  pl.BlockSpec((tm, d), lambda i: (i, 0)),
                  pl.BlockSpec((tm, d), lambda i: (i, 0))],
        out_specs=[pl.BlockSpec((tm, n_cols), lambda i: (i, 0)),
                   pl.BlockSpec((tm, K), lambda i: (i, 0))],
        out_shape=[jax.ShapeDtypeStruct((M, n_cols), BF16), jax.ShapeDtypeStruct((M, K), BF16)],
        compiler_params=_params(("parallel",)),
        name="norm_in_proj_qk",
    )(x, g.reshape(1, K), w, cos2, sin2)


def _mm_kernel(*refs, epilogue, cast_splits):
    n_cast_in = len(cast_splits)
    x_ref, w_ref = refs[:2]
    cast_in = refs[2:2 + n_cast_in]
    o_ref = refs[2 + n_cast_in]
    cast_out = refs[3 + n_cast_in:]
    _side_cast(cast_in, cast_out, cast_splits)
    acc = jnp.dot(x_ref[...], w_ref[...], preferred_element_type=F32)
    if epilogue == "sigmoid":
        acc = jax.nn.sigmoid(acc)
        o_ref[...] = acc.astype(o_ref.dtype)
    elif epilogue == "silu_upper_half":
        half = acc.shape[1] // 2
        gate = acc[:, half:]
        o_ref[:, :half] = acc[:, :half].astype(o_ref.dtype)
        o_ref[:, half:] = (gate * jax.nn.sigmoid(gate)).astype(o_ref.dtype)
    else:
        o_ref[...] = acc.astype(o_ref.dtype)


def _matmul(x, w, *, name, epilogue="none", col0=0, n_cols=None, casts=(), out_dtype, tm, tn):
    M, K = x.shape
    n_cols = w.shape[1] - col0 if n_cols is None else n_cols
    c0 = col0 // tn
    nj, ni = n_cols // tn, M // tm
    c_in, c_args, c_out, c_shape, cast_splits = _side_cast_specs(
        casts, ni * nj, lambda j, i: j * ni + i)
    outs = pl.pallas_call(
        functools.partial(_mm_kernel, epilogue=epilogue, cast_splits=cast_splits),
        grid=(nj, ni),
        in_specs=[pl.BlockSpec((tm, K), lambda j, i: (i, 0)),
                  pl.BlockSpec((K, tn), lambda j, i: (0, c0 + j))] + c_in,
        out_specs=[pl.BlockSpec((tm, tn), lambda j, i: (i, j))] + c_out,
        out_shape=[jax.ShapeDtypeStruct((M, n_cols), out_dtype)] + c_shape,
        compiler_params=_params(("parallel", "arbitrary")),
        name=name,
    )(x, w, *c_args)
    return outs[0], outs[1:]


def _mm_shift_kernel(x_ref, w_ref, mu_ref, o_ref, carry_ref, *, tiles_per_seq):
    i = pl.program_id(1)

    @pl.when(i % tiles_per_seq == 0)
    def _():
        carry_ref[...] = jnp.zeros_like(carry_ref)

    z = jnp.dot(x_ref[...], w_ref[...], preferred_element_type=F32)
    n = z.shape[0]
    first = lax.broadcasted_iota(jnp.int32, z.shape, 0) == 0
    z_prev = jnp.where(first, carry_ref[...], pltpu.roll(z, 1, axis=0))
    carry_ref[...] = z[n - 1:n, :]
    o_ref[...] = (z + mu_ref[...] * (z_prev - z)).astype(o_ref.dtype)


def _matmul_shift(x, w, mu, *, seq_len, name, tm, tn):
    M, K = x.shape
    N = w.shape[1]
    assert seq_len % tm == 0
    return pl.pallas_call(
        functools.partial(_mm_shift_kernel, tiles_per_seq=seq_len // tm),
        grid=(N // tn, M // tm),
        in_specs=[pl.BlockSpec((tm, K), lambda j, i: (i, 0)),
                  pl.BlockSpec((K, tn), lambda j, i: (0, j)),
                  pl.BlockSpec((1, tn), lambda j, i: (0, j))],
        out_specs=pl.BlockSpec((tm, tn), lambda j, i: (i, j)),
        out_shape=jax.ShapeDtypeStruct((M, N), F32),
        scratch_shapes=[pltpu.VMEM((1, tn), F32)],
        compiler_params=_params(("arbitrary", "arbitrary")),
        name=name,
    )(x, w, mu.reshape(1, N))


def _mlp_kernel(*refs, final_norm, cast_splits):
    n_cast_in = len(cast_splits)
    n_in = 5 if final_norm else 4
    x_ref, g_ref, wu_ref, wd_ref = refs[:4]
    fg_ref = refs[4] if final_norm else None
    cast_in = refs[n_in:n_in + n_cast_in]
    o_ref = refs[n_in + n_cast_in]
    cast_out = refs[n_in + n_cast_in + 1:-1]
    xn_ref = refs[-1]
    j = pl.program_id(1)

    @pl.when(j == 0)
    def _():
        x = x_ref[...]
        ms = jnp.mean(x * x, axis=-1, keepdims=True)
        xn_ref[...] = (x * lax.rsqrt(ms + NORM_EPS) * g_ref[...]).astype(BF16)
        o_ref[...] = x

    _side_cast(cast_in, cast_out, cast_splits)
    h = jnp.dot(xn_ref[...], wu_ref[...], preferred_element_type=F32)
    h = jnp.square(jnp.maximum(h, 0.0)).astype(BF16)
    o_ref[...] += jnp.dot(h, wd_ref[...], preferred_element_type=F32)

    if final_norm:
        @pl.when(j == pl.num_programs(1) - 1)
        def _():
            y = o_ref[...]
            ms = jnp.mean(y * y, axis=-1, keepdims=True)
            o_ref[...] = y * lax.rsqrt(ms + NORM_EPS) * fg_ref[...]


def _mlp(x, g, w_up, w_down, final_g=None, cast_next=(), *, tm, tf):
    M, D = x.shape
    F = w_up.shape[1]
    final_norm = final_g is not None
    ni, nj = M // tm, F // tf
    vec = pl.BlockSpec((1, D), lambda i, j: (0, 0))
    in_specs = [pl.BlockSpec((tm, D), lambda i, j: (i, 0)), vec,
                pl.BlockSpec((D, tf), lambda i, j: (0, j)),
                pl.BlockSpec((tf, D), lambda i, j: (j, 0))]
    args = [x, g.reshape(1, D), w_up, w_down]
    if final_norm:
        in_specs.append(vec)
        args.append(final_g.reshape(1, D))
    c_in, c_args, c_out, c_shape, cast_splits = _side_cast_specs(
        cast_next, ni * nj, lambda i, j: i * nj + j)
    outs = pl.pallas_call(
        functools.partial(_mlp_kernel, final_norm=final_norm, cast_splits=cast_splits),
        grid=(ni, nj),
        in_specs=in_specs + c_in,
        out_specs=[pl.BlockSpec((tm, D), lambda i, j: (i, 0))] + c_out,
        out_shape=[jax.ShapeDtypeStruct((M, D), F32)] + c_shape,
        scratch_shapes=[pltpu.VMEM((tm, D), BF16)],
        compiler_params=_params(("parallel", "arbitrary")),
        name="mlp_final" if final_norm else "mlp",
    )(*args, *c_args)
    return outs[0], outs[1:]


def _merge_out_kernel(ya_ref, yb_ref, wa_ref, wb_ref, ga_ref, gb_ref, wo_ref, x_ref, o_ref):
    D = o_ref.shape[1]
    n_split = 2
    width = D // n_split
    ya = ya_ref[...]
    yb = yb_ref[...]
    acc = x_ref[...]
    for c in range(n_split):
        cs = slice(c * width, (c + 1) * width)
        a = jnp.dot(ya, wa_ref[:, cs], preferred_element_type=F32)
        b = jnp.dot(yb, wb_ref[:, cs], preferred_element_type=F32)
        m = (ga_ref[:, cs].astype(F32) * a + gb_ref[:, cs].astype(F32) * b).astype(BF16)
        acc = acc + jnp.dot(m, wo_ref[cs, :], preferred_element_type=F32)
    o_ref[...] = acc


def _merge_out(ya, yb, wa, wb, gates, wo, x, *, tm):
    M, K = ya.shape
    D = wo.shape[1]
    resident = lambda a: pl.BlockSpec(a.shape, lambda i: (0, 0), pipeline_mode=pl.Buffered(1))
    return pl.pallas_call(
        _merge_out_kernel,
        grid=(M // tm,),
        in_specs=[
            pl.BlockSpec((tm, K), lambda i: (i, 0)),
            pl.BlockSpec((tm, K), lambda i: (i, 0)),
            resident(wa), resident(wb),
            pl.BlockSpec((tm, D), lambda i: (i, 0)),
            pl.BlockSpec((tm, D), lambda i: (i, 1)),
            resident(wo),
            pl.BlockSpec((tm, D), lambda i: (i, 0)),
        ],
        out_specs=pl.BlockSpec((tm, D), lambda i: (i, 0)),
        out_shape=jax.ShapeDtypeStruct((M, D), F32),
        compiler_params=_params(("parallel",)),
        name="merge_out_proj",
    )(ya, yb, wa, wb, gates, gates, wo, x)


def _rope_kernel(pos_ref, cos_ref, sin_ref):
    half = RET_HEAD_DIM // 2
    pos = pos_ref[...].astype(F32)
    lane = lax.broadcasted_iota(jnp.int32, (1, RET_HEAD_DIM), 1)
    j = jnp.where(lane < half, lane, lane - half).astype(F32)
    inv_freq = jnp.exp(j * (-math.log(ROPE_BASE) / half))
    ang = pos * inv_freq
    cos_ref[...] = jnp.cos(ang)
    sin_ref[...] = jnp.where(lane < half, -1.0, 1.0) * jnp.sin(ang)


def _rope_tables(positions, *, tm):
    M = positions.size
    pos = positions.reshape(M, 1)
    return pl.pallas_call(
        _rope_kernel,
        grid=(M // tm,),
        in_specs=[pl.BlockSpec((tm, 1), lambda i: (i, 0))],
        out_specs=[pl.BlockSpec((tm, RET_HEAD_DIM), lambda i: (i, 0))] * 2,
        out_shape=[jax.ShapeDtypeStruct((M, RET_HEAD_DIM), F32)] * 2,
        compiler_params=_params(("parallel",)),
        name="rope_tables",
    )(pos)


def _retention_kernel(q_ref, k_ref, v_ref, gate_ref, gng_ref, gnb_ref,
                      o_ref, state_ref, inner_ref, qdec_ref, kdec_ref):
    C = RET_CHUNK
    d = RET_HEAD_DIM
    chunks = [slice(c * C, (c + 1) * C) for c in range(q_ref.shape[1] // C)]
    log_gammas = [math.log1p(-(2.0 ** (-5.0 - h))) for h in range(RET_HEADS)]

    @pl.when(pl.program_id(1) == 0)
    def _():
        state_ref[...] = jnp.zeros_like(state_ref)
        row = lax.broadcasted_iota(jnp.int32, (C, C), 0)
        col = lax.broadcasted_iota(jnp.int32, (C, C), 1)
        diff = (row - col).astype(F32)
        rowd = lax.broadcasted_iota(jnp.int32, (C, d), 0).astype(F32)
        scale = d ** -0.5
        for h, lg in enumerate(log_gammas):
            inner_ref[h] = jnp.where(row >= col, jnp.exp(jnp.maximum(diff, 0.0) * lg) * scale, 0.0)
            qdec_ref[h] = jnp.exp((rowd + 1.0) * lg)
            kdec_ref[h] = jnp.exp((C - 1.0 - rowd) * lg) * scale

    heads = range(RET_HEADS)
    sls = [slice(h * d, (h + 1) * d) for h in heads]
    keys = [(c, h) for c in range(len(chunks)) for h in heads]
    qb = {(c, h): q_ref[0, chunks[c], sls[h]] for c, h in keys}
    kb = {(c, h): k_ref[0, chunks[c], sls[h]] for c, h in keys}
    vb = {(c, h): v_ref[0, chunks[c], sls[h]] for c, h in keys}
    scores = {q: lax.dot_general(qb[q], kb[q], _NT, preferred_element_type=F32) for q in keys}
    inner = {q: _bdot(scores[q] * inner_ref[q[1]], vb[q]) for q in keys}
    kv = {q: lax.dot_general((kb[q].astype(F32) * kdec_ref[q[1]]).astype(BF16), vb[q], _TN,
                             preferred_element_type=F32) for q in keys}
    state = [state_ref[h] for h in heads]
    cross = {}
    for c, h in keys:
        cross[c, h] = _bdot(qb[c, h], state[h]) * qdec_ref[h]
        state[h] = math.exp(C * log_gammas[h]) * state[h] + kv[c, h]
    for h in heads:
        state_ref[h] = state[h]
    for c, h in keys:
        rc, sl = chunks[c], sls[h]
        y = inner[c, h] + cross[c, h]
        mean = jnp.mean(y, axis=-1, keepdims=True)
        yc = y - mean
        var = jnp.mean(yc * yc, axis=-1, keepdims=True)
        yn = yc * lax.rsqrt(var + RET_GN_EPS) * gng_ref[:, sl] + gnb_ref[:, sl]
        o_ref[0, rc, sl] = (gate_ref[0, rc, sl].astype(F32) * yn).astype(o_ref.dtype)


def _retention(qk, vg, gn_g, gn_b):
    B, S, _ = qk.shape
    W = RET_HEADS * RET_HEAD_DIM
    C = RET_CHUNK
    R = RET_ROWS
    zspec = lambda off: pl.BlockSpec((1, R, W), lambda b, n: (b, n, off))
    pspec = pl.BlockSpec((1, W), lambda b, n: (0, 0))
    return pl.pallas_call(
        _retention_kernel,
        grid=(B, S // R),
        in_specs=[zspec(0), zspec(1), zspec(0), zspec(1), pspec, pspec],
        out_specs=pl.BlockSpec((1, R, W), lambda b, n: (b, n, 0)),
        out_shape=jax.ShapeDtypeStruct((B, S, W), BF16),
        scratch_shapes=[pltpu.VMEM((RET_HEADS, RET_HEAD_DIM, RET_HEAD_DIM), F32),
                        pltpu.VMEM((RET_HEADS, C, C), F32),
                        pltpu.VMEM((RET_HEADS, C, RET_HEAD_DIM), F32),
                        pltpu.VMEM((RET_HEADS, C, RET_HEAD_DIM), F32)],
        compiler_params=_params(("parallel", "arbitrary")),
        name="retention",
    )(qk, qk, vg, vg, gn_g.reshape(1, W), gn_b.reshape(1, W))


def _split2(x):
    hi = x.astype(BF16)
    return hi, (x - hi.astype(F32)).astype(BF16)


def _rwkv_staged_kernel(r_ref, k_ref, v_ref, lo_ref,
                        w0_ref, wup_ref, a0_ref, aup_ref, gup_ref, kk_ref, ka_ref, rk_ref,
                        lng_ref, lnb_ref, o_ref, state_ref):
    R = r_ref.shape[1]
    W = r_ref.shape[2]
    C = RWKV_CHUNK
    hd = RWKV_HEAD_DIM
    P = 2 * hd
    assert P == LANES and 2 * C == P and R % C == 0
    chunks = range(R // C)
    rows = [slice(c * C, (c + 1) * C) for c in chunks]
    pairs = range(W // P)
    sls = [slice(p * P, (p + 1) * P) for p in pairs]

    @pl.when(pl.program_id(1) == 0)
    def _():
        state_ref[...] = jnp.zeros_like(state_ref)

    lo = lo_ref[0]
    tw = jnp.tanh(lo[:, 0:LORA_W]).astype(BF16)
    za = lo[:, LORA_W:LORA_W + LORA_A].astype(BF16)
    sg = jax.nn.sigmoid(lo[:, LORA_W + LORA_A:]).astype(BF16)

    r = r_ref[0]
    k = k_ref[0]
    v = v_ref[0]
    logdec = -math.exp(-0.5) * jax.nn.sigmoid(
        w0_ref[...] + jnp.dot(tw, wup_ref[...], preferred_element_type=F32))
    a = jax.nn.sigmoid(a0_ref[...] + jnp.dot(za, aup_ref[...], preferred_element_type=F32))
    g = jnp.dot(sg, gup_ref[...], preferred_element_type=F32)
    kk_raw = k * kk_ref[...]
    k = k * (1.0 + (a - 1.0) * ka_ref[...])

    tri = jnp.where(lax.broadcasted_iota(jnp.int32, (C, C), 0)
                    >= lax.broadcasted_iota(jnp.int32, (C, C), 1), 1.0, 0.0).astype(BF16)
    ld_hi = logdec.astype(BF16)
    ld_mid, ld_lo = _split2(logdec - ld_hi.astype(F32))
    cum = [jnp.dot(tri, ld_hi[rc], preferred_element_type=F32)
           + jnp.dot(tri, ld_mid[rc], preferred_element_type=F32)
           + jnp.dot(tri, ld_lo[rc], preferred_element_type=F32) for rc in rows]

    row = lax.broadcasted_iota(jnp.int32, (C, P), 0)
    lane = lax.broadcasted_iota(jnp.int32, (C, P), 1)
    head0 = lane < hd
    cj = jnp.where(head0, lane, lane - hd)
    strict = row > cj
    incl = row >= cj
    eye2 = jnp.where(row == cj, 1.0, 0.0)
    r2 = lax.broadcasted_iota(jnp.int32, (P, P), 0)
    l2 = lax.broadcasted_iota(jnp.int32, (P, P), 1)
    bdmask = (r2 < hd) == (l2 < hd)
    zero = jnp.zeros((), BF16)

    def blockdiag(x):
        return jnp.where(bdmask, jnp.concatenate([x, x], axis=0), jnp.zeros((), x.dtype))

    def segsum(x):
        h0 = lax.broadcasted_iota(jnp.int32, x.shape, 1) < hd
        s0 = jnp.sum(jnp.where(h0, x, 0.0), axis=-1, keepdims=True)
        s1 = jnp.sum(jnp.where(h0, 0.0, x), axis=-1, keepdims=True)
        return jnp.where(h0, s0, s1)

    kk_sq = [segsum(jnp.square(kk_raw[:, sl])) for sl in sls]
    bonus = [segsum(r[:, sl] * k[:, sl] * rk_ref[:, sl]) for sl in sls]

    lhs, lv, m_rb, tb, bk_end, dec_chunk, vb = ({} for _ in range(7))

    def independent(group):
        keys = [(c, p) for c in group for p in pairs]
        gram = {}
        for q in keys:
            rc, sl = rows[q[0]], sls[q[1]]
            kk = kk_raw[rc, sl] * lax.rsqrt(jnp.maximum(kk_sq[q[1]][rc], 1e-24))
            cum_p = cum[q[0]][:, sl]
            cum_last = cum_p[C - 1:C, :]
            dec_to_end = jnp.exp(cum_last - cum_p)
            inv_p = jnp.exp(-cum_p)
            kka = kk * a[rc, sl]
            rt = r[rc, sl] * jnp.exp(cum_p)
            kt = (k[rc, sl] * inv_p).astype(BF16)
            at = -kk * jnp.exp(cum_p - logdec[rc, sl])
            bt = (kka * inv_p).astype(BF16)
            bk_end[q] = jnp.concatenate([kka * dec_to_end, k[rc, sl] * dec_to_end],
                                        axis=0).astype(BF16)
            dec_chunk[q] = jnp.exp(cum_last)
            vb[q] = v[rc, sl].astype(BF16)
            lhs[q] = jnp.concatenate([at, rt], axis=0).astype(BF16)
            rhs = jnp.concatenate([jnp.where(head0, bt, zero), jnp.where(head0, zero, bt),
                                   jnp.where(head0, kt, zero), jnp.where(head0, zero, kt)], axis=0)
            gram[q] = lax.dot_general(lhs[q], rhs, _NT, preferred_element_type=F32)
        yield

        l_ab = {q: jnp.where(strict, gram[q][:C, :P], 0.0) for q in keys}
        for q in keys:
            l_akrk = jnp.concatenate([jnp.where(strict, gram[q][:C, P:], 0.0),
                                      jnp.where(incl, gram[q][C:, P:], 0.0)], axis=0)
            lv[q] = jnp.dot(l_akrk.astype(BF16), blockdiag(vb[q]), preferred_element_type=F32)
            m_rb[q] = jnp.where(incl, gram[q][C:, :P], 0.0).astype(BF16)
        yield

        t = {q: eye2 + l_ab[q] for q in keys}
        m = {q: _bdot(l_ab[q], blockdiag(l_ab[q].astype(BF16))) for q in keys}
        yield
        n_sq = int(math.log2(C)) - 1
        for step in range(n_sq):
            for q in keys:
                mb = m[q].astype(BF16)
                if step < n_sq - 1:
                    both = jnp.concatenate([blockdiag(t[q].astype(BF16)), blockdiag(mb)], axis=1)
                    prod = jnp.dot(mb, both, preferred_element_type=F32)
                    t[q] = t[q] + prod[:, :P]
                    m[q] = prod[:, P:]
                else:
                    t[q] = t[q] + jnp.dot(mb, blockdiag(t[q].astype(BF16)),
                                          preferred_element_type=F32)
            yield
        for q in keys:
            tb[q] = t[q].astype(BF16)

    state = [state_ref[p] for p in pairs]

    def dependent(group):
        for c in group:
            rc = rows[c]
            ar = [lax.dot_general(lhs[c, p], state[p].astype(BF16), _NT,
                                  preferred_element_type=F32) for p in pairs]
            yield
            ub = [jnp.dot(tb[c, p], blockdiag((ar[p][:C] + lv[c, p][:C]).astype(BF16)),
                          preferred_element_type=F32).astype(BF16) for p in pairs]
            yield
            y = [ar[p][C:] + lv[c, p][C:]
                 + jnp.dot(m_rb[c, p], blockdiag(ub[p]), preferred_element_type=F32) for p in pairs]
            for p in pairs:
                upd = lax.dot_general(jnp.concatenate([ub[p], vb[c, p]], axis=0), bk_end[c, p],
                                      _TN, preferred_element_type=F32)
                state[p] = state[p] * dec_chunk[c, p] + jnp.where(bdmask, upd, 0.0)
            yield
            yc = [y[p] - segsum(y[p]) * (1.0 / hd) for p in pairs]
            var = [segsum(jnp.square(yc[p])) * (1.0 / hd) for p in pairs]
            for p, sl in zip(pairs, sls):
                yn = yc[p] * lax.rsqrt(var[p] + RWKV_LN_EPS) * lng_ref[:, sl] + lnb_ref[:, sl]
                o_ref[0, rc, sl] = ((yn + bonus[p][rc] * v[rc, sl]) * g[rc, sl]).astype(o_ref.dtype)
            yield

    def run(*stages):
        stages = list(stages)
        while stages:
            for stage in tuple(stages):
                if next(stage, stages) is stages:
                    stages.remove(stage)

    groups = [list(chunks[i:i + RWKV_GROUP]) for i in range(0, len(chunks), RWKV_GROUP)]
    run(independent(groups[0]))
    for gi, group in enumerate(groups):
        ahead = [independent(groups[gi + 1])] if gi + 1 < len(groups) else []
        run(dependent(group), *ahead)
    for p in pairs:
        state_ref[p] = state[p]


def _rwkv(z, w0, w_up, a0, a_up, g_up, k_k, k_a, r_k, ln_g, ln_b):
    B, S, _ = z.shape
    W = w0.shape[-1]
    C = RWKV_ROWS
    LO = LORA_W + LORA_A + LORA_G
    P = 2 * RWKV_HEAD_DIM
    lo0 = 3 * W // LO
    zspec = lambda off: pl.BlockSpec((1, C, W), lambda b, n: (b, n, off))
    full = lambda a: pl.BlockSpec(a.shape, lambda b, n: (0,) * a.ndim)
    row = lambda a: a.reshape(1, -1)
    params = [row(w0), w_up.astype(BF16), row(a0), a_up.astype(BF16), g_up.astype(BF16),
              row(k_k), row(k_a), row(r_k), row(ln_g), row(ln_b)]
    return pl.pallas_call(
        _rwkv_staged_kernel,
        grid=(B, S // C),
        in_specs=[zspec(0), zspec(1), zspec(2),
                  pl.BlockSpec((1, C, LO), lambda b, n: (b, n, lo0))] + [full(a) for a in params],
        out_specs=pl.BlockSpec((1, C, W), lambda b, n: (b, n, 0)),
        out_shape=jax.ShapeDtypeStruct((B, S, W), BF16),
        scratch_shapes=[pltpu.VMEM((W // P, P, P), F32)],
        compiler_params=_params(("parallel", "arbitrary")),
        name="rwkv7",
    )(z, z, z, z, *params)


def kernel(x, positions, norm1_g, w_in, ret_gn_g, ret_gn_b, rwkv_mu, rwkv_w0, rwkv_w_up, rwkv_a0, rwkv_a_up, rwkv_g_up, rwkv_k_k, rwkv_k_a, rwkv_r_k, rwkv_ln_g, rwkv_ln_b, w_branch_a, w_branch_b, w_out, norm2_g, mlp_up, mlp_down, final_g):
    B, S, D = x.shape
    M = B * S
    depth = w_in.shape[0]
    ret_w = ret_gn_g.shape[-1]
    rwkv_w = rwkv_w0.shape[-1]
    ret_cols = 4 * ret_w
    rwkv_cols = rwkv_mu.shape[-1]
    assert ret_w == RET_HEADS * RET_HEAD_DIM and rwkv_w % (2 * RWKV_HEAD_DIM) == 0
    assert rwkv_cols == 3 * rwkv_w + LORA_W + LORA_A + LORA_G
    assert S % PROJ_ROWS == 0 and S % RET_ROWS == 0 and S % RWKV_ROWS == 0
    assert M % ROPE_ROWS == 0 and M % MLP_ROWS == 0 and mlp_up.shape[-1] % MLP_HIDDEN == 0
    cos2, sin2 = _rope_tables(positions, tm=ROPE_ROWS)

    in_splits = (0, ret_cols, ret_cols + rwkv_cols, w_in.shape[-1])
    w_in0 = w_in[0]
    w_ret, w_rwkv, w_gate = [w_in0[:, in_splits[i]:in_splits[i + 1]].astype(BF16) for i in range(3)]
    full = lambda t, layer: (t, layer, (0, t.shape[-1]))
    later = (w_branch_a, w_branch_b, w_out, mlp_up, mlp_down)

    xf = x.reshape(M, D)
    for l in range(depth):
        z_qk, xn = _norm_qk_proj(xf, norm1_g[l], w_ret, cos2, sin2, n_cols=2 * ret_w,
                                 tm=QK_PROJ_ROWS)
        z_vg, _ = _matmul(xn, w_ret, name="in_proj_vg", epilogue="silu_upper_half",
                          col0=2 * ret_w, out_dtype=BF16, tm=PROJ_ROWS, tn=2 * ret_w)
        z_gate, cast0 = _matmul(xn, w_gate, name="in_proj_gate", epilogue="sigmoid",
                                casts=[full(t, 0) for t in later] if l == 0 else (),
                                out_dtype=BF16, tm=PROJ_ROWS, tn=PROJ_COLS)
        if l == 0:
            w_a, w_b, w_o, w_up, w_down = cast0
        z_rwkv = _matmul_shift(xn, w_rwkv, rwkv_mu[l], seq_len=S, name="in_proj_rwkv",
                               tm=PROJ_ROWS, tn=rwkv_cols // 2)
        y_ret = _retention(z_qk.reshape(B, S, -1), z_vg.reshape(B, S, -1),
                           ret_gn_g[l], ret_gn_b[l])
        y_rwkv = _rwkv(z_rwkv.reshape(B, S, -1), rwkv_w0[l], rwkv_w_up[l],
                       rwkv_a0[l], rwkv_a_up[l], rwkv_g_up[l], rwkv_k_k[l], rwkv_k_a[l],
                       rwkv_r_k[l], rwkv_ln_g[l], rwkv_ln_b[l])
        xf = _merge_out(y_ret.reshape(M, ret_w), y_rwkv.reshape(M, rwkv_w),
                        w_a, w_b, z_gate, w_o, xf, tm=MERGE_ROWS)
        if l + 1 < depth:
            cast_next = [(w_in, l + 1, in_splits)] + [full(t, l + 1) for t in later]
            xf, wts = _mlp(xf, norm2_g[l], w_up, w_down, cast_next=cast_next,
                           tm=MLP_ROWS, tf=MLP_HIDDEN)
            w_ret, w_rwkv, w_gate, w_a, w_b, w_o, w_up, w_down = wts
        else:
            xf, _ = _mlp(xf, norm2_g[l], w_up, w_down, final_g, tm=MLP_ROWS, tf=MLP_HIDDEN)
    return xf.reshape(B, S, D)
```

```python
import functools
import math

import jax
import jax.numpy as jnp
from jax import lax
from jax.experimental import pallas as pl
from jax.experimental.pallas import tpu as pltpu

F32 = jnp.float32
BF16 = jnp.bfloat16

RET_HEADS = 8
RET_HEAD_DIM = 128
RET_GN_EPS = 1e-5
ROPE_BASE = 10000.0
RWKV_HEAD_DIM = 64
RWKV_LN_EPS = 64e-5
NORM_EPS = 1e-6
LORA_W = 64
LORA_A = 64
LORA_G = 128

LANES = 128
BF16_SUBLANES = 16
V7X_VMEM_LIMIT_BYTES = 56 * 1024 * 1024

ROPE_ROWS = 2048
QK_PROJ_ROWS = 512
PROJ_ROWS = 1024
PROJ_COLS = 2048
MERGE_ROWS = 512
MLP_ROWS = 1024
MLP_HIDDEN = 512
RET_CHUNK = 128
RET_ROWS = 256
RWKV_CHUNK = 64
RWKV_ROWS = 512
RWKV_GROUP = 2

_NT = (((1,), (1,)), ((), ()))
_TN = (((0,), (0,)), ((), ()))


def _params(semantics):
    return pltpu.CompilerParams(dimension_semantics=semantics,
                                vmem_limit_bytes=V7X_VMEM_LIMIT_BYTES)


def _bdot(a, b):
    return jnp.dot(a.astype(BF16), b.astype(BF16), preferred_element_type=F32)


def _side_cast_specs(casts, n_steps, step_of):
    in_specs, args, out_specs, out_shape, all_splits = [], [], [], [], []
    for w, layer, splits in casts:
        _, R, N = w.shape
        rows = max(BF16_SUBLANES, R // n_steps)
        repeat = rows * n_steps // R
        in_specs.append(pl.BlockSpec(
            (None, rows, N),
            lambda *g, layer=layer, repeat=repeat: (layer, step_of(*g) // repeat, 0)))
        args.append(w)
        for lo, hi in zip(splits[:-1], splits[1:]):
            out_specs.append(pl.BlockSpec(
                (rows, hi - lo), lambda *g, repeat=repeat: (step_of(*g) // repeat, 0)))
            out_shape.append(jax.ShapeDtypeStruct((R, hi - lo), BF16))
        all_splits.append(tuple(splits))
    return in_specs, args, out_specs, out_shape, tuple(all_splits)


def _side_cast(cast_in, cast_out, cast_splits):
    k = 0
    for w_ref, splits in zip(cast_in, cast_splits):
        for lo, hi in zip(splits[:-1], splits[1:]):
            cast_out[k][...] = w_ref[:, lo:hi].astype(BF16)
            k += 1


def _rotary(t, cos2, sin2):
    return t * cos2 + pltpu.roll(t, RET_HEAD_DIM // 2, axis=1) * sin2


def _norm_qk_kernel(x_ref, g_ref, w_ref, cos_ref, sin_ref, o_ref, xn_ref):
    x = x_ref[...]
    ms = jnp.mean(x * x, axis=-1, keepdims=True)
    xn = (x * lax.rsqrt(ms + NORM_EPS) * g_ref[...]).astype(BF16)
    xn_ref[...] = xn
    acc = jnp.dot(xn, w_ref[...], preferred_element_type=F32)
    cos2 = cos_ref[...]
    sin2 = sin_ref[...]
    for h in range(acc.shape[1] // RET_HEAD_DIM):
        sl = slice(h * RET_HEAD_DIM, (h + 1) * RET_HEAD_DIM)
        o_ref[:, sl] = _rotary(acc[:, sl], cos2, sin2).astype(o_ref.dtype)


def _norm_qk_proj(x, g, w, cos2, sin2, *, n_cols, tm):
    M, K = x.shape
    d = RET_HEAD_DIM
    return pl.pallas_call(
        _norm_qk_kernel,
        grid=(M // tm,),
        in_specs=[pl.BlockSpec((tm, K), lambda i: (i, 0)),
                  pl.BlockSpec((1, K), lambda i: (0, 0)),
                  pl.BlockSpec((K, n_cols), lambda i: (0, 0)),
                  pl.BlockSpec((tm, d), lambda i: (i, 0)),
                  pl.BlockSpec((tm, d), lambda i: (i, 0))],
        out_specs=[pl.BlockSpec((tm, n_cols), lambda i: (i, 0)),
                   pl.BlockSpec((tm, K), lambda i: (i, 0))],
        out_shape=[jax.ShapeDtypeStruct((M, n_cols), BF16), jax.ShapeDtypeStruct((M, K), BF16)],
        compiler_params=_params(("parallel",)),
        name="norm_in_proj_qk",
    )(x, g.reshape(1, K), w, cos2, sin2)


def _mm_kernel(*refs, epilogue, cast_splits):
    n_cast_in = len(cast_splits)
    x_ref, w_ref = refs[:2]
    cast_in = refs[2:2 + n_cast_in]
    o_ref = refs[2 + n_cast_in]
    cast_out = refs[3 + n_cast_in:]
    _side_cast(cast_in, cast_out, cast_splits)
    acc = jnp.dot(x_ref[...], w_ref[...], preferred_element_type=F32)
    if epilogue == "sigmoid":
        acc = jax.nn.sigmoid(acc)
        o_ref[...] = acc.astype(o_ref.dtype)
    elif epilogue == "silu_upper_half":
        half = acc.shape[1] // 2
        gate = acc[:, half:]
        o_ref[:, :half] = acc[:, :half].astype(o_ref.dtype)
        o_ref[:, half:] = (gate * jax.nn.sigmoid(gate)).astype(o_ref.dtype)
    else:
        o_ref[...] = acc.astype(o_ref.dtype)


def _matmul(x, w, *, name, epilogue="none", col0=0, n_cols=None, casts=(), out_dtype, tm, tn):
    M, K = x.shape
    n_cols = w.shape[1] - col0 if n_cols is None else n_cols
    c0 = col0 // tn
    nj, ni = n_cols // tn, M // tm
    c_in, c_args, c_out, c_shape, cast_splits = _side_cast_specs(
        casts, ni * nj, lambda j, i: j * ni + i)
    outs = pl.pallas_call(
        functools.partial(_mm_kernel, epilogue=epilogue, cast_splits=cast_splits),
        grid=(nj, ni),
        in_specs=[pl.BlockSpec((tm, K), lambda j, i: (i, 0)),
                  pl.BlockSpec((K, tn), lambda j, i: (0, c0 + j))] + c_in,
        out_specs=[pl.BlockSpec((tm, tn), lambda j, i: (i, j))] + c_out,
        out_shape=[jax.ShapeDtypeStruct((M, n_cols), out_dtype)] + c_shape,
        compiler_params=_params(("parallel", "arbitrary")),
        name=name,
    )(x, w, *c_args)
    return outs[0], outs[1:]


def _mm_shift_kernel(*refs, tiles_per_seq, cast_splits):
    n_cast_in = len(cast_splits)
    x_ref, w_ref, mu_ref = refs[:3]
    cast_in = refs[3:3 + n_cast_in]
    o_ref = refs[3 + n_cast_in]
    cast_out = refs[4 + n_cast_in:-1]
    carry_ref = refs[-1]
    i = pl.program_id(1)

    @pl.when(i % tiles_per_seq == 0)
    def _():
        carry_ref[...] = jnp.zeros_like(carry_ref)

    _side_cast(cast_in, cast_out, cast_splits)
    z = jnp.dot(x_ref[...], w_ref[...], preferred_element_type=F32)
    n = z.shape[0]
    first = lax.broadcasted_iota(jnp.int32, z.shape, 0) == 0
    z_prev = jnp.where(first, carry_ref[...], pltpu.roll(z, 1, axis=0))
    carry_ref[...] = z[n - 1:n, :]
    o_ref[...] = (z + mu_ref[...] * (z_prev - z)).astype(o_ref.dtype)


def _matmul_shift(x, w, mu, *, seq_len, casts=(), name, tm, tn):
    M, K = x.shape
    N = w.shape[1]
    assert seq_len % tm == 0
    nj, ni = N // tn, M // tm
    c_in, c_args, c_out, c_shape, cast_splits = _side_cast_specs(
        casts, ni * nj, lambda j, i: j * ni + i)
    outs = pl.pallas_call(
        functools.partial(_mm_shift_kernel, tiles_per_seq=seq_len // tm, cast_splits=cast_splits),
        grid=(nj, ni),
        in_specs=[pl.BlockSpec((tm, K), lambda j, i: (i, 0)),
                  pl.BlockSpec((K, tn), lambda j, i: (0, j)),
                  pl.BlockSpec((1, tn), lambda j, i: (0, j))] + c_in,
        out_specs=[pl.BlockSpec((tm, tn), lambda j, i: (i, j))] + c_out,
        out_shape=[jax.ShapeDtypeStruct((M, N), F32)] + c_shape,
        scratch_shapes=[pltpu.VMEM((1, tn), F32)],
        compiler_params=_params(("arbitrary", "arbitrary")),
        name=name,
    )(x, w, mu.reshape(1, N), *c_args)
    return outs[0], outs[1:]


def _mlp_kernel(*refs, final_norm):
    x_ref, g_ref, wu_ref, wd_ref = refs[:4]
    fg_ref = refs[4] if final_norm else None
    o_ref, xn_ref = refs[-2:]
    j = pl.program_id(1)

    @pl.when(j == 0)
    def _():
        x = x_ref[...]
        ms = jnp.mean(x * x, axis=-1, keepdims=True)
        xn_ref[...] = (x * lax.rsqrt(ms + NORM_EPS) * g_ref[...]).astype(BF16)
        o_ref[...] = x

    h = jnp.dot(xn_ref[...], wu_ref[...], preferred_element_type=F32)
    h = jnp.square(jnp.maximum(h, 0.0)).astype(BF16)
    o_ref[...] += jnp.dot(h, wd_ref[...], preferred_element_type=F32)

    if final_norm:
        @pl.when(j == pl.num_programs(1) - 1)
        def _():
            y = o_ref[...]
            ms = jnp.mean(y * y, axis=-1, keepdims=True)
            o_ref[...] = y * lax.rsqrt(ms + NORM_EPS) * fg_ref[...]


def _mlp(x, g, w_up, w_down, final_g=None, *, tm, tf):
    M, D = x.shape
    F = w_up.shape[1]
    final_norm = final_g is not None
    vec = pl.BlockSpec((1, D), lambda i, j: (0, 0))
    in_specs = [pl.BlockSpec((tm, D), lambda i, j: (i, 0)), vec,
                pl.BlockSpec((D, tf), lambda i, j: (0, j)),
                pl.BlockSpec((tf, D), lambda i, j: (j, 0))]
    args = [x, g.reshape(1, D), w_up, w_down]
    if final_norm:
        in_specs.append(vec)
        args.append(final_g.reshape(1, D))
    return pl.pallas_call(
        functools.partial(_mlp_kernel, final_norm=final_norm),
        grid=(M // tm, F // tf),
        in_specs=in_specs,
        out_specs=pl.BlockSpec((tm, D), lambda i, j: (i, 0)),
        out_shape=jax.ShapeDtypeStruct((M, D), F32),
        scratch_shapes=[pltpu.VMEM((tm, D), BF16)],
        compiler_params=_params(("parallel", "arbitrary")),
        name="mlp_final" if final_norm else "mlp",
    )(*args)


def _merge_out_kernel(ya_ref, yb_ref, wa_ref, wb_ref, ga_ref, gb_ref, wo_ref, x_ref, o_ref):
    D = o_ref.shape[1]
    n_split = 2
    width = D // n_split
    ya = ya_ref[...]
    yb = yb_ref[...]
    acc = x_ref[...]
    for c in range(n_split):
        cs = slice(c * width, (c + 1) * width)
        a = jnp.dot(ya, wa_ref[:, cs], preferred_element_type=F32)
        b = jnp.dot(yb, wb_ref[:, cs], preferred_element_type=F32)
        m = (ga_ref[:, cs].astype(F32) * a + gb_ref[:, cs].astype(F32) * b).astype(BF16)
        acc = acc + jnp.dot(m, wo_ref[cs, :], preferred_element_type=F32)
    o_ref[...] = acc


def _merge_out(ya, yb, wa, wb, gates, wo, x, *, tm):
    M, K = ya.shape
    D = wo.shape[1]
    resident = lambda a: pl.BlockSpec(a.shape, lambda i: (0, 0), pipeline_mode=pl.Buffered(1))
    return pl.pallas_call(
        _merge_out_kernel,
        grid=(M // tm,),
        in_specs=[
            pl.BlockSpec((tm, K), lambda i: (i, 0)),
            pl.BlockSpec((tm, K), lambda i: (i, 0)),
            resident(wa), resident(wb),
            pl.BlockSpec((tm, D), lambda i: (i, 0)),
            pl.BlockSpec((tm, D), lambda i: (i, 1)),
            resident(wo),
            pl.BlockSpec((tm, D), lambda i: (i, 0)),
        ],
        out_specs=pl.BlockSpec((tm, D), lambda i: (i, 0)),
        out_shape=jax.ShapeDtypeStruct((M, D), F32),
        compiler_params=_params(("parallel",)),
        name="merge_out_proj",
    )(ya, yb, wa, wb, gates, gates, wo, x)


def _rope_kernel(pos_ref, cos_ref, sin_ref):
    half = RET_HEAD_DIM // 2
    pos = pos_ref[...].astype(F32)
    lane = lax.broadcasted_iota(jnp.int32, (1, RET_HEAD_DIM), 1)
    j = jnp.where(lane < half, lane, lane - half).astype(F32)
    inv_freq = jnp.exp(j * (-math.log(ROPE_BASE) / half))
    ang = pos * inv_freq
    cos_ref[...] = jnp.cos(ang)
    sin_ref[...] = jnp.where(lane < half, -1.0, 1.0) * jnp.sin(ang)


def _rope_tables(positions, *, tm):
    M = positions.size
    pos = positions.reshape(M, 1)
    return pl.pallas_call(
        _rope_kernel,
        grid=(M // tm,),
        in_specs=[pl.BlockSpec((tm, 1), lambda i: (i, 0))],
        out_specs=[pl.BlockSpec((tm, RET_HEAD_DIM), lambda i: (i, 0))] * 2,
        out_shape=[jax.ShapeDtypeStruct((M, RET_HEAD_DIM), F32)] * 2,
        compiler_params=_params(("parallel",)),
        name="rope_tables",
    )(pos)


def _retention_kernel(q_ref, k_ref, v_ref, gate_ref, gng_ref, gnb_ref,
                      o_ref, state_ref, inner_ref, qdec_ref, kdec_ref):
    C = RET_CHUNK
    d = RET_HEAD_DIM
    chunks = [slice(c * C, (c + 1) * C) for c in range(q_ref.shape[1] // C)]
    log_gammas = [math.log1p(-(2.0 ** (-5.0 - h))) for h in range(RET_HEADS)]

    @pl.when(pl.program_id(1) == 0)
    def _():
        state_ref[...] = jnp.zeros_like(state_ref)
        row = lax.broadcasted_iota(jnp.int32, (C, C), 0)
        col = lax.broadcasted_iota(jnp.int32, (C, C), 1)
        diff = (row - col).astype(F32)
        rowd = lax.broadcasted_iota(jnp.int32, (C, d), 0).astype(F32)
        scale = d ** -0.5
        for h, lg in enumerate(log_gammas):
            inner_ref[h] = jnp.where(row >= col, jnp.exp(jnp.maximum(diff, 0.0) * lg) * scale, 0.0)
            qdec_ref[h] = jnp.exp((rowd + 1.0) * lg)
            kdec_ref[h] = jnp.exp((C - 1.0 - rowd) * lg) * scale

    heads = range(RET_HEADS)
    sls = [slice(h * d, (h + 1) * d) for h in heads]
    keys = [(c, h) for c in range(len(chunks)) for h in heads]
    qb = {(c, h): q_ref[0, chunks[c], sls[h]] for c, h in keys}
    kb = {(c, h): k_ref[0, chunks[c], sls[h]] for c, h in keys}
    vb = {(c, h): v_ref[0, chunks[c], sls[h]] for c, h in keys}
    scores = {q: lax.dot_general(qb[q], kb[q], _NT, preferred_element_type=F32) for q in keys}
    inner = {q: _bdot(scores[q] * inner_ref[q[1]], vb[q]) for q in keys}
    kv = {q: lax.dot_general((kb[q].astype(F32) * kdec_ref[q[1]]).astype(BF16), vb[q], _TN,
                             preferred_element_type=F32) for q in keys}
    state = [state_ref[h] for h in heads]
    cross = {}
    for c, h in keys:
        cross[c, h] = _bdot(qb[c, h], state[h]) * qdec_ref[h]
        state[h] = math.exp(C * log_gammas[h]) * state[h] + kv[c, h]
    for h in heads:
        state_ref[h] = state[h]
    for c, h in keys:
        rc, sl = chunks[c], sls[h]
        y = inner[c, h] + cross[c, h]
        mean = jnp.mean(y, axis=-1, keepdims=True)
        yc = y - mean
        var = jnp.mean(yc * yc, axis=-1, keepdims=True)
        yn = yc * lax.rsqrt(var + RET_GN_EPS) * gng_ref[:, sl] + gnb_ref[:, sl]
        o_ref[0, rc, sl] = (gate_ref[0, rc, sl].astype(F32) * yn).astype(o_ref.dtype)


def _retention(qk, vg, gn_g, gn_b):
    B, S, _ = qk.shape
    W = RET_HEADS * RET_HEAD_DIM
    C = RET_CHUNK
    R = RET_ROWS
    zspec = lambda off: pl.BlockSpec((1, R, W), lambda b, n: (b, n, off))
    pspec = pl.BlockSpec((1, W), lambda b, n: (0, 0))
    return pl.pallas_call(
        _retention_kernel,
        grid=(B, S // R),
        in_specs=[zspec(0), zspec(1), zspec(0), zspec(1), pspec, pspec],
        out_specs=pl.BlockSpec((1, R, W), lambda b, n: (b, n, 0)),
        out_shape=jax.ShapeDtypeStruct((B, S, W), BF16),
        scratch_shapes=[pltpu.VMEM((RET_HEADS, RET_HEAD_DIM, RET_HEAD_DIM), F32),
                        pltpu.VMEM((RET_HEADS, C, C), F32),
                        pltpu.VMEM((RET_HEADS, C, RET_HEAD_DIM), F32),
                        pltpu.VMEM((RET_HEADS, C, RET_HEAD_DIM), F32)],
        compiler_params=_params(("parallel", "arbitrary")),
        name="retention",
    )(qk, qk, vg, vg, gn_g.reshape(1, W), gn_b.reshape(1, W))


def _split2(x):
    hi = x.astype(BF16)
    return hi, (x - hi.astype(F32)).astype(BF16)


def _rwkv_staged_kernel(r_ref, k_ref, v_ref, lo_ref,
                        w0_ref, wup_ref, a0_ref, aup_ref, gup_ref, kk_ref, ka_ref, rk_ref,
                        lng_ref, lnb_ref, o_ref, state_ref):
    R = r_ref.shape[1]
    W = r_ref.shape[2]
    C = RWKV_CHUNK
    hd = RWKV_HEAD_DIM
    P = 2 * hd
    assert P == LANES and 2 * C == P and R % C == 0
    chunks = range(R // C)
    rows = [slice(c * C, (c + 1) * C) for c in chunks]
    pairs = range(W // P)
    sls = [slice(p * P, (p + 1) * P) for p in pairs]

    @pl.when(pl.program_id(1) == 0)
    def _():
        state_ref[...] = jnp.zeros_like(state_ref)

    lo = lo_ref[0]
    tw = jnp.tanh(lo[:, 0:LORA_W]).astype(BF16)
    za = lo[:, LORA_W:LORA_W + LORA_A].astype(BF16)
    sg = jax.nn.sigmoid(lo[:, LORA_W + LORA_A:]).astype(BF16)

    r = r_ref[0]
    k = k_ref[0]
    v = v_ref[0]
    logdec = -math.exp(-0.5) * jax.nn.sigmoid(
        w0_ref[...] + jnp.dot(tw, wup_ref[...], preferred_element_type=F32))
    a = jax.nn.sigmoid(a0_ref[...] + jnp.dot(za, aup_ref[...], preferred_element_type=F32))
    g = jnp.dot(sg, gup_ref[...], preferred_element_type=F32)
    kk_raw = k * kk_ref[...]
    k = k * (1.0 + (a - 1.0) * ka_ref[...])

    tri = jnp.where(lax.broadcasted_iota(jnp.int32, (C, C), 0)
                    >= lax.broadcasted_iota(jnp.int32, (C, C), 1), 1.0, 0.0).astype(BF16)
    ld_hi = logdec.astype(BF16)
    ld_mid, ld_lo = _split2(logdec - ld_hi.astype(F32))
    cum = [jnp.dot(tri, ld_hi[rc], preferred_element_type=F32)
           + jnp.dot(tri, ld_mid[rc], preferred_element_type=F32)
           + jnp.dot(tri, ld_lo[rc], preferred_element_type=F32) for rc in rows]

    row = lax.broadcasted_iota(jnp.int32, (C, P), 0)
    lane = lax.broadcasted_iota(jnp.int32, (C, P), 1)
    head0 = lane < hd
    cj = jnp.where(head0, lane, lane - hd)
    strict = row > cj
    incl = row >= cj
    eye2 = jnp.where(row == cj, 1.0, 0.0)
    r2 = lax.broadcasted_iota(jnp.int32, (P, P), 0)
    l2 = lax.broadcasted_iota(jnp.int32, (P, P), 1)
    bdmask = (r2 < hd) == (l2 < hd)
    zero = jnp.zeros((), BF16)

    def blockdiag(x):
        return jnp.where(bdmask, jnp.concatenate([x, x], axis=0), jnp.zeros((), x.dtype))

    def segsum(x):
        h0 = lax.broadcasted_iota(jnp.int32, x.shape, 1) < hd
        s0 = jnp.sum(jnp.where(h0, x, 0.0), axis=-1, keepdims=True)
        s1 = jnp.sum(jnp.where(h0, 0.0, x), axis=-1, keepdims=True)
        return jnp.where(h0, s0, s1)

    kk_sq = [segsum(jnp.square(kk_raw[:, sl])) for sl in sls]
    bonus = [segsum(r[:, sl] * k[:, sl] * rk_ref[:, sl]) for sl in sls]

    lhs, lv, m_rb, tb, bk_end, dec_chunk, vb = ({} for _ in range(7))

    def independent(group):
        keys = [(c, p) for c in group for p in pairs]
        gram = {}
        for q in keys:
            rc, sl = rows[q[0]], sls[q[1]]
            kk = kk_raw[rc, sl] * lax.rsqrt(jnp.maximum(kk_sq[q[1]][rc], 1e-24))
            cum_p = cum[q[0]][:, sl]
            cum_last = cum_p[C - 1:C, :]
            dec_to_end = jnp.exp(cum_last - cum_p)
            inv_p = jnp.exp(-cum_p)
            kka = kk * a[rc, sl]
            rt = r[rc, sl] * jnp.exp(cum_p)
            kt = (k[rc, sl] * inv_p).astype(BF16)
            at = -kk * jnp.exp(cum_p - logdec[rc, sl])
            bt = (kka * inv_p).astype(BF16)
            bk_end[q] = jnp.concatenate([kka * dec_to_end, k[rc, sl] * dec_to_end],
                                        axis=0).astype(BF16)
            dec_chunk[q] = jnp.exp(cum_last)
            vb[q] = v[rc, sl].astype(BF16)
            lhs[q] = jnp.concatenate([at, rt], axis=0).astype(BF16)
            rhs = jnp.concatenate([jnp.where(head0, bt, zero), jnp.where(head0, zero, bt),
                                   jnp.where(head0, kt, zero), jnp.where(head0, zero, kt)], axis=0)
            gram[q] = lax.dot_general(lhs[q], rhs, _NT, preferred_element_type=F32)
        yield

        l_ab = {q: jnp.where(strict, gram[q][:C, :P], 0.0) for q in keys}
        for q in keys:
            l_akrk = jnp.concatenate([jnp.where(strict, gram[q][:C, P:], 0.0),
                                      jnp.where(incl, gram[q][C:, P:], 0.0)], axis=0)
            lv[q] = jnp.dot(l_akrk.astype(BF16), blockdiag(vb[q]), preferred_element_type=F32)
            m_rb[q] = jnp.where(incl, gram[q][C:, :P], 0.0).astype(BF16)
        yield

        t = {q: eye2 + l_ab[q] for q in keys}
        m = {q: _bdot(l_ab[q], blockdiag(l_ab[q].astype(BF16))) for q in keys}
        yield
        n_sq = int(math.log2(C)) - 1
        for step in range(n_sq):
            for q in keys:
                mb = m[q].astype(BF16)
                if step < n_sq - 1:
                    both = jnp.concatenate([blockdiag(t[q].astype(BF16)), blockdiag(mb)], axis=1)
                    prod = jnp.dot(mb, both, preferred_element_type=F32)
                    t[q] = t[q] + prod[:, :P]
                    m[q] = prod[:, P:]
                else:
                    t[q] = t[q] + jnp.dot(mb, blockdiag(t[q].astype(BF16)),
                                          preferred_element_type=F32)
            yield
        for q in keys:
            tb[q] = t[q].astype(BF16)

    state = [state_ref[p] for p in pairs]

    def dependent(group):
        for c in group:
            rc = rows[c]
            ar = [lax.dot_general(lhs[c, p], state[p].astype(BF16), _NT,
                                  preferred_element_type=F32) for p in pairs]
            yield
            ub = [jnp.dot(tb[c, p], blockdiag((ar[p][:C] + lv[c, p][:C]).astype(BF16)),
                          preferred_element_type=F32).astype(BF16) for p in pairs]
            yield
            y = [ar[p][C:] + lv[c, p][C:]
                 + jnp.dot(m_rb[c, p], blockdiag(ub[p]), preferred_element_type=F32) for p in pairs]
            for p in pairs:
                upd = lax.dot_general(jnp.concatenate([ub[p], vb[c, p]], axis=0), bk_end[c, p],
                                      _TN, preferred_element_type=F32)
                state[p] = state[p] * dec_chunk[c, p] + jnp.where(bdmask, upd, 0.0)
            yield
            yc = [y[p] - segsum(y[p]) * (1.0 / hd) for p in pairs]
            var = [segsum(jnp.square(yc[p])) * (1.0 / hd) for p in pairs]
            for p, sl in zip(pairs, sls):
                yn = yc[p] * lax.rsqrt(var[p] + RWKV_LN_EPS) * lng_ref[:, sl] + lnb_ref[:, sl]
                o_ref[0, rc, sl] = ((yn + bonus[p][rc] * v[rc, sl]) * g[rc, sl]).astype(o_ref.dtype)
            yield

    def run(*stages):
        stages = list(stages)
        while stages:
            for stage in tuple(stages):
                if next(stage, stages) is stages:
                    stages.remove(stage)

    groups = [list(chunks[i:i + RWKV_GROUP]) for i in range(0, len(chunks), RWKV_GROUP)]
    run(independent(groups[0]))
    for gi, group in enumerate(groups):
        ahead = [independent(groups[gi + 1])] if gi + 1 < len(groups) else []
        run(dependent(group), *ahead)
    for p in pairs:
        state_ref[p] = state[p]


def _rwkv(z, w0, w_up, a0, a_up, g_up, k_k, k_a, r_k, ln_g, ln_b):
    B, S, _ = z.shape
    W = w0.shape[-1]
    C = RWKV_ROWS
    LO = LORA_W + LORA_A + LORA_G
    P = 2 * RWKV_HEAD_DIM
    lo0 = 3 * W // LO
    zspec = lambda off: pl.BlockSpec((1, C, W), lambda b, n: (b, n, off))
    full = lambda a: pl.BlockSpec(a.shape, lambda b, n: (0,) * a.ndim)
    row = lambda a: a.reshape(1, -1)
    params = [row(w0), w_up.astype(BF16), row(a0), a_up.astype(BF16), g_up.astype(BF16),
              row(k_k), row(k_a), row(r_k), row(ln_g), row(ln_b)]
    return pl.pallas_call(
        _rwkv_staged_kernel,
        grid=(B, S // C),
        in_specs=[zspec(0), zspec(1), zspec(2),
                  pl.BlockSpec((1, C, LO), lambda b, n: (b, n, lo0))] + [full(a) for a in params],
        out_specs=pl.BlockSpec((1, C, W), lambda b, n: (b, n, 0)),
        out_shape=jax.ShapeDtypeStruct((B, S, W), BF16),
        scratch_shapes=[pltpu.VMEM((W // P, P, P), F32)],
        compiler_params=_params(("parallel", "arbitrary")),
        name="rwkv7",
    )(z, z, z, z, *params)


def kernel(x, positions, norm1_g, w_in, ret_gn_g, ret_gn_b, rwkv_mu, rwkv_w0, rwkv_w_up, rwkv_a0, rwkv_a_up, rwkv_g_up, rwkv_k_k, rwkv_k_a, rwkv_r_k, rwkv_ln_g, rwkv_ln_b, w_branch_a, w_branch_b, w_out, norm2_g, mlp_up, mlp_down, final_g):
    B, S, D = x.shape
    M = B * S
    depth = w_in.shape[0]
    ret_w = ret_gn_g.shape[-1]
    rwkv_w = rwkv_w0.shape[-1]
    ret_cols = 4 * ret_w
    rwkv_cols = rwkv_mu.shape[-1]
    assert ret_w == RET_HEADS * RET_HEAD_DIM and rwkv_w % (2 * RWKV_HEAD_DIM) == 0
    assert rwkv_cols == 3 * rwkv_w + LORA_W + LORA_A + LORA_G
    assert S % PROJ_ROWS == 0 and S % RET_ROWS == 0 and S % RWKV_ROWS == 0
    assert M % ROPE_ROWS == 0 and M % MLP_ROWS == 0 and mlp_up.shape[-1] % MLP_HIDDEN == 0
    cos2, sin2 = _rope_tables(positions, tm=ROPE_ROWS)

    in_splits = (0, ret_cols, ret_cols + rwkv_cols, w_in.shape[-1])
    w_in0 = w_in[0]
    w_ret, w_rwkv, w_gate = [w_in0[:, in_splits[i]:in_splits[i + 1]].astype(BF16) for i in range(3)]
    full = lambda t, layer: (t, layer, (0, t.shape[-1]))
    mix_weights = (w_branch_a, w_branch_b, w_out)
    mlp_weights = (mlp_up, mlp_down)

    xf = x.reshape(M, D)
    for l in range(depth):
        more = l + 1 < depth
        next_mix = ([(w_in, l + 1, in_splits)] + [full(t, l + 1) for t in mix_weights]) if more else []
        next_mlp = [full(t, l + 1) for t in mlp_weights] if more else []
        z_qk, xn = _norm_qk_proj(xf, norm1_g[l], w_ret, cos2, sin2, n_cols=2 * ret_w,
                                 tm=QK_PROJ_ROWS)
        z_vg, cast_vg = _matmul(xn, w_ret, name="in_proj_vg", epilogue="silu_upper_half",
                                col0=2 * ret_w, casts=next_mix if l == 0 else (),
                                out_dtype=BF16, tm=PROJ_ROWS, tn=2 * ret_w)
        gate_casts = [full(t, 0) for t in mix_weights + mlp_weights] if l == 0 else next_mix
        z_gate, cast_gate = _matmul(xn, w_gate, name="in_proj_gate", epilogue="sigmoid",
                                    casts=gate_casts, out_dtype=BF16, tm=PROJ_ROWS, tn=PROJ_COLS)
        if l == 0:
            w_a, w_b, w_o, w_up, w_down = cast_gate
        z_rwkv, cast_rwkv = _matmul_shift(xn, w_rwkv, rwkv_mu[l], seq_len=S, casts=next_mlp,
                                          name="in_proj_rwkv", tm=PROJ_ROWS, tn=rwkv_cols // 2)
        y_ret = _retention(z_qk.reshape(B, S, -1), z_vg.reshape(B, S, -1),
                           ret_gn_g[l], ret_gn_b[l])
        y_rwkv = _rwkv(z_rwkv.reshape(B, S, -1), rwkv_w0[l], rwkv_w_up[l],
                       rwkv_a0[l], rwkv_a_up[l], rwkv_g_up[l], rwkv_k_k[l], rwkv_k_a[l],
                       rwkv_r_k[l], rwkv_ln_g[l], rwkv_ln_b[l])
        xf = _merge_out(y_ret.reshape(M, ret_w), y_rwkv.reshape(M, rwkv_w),
                        w_a, w_b, z_gate, w_o, xf, tm=MERGE_ROWS)
        xf = _mlp(xf, norm2_g[l], w_up, w_down, None if more else final_g,
                  tm=MLP_ROWS, tf=MLP_HIDDEN)
        if more:
            w_ret, w_rwkv, w_gate, w_a, w_b, w_o = cast_vg if l == 0 else cast_gate
            w_up, w_down = cast_rwkv
    return xf.reshape(B, S, D)
```

```python
import functools
import math

import jax
import jax.numpy as jnp
from jax import lax
from jax.experimental import pallas as pl
from jax.experimental.pallas import tpu as pltpu

F32 = jnp.float32
BF16 = jnp.bfloat16

RET_HEADS = 8
RET_HEAD_DIM = 128
RET_GN_EPS = 1e-5
ROPE_BASE = 10000.0
RWKV_HEAD_DIM = 64
RWKV_LN_EPS = 64e-5
NORM_EPS = 1e-6
LORA_W = 64
LORA_A = 64
LORA_G = 128

LANES = 128
BF16_SUBLANES = 16
V7X_VMEM_LIMIT_BYTES = 56 * 1024 * 1024

ROPE_ROWS = 2048
QK_PROJ_ROWS = 512
PROJ_ROWS = 1024
PROJ_COLS = 2048
MERGE_ROWS = 512
MLP_ROWS = 1024
MLP_HIDDEN = 512
RET_CHUNK = 128
RET_ROWS = 512
RWKV_CHUNK = 64
RWKV_ROWS = 512
RWKV_GROUP = 2

_NT = (((1,), (1,)), ((), ()))
_TN = (((0,), (0,)), ((), ()))


def _params(semantics):
    return pltpu.CompilerParams(dimension_semantics=semantics,
                                vmem_limit_bytes=V7X_VMEM_LIMIT_BYTES)


def _bdot(a, b):
    return jnp.dot(a.astype(BF16), b.astype(BF16), preferred_element_type=F32)


def _side_cast_specs(casts, n_steps, step_of):
    in_specs, args, out_specs, out_shape, all_splits = [], [], [], [], []
    for w, layer, splits in casts:
        _, R, N = w.shape
        rows = max(BF16_SUBLANES, R // n_steps)
        repeat = rows * n_steps // R
        in_specs.append(pl.BlockSpec(
            (None, rows, N),
            lambda *g, layer=layer, repeat=repeat: (layer, step_of(*g) // repeat, 0)))
        args.append(w)
        for lo, hi in zip(splits[:-1], splits[1:]):
            out_specs.append(pl.BlockSpec(
                (rows, hi - lo), lambda *g, repeat=repeat: (step_of(*g) // repeat, 0)))
            out_shape.append(jax.ShapeDtypeStruct((R, hi - lo), BF16))
        all_splits.append(tuple(splits))
    return in_specs, args, out_specs, out_shape, tuple(all_splits)


def _side_cast(cast_in, cast_out, cast_splits):
    k = 0
    for w_ref, splits in zip(cast_in, cast_splits):
        for lo, hi in zip(splits[:-1], splits[1:]):
            cast_out[k][...] = w_ref[:, lo:hi].astype(BF16)
            k += 1


def _rotary(t, cos2, sin2):
    return t * cos2 + pltpu.roll(t, RET_HEAD_DIM // 2, axis=1) * sin2


def _norm_qk_kernel(*refs, cast_splits):
    n_cast_in = len(cast_splits)
    x_ref, g_ref, w_ref, cos_ref, sin_ref = refs[:5]
    cast_in = refs[5:5 + n_cast_in]
    o_ref, xn_ref = refs[5 + n_cast_in:7 + n_cast_in]
    cast_out = refs[7 + n_cast_in:]
    _side_cast(cast_in, cast_out, cast_splits)
    x = x_ref[...]
    ms = jnp.mean(x * x, axis=-1, keepdims=True)
    xn = (x * lax.rsqrt(ms + NORM_EPS) * g_ref[...]).astype(BF16)
    xn_ref[...] = xn
    acc = jnp.dot(xn, w_ref[...], preferred_element_type=F32)
    cos2 = cos_ref[...]
    sin2 = sin_ref[...]
    for h in range(acc.shape[1] // RET_HEAD_DIM):
        sl = slice(h * RET_HEAD_DIM, (h + 1) * RET_HEAD_DIM)
        o_ref[:, sl] = _rotary(acc[:, sl], cos2, sin2).astype(o_ref.dtype)


def _norm_qk_proj(x, g, w, cos2, sin2, *, n_cols, casts=(), tm):
    M, K = x.shape
    d = RET_HEAD_DIM
    c_in, c_args, c_out, c_shape, cast_splits = _side_cast_specs(casts, M // tm, lambda i: i)
    outs = pl.pallas_call(
        functools.partial(_norm_qk_kernel, cast_splits=cast_splits),
        grid=(M // tm,),
        in_specs=[pl.BlockSpec((tm, K), lambda i: (i, 0)),
                  pl.BlockSpec((1, K), lambda i: (0, 0)),
                  pl.BlockSpec((K, n_cols), lambda i: (0, 0)),
                  pl.BlockSpec((tm, d), lambda i: (i, 0)),
                  pl.BlockSpec((tm, d), lambda i: (i, 0))] + c_in,
        out_specs=[pl.BlockSpec((tm, n_cols), lambda i: (i, 0)),
                   pl.BlockSpec((tm, K), lambda i: (i, 0))] + c_out,
        out_shape=[jax.ShapeDtypeStruct((M, n_cols), BF16),
                   jax.ShapeDtypeStruct((M, K), BF16)] + c_shape,
        compiler_params=_params(("parallel",)),
        name="norm_in_proj_qk",
    )(x, g.reshape(1, K), w, cos2, sin2, *c_args)
    return outs[0], outs[1], outs[2:]


def _mm_kernel(*refs, epilogue, cast_splits):
    n_cast_in = len(cast_splits)
    x_ref, w_ref = refs[:2]
    cast_in = refs[2:2 + n_cast_in]
    o_ref = refs[2 + n_cast_in]
    cast_out = refs[3 + n_cast_in:]
    _side_cast(cast_in, cast_out, cast_splits)
    acc = jnp.dot(x_ref[...], w_ref[...], preferred_element_type=F32)
    if epilogue == "sigmoid":
        acc = jax.nn.sigmoid(acc)
        o_ref[...] = acc.astype(o_ref.dtype)
    elif epilogue == "silu_upper_half":
        half = acc.shape[1] // 2
        gate = acc[:, half:]
        o_ref[:, :half] = acc[:, :half].astype(o_ref.dtype)
        o_ref[:, half:] = (gate * jax.nn.sigmoid(gate)).astype(o_ref.dtype)
    else:
        o_ref[...] = acc.astype(o_ref.dtype)


def _matmul(x, w, *, name, epilogue="none", col0=0, n_cols=None, casts=(), out_dtype, tm, tn):
    M, K = x.shape
    n_cols = w.shape[1] - col0 if n_cols is None else n_cols
    c0 = col0 // tn
    nj, ni = n_cols // tn, M // tm
    c_in, c_args, c_out, c_shape, cast_splits = _side_cast_specs(
        casts, ni * nj, lambda j, i: j * ni + i)
    outs = pl.pallas_call(
        functools.partial(_mm_kernel, epilogue=epilogue, cast_splits=cast_splits),
        grid=(nj, ni),
        in_specs=[pl.BlockSpec((tm, K), lambda j, i: (i, 0)),
                  pl.BlockSpec((K, tn), lambda j, i: (0, c0 + j))] + c_in,
        out_specs=[pl.BlockSpec((tm, tn), lambda j, i: (i, j))] + c_out,
        out_shape=[jax.ShapeDtypeStruct((M, n_cols), out_dtype)] + c_shape,
        compiler_params=_params(("parallel", "arbitrary")),
        name=name,
    )(x, w, *c_args)
    return outs[0], outs[1:]


def _mm_shift_kernel(*refs, tiles_per_seq, cast_splits):
    n_cast_in = len(cast_splits)
    x_ref, w_ref, mu_ref = refs[:3]
    cast_in = refs[3:3 + n_cast_in]
    o_ref = refs[3 + n_cast_in]
    cast_out = refs[4 + n_cast_in:-1]
    carry_ref = refs[-1]
    i = pl.program_id(1)

    @pl.when(i % tiles_per_seq == 0)
    def _():
        carry_ref[...] = jnp.zeros_like(carry_ref)

    _side_cast(cast_in, cast_out, cast_splits)
    z = jnp.dot(x_ref[...], w_ref[...], preferred_element_type=F32)
    n = z.shape[0]
    first = lax.broadcasted_iota(jnp.int32, z.shape, 0) == 0
    z_prev = jnp.where(first, carry_ref[...], pltpu.roll(z, 1, axis=0))
    carry_ref[...] = z[n - 1:n, :]
    o_ref[...] = (z + mu_ref[...] * (z_prev - z)).astype(o_ref.dtype)


def _matmul_shift(x, w, mu, *, seq_len, casts=(), name, tm, tn):
    M, K = x.shape
    N = w.shape[1]
    assert seq_len % tm == 0
    nj, ni = N // tn, M // tm
    c_in, c_args, c_out, c_shape, cast_splits = _side_cast_specs(
        casts, ni * nj, lambda j, i: j * ni + i)
    outs = pl.pallas_call(
        functools.partial(_mm_shift_kernel, tiles_per_seq=seq_len // tm, cast_splits=cast_splits),
        grid=(nj, ni),
        in_specs=[pl.BlockSpec((tm, K), lambda j, i: (i, 0)),
                  pl.BlockSpec((K, tn), lambda j, i: (0, j)),
                  pl.BlockSpec((1, tn), lambda j, i: (0, j))] + c_in,
        out_specs=[pl.BlockSpec((tm, tn), lambda j, i: (i, j))] + c_out,
        out_shape=[jax.ShapeDtypeStruct((M, N), F32)] + c_shape,
        scratch_shapes=[pltpu.VMEM((1, tn), F32)],
        compiler_params=_params(("arbitrary", "arbitrary")),
        name=name,
    )(x, w, mu.reshape(1, N), *c_args)
    return outs[0], outs[1:]


def _mlp_kernel(*refs, final_norm):
    x_ref, g_ref, wu_ref, wd_ref = refs[:4]
    fg_ref = refs[4] if final_norm else None
    o_ref, xn_ref = refs[-2:]
    j = pl.program_id(1)

    @pl.when(j == 0)
    def _():
        x = x_ref[...]
        ms = jnp.mean(x * x, axis=-1, keepdims=True)
        xn_ref[...] = (x * lax.rsqrt(ms + NORM_EPS) * g_ref[...]).astype(BF16)
        o_ref[...] = x

    h = jnp.dot(xn_ref[...], wu_ref[...], preferred_element_type=F32)
    h = jnp.square(jnp.maximum(h, 0.0)).astype(BF16)
    o_ref[...] += jnp.dot(h, wd_ref[...], preferred_element_type=F32)

    if final_norm:
        @pl.when(j == pl.num_programs(1) - 1)
        def _():
            y = o_ref[...]
            ms = jnp.mean(y * y, axis=-1, keepdims=True)
            o_ref[...] = y * lax.rsqrt(ms + NORM_EPS) * fg_ref[...]


def _mlp(x, g, w_up, w_down, final_g=None, *, tm, tf):
    M, D = x.shape
    F = w_up.shape[1]
    final_norm = final_g is not None
    vec = pl.BlockSpec((1, D), lambda i, j: (0, 0))
    in_specs = [pl.BlockSpec((tm, D), lambda i, j: (i, 0)), vec,
                pl.BlockSpec((D, tf), lambda i, j: (0, j)),
                pl.BlockSpec((tf, D), lambda i, j: (j, 0))]
    args = [x, g.reshape(1, D), w_up, w_down]
    if final_norm:
        in_specs.append(vec)
        args.append(final_g.reshape(1, D))
    return pl.pallas_call(
        functools.partial(_mlp_kernel, final_norm=final_norm),
        grid=(M // tm, F // tf),
        in_specs=in_specs,
        out_specs=pl.BlockSpec((tm, D), lambda i, j: (i, 0)),
        out_shape=jax.ShapeDtypeStruct((M, D), F32),
        scratch_shapes=[pltpu.VMEM((tm, D), BF16)],
        compiler_params=_params(("parallel", "arbitrary")),
        name="mlp_final" if final_norm else "mlp",
    )(*args)


def _merge_out_kernel(ya_ref, yb_ref, wa_ref, wb_ref, ga_ref, gb_ref, wo_ref, x_ref, o_ref):
    D = o_ref.shape[1]
    n_split = 2
    width = D // n_split
    ya = ya_ref[...]
    yb = yb_ref[...]
    acc = x_ref[...]
    for c in range(n_split):
        cs = slice(c * width, (c + 1) * width)
        a = jnp.dot(ya, wa_ref[:, cs], preferred_element_type=F32)
        b = jnp.dot(yb, wb_ref[:, cs], preferred_element_type=F32)
        m = (ga_ref[:, cs].astype(F32) * a + gb_ref[:, cs].astype(F32) * b).astype(BF16)
        acc = acc + jnp.dot(m, wo_ref[cs, :], preferred_element_type=F32)
    o_ref[...] = acc


def _merge_out(ya, yb, wa, wb, gates, wo, x, *, tm):
    M, K = ya.shape
    D = wo.shape[1]
    resident = lambda a: pl.BlockSpec(a.shape, lambda i: (0, 0), pipeline_mode=pl.Buffered(1))
    return pl.pallas_call(
        _merge_out_kernel,
        grid=(M // tm,),
        in_specs=[
            pl.BlockSpec((tm, K), lambda i: (i, 0)),
            pl.BlockSpec((tm, K), lambda i: (i, 0)),
            resident(wa), resident(wb),
            pl.BlockSpec((tm, D), lambda i: (i, 0)),
            pl.BlockSpec((tm, D), lambda i: (i, 1)),
            resident(wo),
            pl.BlockSpec((tm, D), lambda i: (i, 0)),
        ],
        out_specs=pl.BlockSpec((tm, D), lambda i: (i, 0)),
        out_shape=jax.ShapeDtypeStruct((M, D), F32),
        compiler_params=_params(("parallel",)),
        name="merge_out_proj",
    )(ya, yb, wa, wb, gates, gates, wo, x)


def _rope_kernel(pos_ref, cos_ref, sin_ref):
    half = RET_HEAD_DIM // 2
    pos = pos_ref[...].astype(F32)
    lane = lax.broadcasted_iota(jnp.int32, (1, RET_HEAD_DIM), 1)
    j = jnp.where(lane < half, lane, lane - half).astype(F32)
    inv_freq = jnp.exp(j * (-math.log(ROPE_BASE) / half))
    ang = pos * inv_freq
    cos_ref[...] = jnp.cos(ang)
    sin_ref[...] = jnp.where(lane < half, -1.0, 1.0) * jnp.sin(ang)


def _rope_tables(positions, *, tm):
    M = positions.size
    pos = positions.reshape(M, 1)
    return pl.pallas_call(
        _rope_kernel,
        grid=(M // tm,),
        in_specs=[pl.BlockSpec((tm, 1), lambda i: (i, 0))],
        out_specs=[pl.BlockSpec((tm, RET_HEAD_DIM), lambda i: (i, 0))] * 2,
        out_shape=[jax.ShapeDtypeStruct((M, RET_HEAD_DIM), F32)] * 2,
        compiler_params=_params(("parallel",)),
        name="rope_tables",
    )(pos)


def _retention_kernel(q_ref, k_ref, v_ref, gate_ref, gng_ref, gnb_ref,
                      o_ref, state_ref, inner_ref, qdec_ref, kdec_ref):
    C = RET_CHUNK
    d = RET_HEAD_DIM
    chunks = [slice(c * C, (c + 1) * C) for c in range(q_ref.shape[1] // C)]
    log_gammas = [math.log1p(-(2.0 ** (-5.0 - h))) for h in range(RET_HEADS)]

    @pl.when(pl.program_id(1) == 0)
    def _():
        state_ref[...] = jnp.zeros_like(state_ref)
        row = lax.broadcasted_iota(jnp.int32, (C, C), 0)
        col = lax.broadcasted_iota(jnp.int32, (C, C), 1)
        diff = (row - col).astype(F32)
        rowd = lax.broadcasted_iota(jnp.int32, (C, d), 0).astype(F32)
        scale = d ** -0.5
        for h, lg in enumerate(log_gammas):
            inner_ref[h] = jnp.where(row >= col, jnp.exp(jnp.maximum(diff, 0.0) * lg) * scale, 0.0)
            qdec_ref[h] = jnp.exp((rowd + 1.0) * lg)
            kdec_ref[h] = jnp.exp((C - 1.0 - rowd) * lg) * scale

    heads = range(RET_HEADS)
    sls = [slice(h * d, (h + 1) * d) for h in heads]
    keys = [(c, h) for c in range(len(chunks)) for h in heads]
    qb = {(c, h): q_ref[0, chunks[c], sls[h]] for c, h in keys}
    kb = {(c, h): k_ref[0, chunks[c], sls[h]] for c, h in keys}
    vb = {(c, h): v_ref[0, chunks[c], sls[h]] for c, h in keys}
    scores = {q: lax.dot_general(qb[q], kb[q], _NT, preferred_element_type=F32) for q in keys}
    inner = {q: _bdot(scores[q] * inner_ref[q[1]], vb[q]) for q in keys}
    kv = {q: lax.dot_general((kb[q].astype(F32) * kdec_ref[q[1]]).astype(BF16), vb[q], _TN,
                             preferred_element_type=F32) for q in keys}
    state = [state_ref[h] for h in heads]
    cross = {}
    for c, h in keys:
        cross[c, h] = _bdot(qb[c, h], state[h]) * qdec_ref[h]
        state[h] = math.exp(C * log_gammas[h]) * state[h] + kv[c, h]
    for h in heads:
        state_ref[h] = state[h]
    for c, h in keys:
        rc, sl = chunks[c], sls[h]
        y = inner[c, h] + cross[c, h]
        mean = jnp.mean(y, axis=-1, keepdims=True)
        yc = y - mean
        var = jnp.mean(yc * yc, axis=-1, keepdims=True)
        yn = yc * lax.rsqrt(var + RET_GN_EPS) * gng_ref[:, sl] + gnb_ref[:, sl]
        o_ref[0, rc, sl] = (gate_ref[0, rc, sl].astype(F32) * yn).astype(o_ref.dtype)


def _retention(qk, vg, gn_g, gn_b):
    B, S, _ = qk.shape
    W = RET_HEADS * RET_HEAD_DIM
    C = RET_CHUNK
    R = RET_ROWS
    zspec = lambda off: pl.BlockSpec((1, R, W), lambda b, n: (b, n, off))
    pspec = pl.BlockSpec((1, W), lambda b, n: (0, 0))
    return pl.pallas_call(
        _retention_kernel,
        grid=(B, S // R),
        in_specs=[zspec(0), zspec(1), zspec(0), zspec(1), pspec, pspec],
        out_specs=pl.BlockSpec((1, R, W), lambda b, n: (b, n, 0)),
        out_shape=jax.ShapeDtypeStruct((B, S, W), BF16),
        scratch_shapes=[pltpu.VMEM((RET_HEADS, RET_HEAD_DIM, RET_HEAD_DIM), F32),
                        pltpu.VMEM((RET_HEADS, C, C), F32),
                        pltpu.VMEM((RET_HEADS, C, RET_HEAD_DIM), F32),
                        pltpu.VMEM((RET_HEADS, C, RET_HEAD_DIM), F32)],
        compiler_params=_params(("parallel", "arbitrary")),
        name="retention",
    )(qk, qk, vg, vg, gn_g.reshape(1, W), gn_b.reshape(1, W))


def _split2(x):
    hi = x.astype(BF16)
    return hi, (x - hi.astype(F32)).astype(BF16)


def _rwkv_staged_kernel(r_ref, k_ref, v_ref, lo_ref,
                        w0_ref, wup_ref, a0_ref, aup_ref, gup_ref, kk_ref, ka_ref, rk_ref,
                        lng_ref, lnb_ref, o_ref, state_ref):
    R = r_ref.shape[1]
    W = r_ref.shape[2]
    C = RWKV_CHUNK
    hd = RWKV_HEAD_DIM
    P = 2 * hd
    assert P == LANES and 2 * C == P and R % C == 0
    chunks = range(R // C)
    rows = [slice(c * C, (c + 1) * C) for c in chunks]
    pairs = range(W // P)
    sls = [slice(p * P, (p + 1) * P) for p in pairs]

    @pl.when(pl.program_id(1) == 0)
    def _():
        state_ref[...] = jnp.zeros_like(state_ref)

    lo = lo_ref[0]
    tw = jnp.tanh(lo[:, 0:LORA_W]).astype(BF16)
    za = lo[:, LORA_W:LORA_W + LORA_A].astype(BF16)
    sg = jax.nn.sigmoid(lo[:, LORA_W + LORA_A:]).astype(BF16)

    r = r_ref[0]
    k = k_ref[0]
    v = v_ref[0]
    logdec = -math.exp(-0.5) * jax.nn.sigmoid(
        w0_ref[...] + jnp.dot(tw, wup_ref[...], preferred_element_type=F32))
    a = jax.nn.sigmoid(a0_ref[...] + jnp.dot(za, aup_ref[...], preferred_element_type=F32))
    g = jnp.dot(sg, gup_ref[...], preferred_element_type=F32)
    kk_raw = k * kk_ref[...]
    k = k * (1.0 + (a - 1.0) * ka_ref[...])

    tri = jnp.where(lax.broadcasted_iota(jnp.int32, (C, C), 0)
                    >= lax.broadcasted_iota(jnp.int32, (C, C), 1), 1.0, 0.0).astype(BF16)
    ld_hi = logdec.astype(BF16)
    ld_mid, ld_lo = _split2(logdec - ld_hi.astype(F32))
    cum = [jnp.dot(tri, ld_hi[rc], preferred_element_type=F32)
           + jnp.dot(tri, ld_mid[rc], preferred_element_type=F32)
           + jnp.dot(tri, ld_lo[rc], preferred_element_type=F32) for rc in rows]

    row = lax.broadcasted_iota(jnp.int32, (C, P), 0)
    lane = lax.broadcasted_iota(jnp.int32, (C, P), 1)
    head0 = lane < hd
    cj = jnp.where(head0, lane, lane - hd)
    strict = row > cj
    incl = row >= cj
    eye2 = jnp.where(row == cj, 1.0, 0.0)
    r2 = lax.broadcasted_iota(jnp.int32, (P, P), 0)
    l2 = lax.broadcasted_iota(jnp.int32, (P, P), 1)
    bdmask = (r2 < hd) == (l2 < hd)
    zero = jnp.zeros((), BF16)

    def blockdiag(x):
        return jnp.where(bdmask, jnp.concatenate([x, x], axis=0), jnp.zeros((), x.dtype))

    def segsum(x):
        h0 = lax.broadcasted_iota(jnp.int32, x.shape, 1) < hd
        s0 = jnp.sum(jnp.where(h0, x, 0.0), axis=-1, keepdims=True)
        s1 = jnp.sum(jnp.where(h0, 0.0, x), axis=-1, keepdims=True)
        return jnp.where(h0, s0, s1)

    kk_sq = [segsum(jnp.square(kk_raw[:, sl])) for sl in sls]
    bonus = [segsum(r[:, sl] * k[:, sl] * rk_ref[:, sl]) for sl in sls]

    lhs, lv, m_rb, tb, bk_end, dec_chunk, vb = ({} for _ in range(7))

    def independent(group):
        keys = [(c, p) for c in group for p in pairs]
        gram = {}
        for q in keys:
            rc, sl = rows[q[0]], sls[q[1]]
            kk = kk_raw[rc, sl] * lax.rsqrt(jnp.maximum(kk_sq[q[1]][rc], 1e-24))
            cum_p = cum[q[0]][:, sl]
            cum_last = cum_p[C - 1:C, :]
            dec_to_end = jnp.exp(cum_last - cum_p)
            inv_p = jnp.exp(-cum_p)
            kka = kk * a[rc, sl]
            rt = r[rc, sl] * jnp.exp(cum_p)
            kt = (k[rc, sl] * inv_p).astype(BF16)
            at = -kk * jnp.exp(cum_p - logdec[rc, sl])
            bt = (kka * inv_p).astype(BF16)
            bk_end[q] = jnp.concatenate([kka * dec_to_end, k[rc, sl] * dec_to_end],
                                        axis=0).astype(BF16)
            dec_chunk[q] = jnp.exp(cum_last)
            vb[q] = v[rc, sl].astype(BF16)
            lhs[q] = jnp.concatenate([at, rt], axis=0).astype(BF16)
            rhs = jnp.concatenate([jnp.where(head0, bt, zero), jnp.where(head0, zero, bt),
                                   jnp.where(head0, kt, zero), jnp.where(head0, zero, kt)], axis=0)
            gram[q] = lax.dot_general(lhs[q], rhs, _NT, preferred_element_type=F32)
        yield

        l_ab = {q: jnp.where(strict, gram[q][:C, :P], 0.0) for q in keys}
        for q in keys:
            l_akrk = jnp.concatenate([jnp.where(strict, gram[q][:C, P:], 0.0),
                                      jnp.where(incl, gram[q][C:, P:], 0.0)], axis=0)
            lv[q] = jnp.dot(l_akrk.astype(BF16), blockdiag(vb[q]), preferred_element_type=F32)
            m_rb[q] = jnp.where(incl, gram[q][C:, :P], 0.0).astype(BF16)
        yield

        t = {q: eye2 + l_ab[q] for q in keys}
        m = {q: _bdot(l_ab[q], blockdiag(l_ab[q].astype(BF16))) for q in keys}
        yield
        n_sq = int(math.log2(C)) - 1
        for step in range(n_sq):
            for q in keys:
                mb = m[q].astype(BF16)
                if step < n_sq - 1:
                    both = jnp.concatenate([blockdiag(t[q].astype(BF16)), blockdiag(mb)], axis=1)
                    prod = jnp.dot(mb, both, preferred_element_type=F32)
                    t[q] = t[q] + prod[:, :P]
                    m[q] = prod[:, P:]
                else:
                    t[q] = t[q] + jnp.dot(mb, blockdiag(t[q].astype(BF16)),
                                          preferred_element_type=F32)
            yield
        for q in keys:
            tb[q] = t[q].astype(BF16)

    state = [state_ref[p] for p in pairs]

    def dependent(group):
        for c in group:
            rc = rows[c]
            ar = [lax.dot_general(lhs[c, p], state[p].astype(BF16), _NT,
                                  preferred_element_type=F32) for p in pairs]
            yield
            ub = [jnp.dot(tb[c, p], blockdiag((ar[p][:C] + lv[c, p][:C]).astype(BF16)),
                          preferred_element_type=F32).astype(BF16) for p in pairs]
            yield
            y = [ar[p][C:] + lv[c, p][C:]
                 + jnp.dot(m_rb[c, p], blockdiag(ub[p]), preferred_element_type=F32) for p in pairs]
            for p in pairs:
                upd = lax.dot_general(jnp.concatenate([ub[p], vb[c, p]], axis=0), bk_end[c, p],
                                      _TN, preferred_element_type=F32)
                state[p] = state[p] * dec_chunk[c, p] + jnp.where(bdmask, upd, 0.0)
            yield
            yc = [y[p] - segsum(y[p]) * (1.0 / hd) for p in pairs]
            var = [segsum(jnp.square(yc[p])) * (1.0 / hd) for p in pairs]
            for p, sl in zip(pairs, sls):
                yn = yc[p] * lax.rsqrt(var[p] + RWKV_LN_EPS) * lng_ref[:, sl] + lnb_ref[:, sl]
                o_ref[0, rc, sl] = ((yn + bonus[p][rc] * v[rc, sl]) * g[rc, sl]).astype(o_ref.dtype)
            yield

    def run(*stages):
        stages = list(stages)
        while stages:
            for stage in tuple(stages):
                if next(stage, stages) is stages:
                    stages.remove(stage)

    groups = [list(chunks[i:i + RWKV_GROUP]) for i in range(0, len(chunks), RWKV_GROUP)]
    run(independent(groups[0]))
    for gi, group in enumerate(groups):
        ahead = [independent(groups[gi + 1])] if gi + 1 < len(groups) else []
        run(dependent(group), *ahead)
    for p in pairs:
        state_ref[p] = state[p]


def _rwkv(z, w0, w_up, a0, a_up, g_up, k_k, k_a, r_k, ln_g, ln_b):
    B, S, _ = z.shape
    W = w0.shape[-1]
    C = RWKV_ROWS
    LO = LORA_W + LORA_A + LORA_G
    P = 2 * RWKV_HEAD_DIM
    lo0 = 3 * W // LO
    zspec = lambda off: pl.BlockSpec((1, C, W), lambda b, n: (b, n, off))
    full = lambda a: pl.BlockSpec(a.shape, lambda b, n: (0,) * a.ndim)
    row = lambda a: a.reshape(1, -1)
    params = [row(w0), w_up.astype(BF16), row(a0), a_up.astype(BF16), g_up.astype(BF16),
              row(k_k), row(k_a), row(r_k), row(ln_g), row(ln_b)]
    return pl.pallas_call(
        _rwkv_staged_kernel,
        grid=(B, S // C),
        in_specs=[zspec(0), zspec(1), zspec(2),
                  pl.BlockSpec((1, C, LO), lambda b, n: (b, n, lo0))] + [full(a) for a in params],
        out_specs=pl.BlockSpec((1, C, W), lambda b, n: (b, n, 0)),
        out_shape=jax.ShapeDtypeStruct((B, S, W), BF16),
        scratch_shapes=[pltpu.VMEM((W // P, P, P), F32)],
        compiler_params=_params(("parallel", "arbitrary")),
        name="rwkv7",
    )(z, z, z, z, *params)


def kernel(x, positions, norm1_g, w_in, ret_gn_g, ret_gn_b, rwkv_mu, rwkv_w0, rwkv_w_up, rwkv_a0, rwkv_a_up, rwkv_g_up, rwkv_k_k, rwkv_k_a, rwkv_r_k, rwkv_ln_g, rwkv_ln_b, w_branch_a, w_branch_b, w_out, norm2_g, mlp_up, mlp_down, final_g):
    B, S, D = x.shape
    M = B * S
    depth = w_in.shape[0]
    ret_w = ret_gn_g.shape[-1]
    rwkv_w = rwkv_w0.shape[-1]
    ret_cols = 4 * ret_w
    rwkv_cols = rwkv_mu.shape[-1]
    assert ret_w == RET_HEADS * RET_HEAD_DIM and rwkv_w % (2 * RWKV_HEAD_DIM) == 0
    assert rwkv_cols == 3 * rwkv_w + LORA_W + LORA_A + LORA_G
    assert S % PROJ_ROWS == 0 and S % RET_ROWS == 0 and S % RWKV_ROWS == 0
    assert M % ROPE_ROWS == 0 and M % MLP_ROWS == 0 and mlp_up.shape[-1] % MLP_HIDDEN == 0
    cos2, sin2 = _rope_tables(positions, tm=ROPE_ROWS)

    in_splits = (0, ret_cols, ret_cols + rwkv_cols, w_in.shape[-1])
    w_ret = w_in[0][:, :ret_cols].astype(BF16)
    full = lambda t, layer: (t, layer, (0, t.shape[-1]))
    mix_weights = (w_branch_a, w_branch_b, w_out)
    mlp_weights = (mlp_up, mlp_down)

    xf = x.reshape(M, D)
    for l in range(depth):
        more = l + 1 < depth
        next_mix = ([(w_in, l + 1, in_splits)] + [full(t, l + 1) for t in mix_weights]) if more else []
        next_mlp = [full(t, l + 1) for t in mlp_weights] if more else []
        z_qk, xn, cast_qk = _norm_qk_proj(
            xf, norm1_g[l], w_ret, cos2, sin2, n_cols=2 * ret_w,
            casts=[(w_in, 0, in_splits[1:])] if l == 0 else (), tm=QK_PROJ_ROWS)
        if l == 0:
            w_rwkv, w_gate = cast_qk
        z_vg, cast_vg = _matmul(xn, w_ret, name="in_proj_vg", epilogue="silu_upper_half",
                                col0=2 * ret_w, casts=next_mix if l == 0 else (),
                                out_dtype=BF16, tm=PROJ_ROWS, tn=2 * ret_w)
        gate_casts = [full(t, 0) for t in mix_weights + mlp_weights] if l == 0 else next_mix
        z_gate, cast_gate = _matmul(xn, w_gate, name="in_proj_gate", epilogue="sigmoid",
                                    casts=gate_casts, out_dtype=BF16, tm=PROJ_ROWS, tn=PROJ_COLS)
        if l == 0:
            w_a, w_b, w_o, w_up, w_down = cast_gate
        z_rwkv, cast_rwkv = _matmul_shift(xn, w_rwkv, rwkv_mu[l], seq_len=S, casts=next_mlp,
                                          name="in_proj_rwkv", tm=PROJ_ROWS, tn=rwkv_cols // 2)
        y_ret = _retention(z_qk.reshape(B, S, -1), z_vg.reshape(B, S, -1),
                           ret_gn_g[l], ret_gn_b[l])
        y_rwkv = _rwkv(z_rwkv.reshape(B, S, -1), rwkv_w0[l], rwkv_w_up[l],
                       rwkv_a0[l], rwkv_a_up[l], rwkv_g_up[l], rwkv_k_k[l], rwkv_k_a[l],
                       rwkv_r_k[l], rwkv_ln_g[l], rwkv_ln_b[l])
        xf = _merge_out(y_ret.reshape(M, ret_w), y_rwkv.reshape(M, rwkv_w),
                        w_a, w_b, z_gate, w_o, xf, tm=MERGE_ROWS)
        xf = _mlp(xf, norm2_g[l], w_up, w_down, None if more else final_g,
                  tm=MLP_ROWS, tf=MLP_HIDDEN)
        if more:
            w_ret, w_rwkv, w_gate, w_a, w_b, w_o = cast_vg if l == 0 else cast_gate
            w_up, w_down = cast_rwkv
    return xf.reshape(B, S, D)
```

```python
import functools
import math

import jax
import jax.numpy as jnp
from jax import lax
from jax.experimental import pallas as pl
from jax.experimental.pallas import tpu as pltpu

F32 = jnp.float32
BF16 = jnp.bfloat16

RET_HEADS = 8
RET_HEAD_DIM = 128
RET_GN_EPS = 1e-5
ROPE_BASE = 10000.0
RWKV_HEAD_DIM = 64
RWKV_LN_EPS = 64e-5
NORM_EPS = 1e-6
LORA_W = 64
LORA_A = 64
LORA_G = 128

LANES = 128
BF16_SUBLANES = 16
V7X_VMEM_LIMIT_BYTES = 56 * 1024 * 1024

ROPE_ROWS = 2048
QK_PROJ_ROWS = 512
PROJ_ROWS = 1024
PROJ_COLS = 2048
RWKV_PROJ_ROWS = 512
MERGE_ROWS = 512
MLP_ROWS = 1024
MLP_HIDDEN = 512
RET_CHUNK = 128
RET_ROWS = 512
RWKV_CHUNK = 64
RWKV_ROWS = 512
RWKV_GROUP = 2

_NT = (((1,), (1,)), ((), ()))
_TN = (((0,), (0,)), ((), ()))


def _params(semantics):
    return pltpu.CompilerParams(dimension_semantics=semantics,
                                vmem_limit_bytes=V7X_VMEM_LIMIT_BYTES)


def _bdot(a, b):
    return jnp.dot(a.astype(BF16), b.astype(BF16), preferred_element_type=F32)


def _side_cast_specs(casts, n_steps, step_of):
    in_specs, args, out_specs, out_shape, all_splits = [], [], [], [], []
    for w, layer, splits in casts:
        _, R, N = w.shape
        rows = max(BF16_SUBLANES, R // n_steps)
        repeat = rows * n_steps // R
        in_specs.append(pl.BlockSpec(
            (None, rows, N),
            lambda *g, layer=layer, repeat=repeat: (layer, step_of(*g) // repeat, 0)))
        args.append(w)
        for lo, hi in zip(splits[:-1], splits[1:]):
            out_specs.append(pl.BlockSpec(
                (rows, hi - lo), lambda *g, repeat=repeat: (step_of(*g) // repeat, 0)))
            out_shape.append(jax.ShapeDtypeStruct((R, hi - lo), BF16))
        all_splits.append(tuple(splits))
    return in_specs, args, out_specs, out_shape, tuple(all_splits)


def _side_cast(cast_in, cast_out, cast_splits):
    k = 0
    for w_ref, splits in zip(cast_in, cast_splits):
        for lo, hi in zip(splits[:-1], splits[1:]):
            cast_out[k][...] = w_ref[:, lo:hi].astype(BF16)
            k += 1


def _rotary(t, cos2, sin2):
    return t * cos2 + pltpu.roll(t, RET_HEAD_DIM // 2, axis=1) * sin2


def _norm_qk_kernel(*refs, cast_splits):
    n_cast_in = len(cast_splits)
    x_ref, g_ref, w_ref, cos_ref, sin_ref = refs[:5]
    cast_in = refs[5:5 + n_cast_in]
    o_ref, xn_ref = refs[5 + n_cast_in:7 + n_cast_in]
    cast_out = refs[7 + n_cast_in:]
    _side_cast(cast_in, cast_out, cast_splits)
    x = x_ref[...]
    ms = jnp.mean(x * x, axis=-1, keepdims=True)
    xn = (x * lax.rsqrt(ms + NORM_EPS) * g_ref[...]).astype(BF16)
    xn_ref[...] = xn
    acc = jnp.dot(xn, w_ref[...], preferred_element_type=F32)
    cos2 = cos_ref[...]
    sin2 = sin_ref[...]
    for h in range(acc.shape[1] // RET_HEAD_DIM):
        sl = slice(h * RET_HEAD_DIM, (h + 1) * RET_HEAD_DIM)
        o_ref[:, sl] = _rotary(acc[:, sl], cos2, sin2).astype(o_ref.dtype)


def _norm_qk_proj(x, g, w, cos2, sin2, *, n_cols, casts=(), tm):
    M, K = x.shape
    d = RET_HEAD_DIM
    c_in, c_args, c_out, c_shape, cast_splits = _side_cast_specs(casts, M // tm, lambda i: i)
    outs = pl.pallas_call(
        functools.partial(_norm_qk_kernel, cast_splits=cast_splits),
        grid=(M // tm,),
        in_specs=[pl.BlockSpec((tm, K), lambda i: (i, 0)),
                  pl.BlockSpec((1, K), lambda i: (0, 0)),
                  pl.BlockSpec((K, n_cols), lambda i: (0, 0)),
                  pl.BlockSpec((tm, d), lambda i: (i, 0)),
                  pl.BlockSpec((tm, d), lambda i: (i, 0))] + c_in,
        out_specs=[pl.BlockSpec((tm, n_cols), lambda i: (i, 0)),
                   pl.BlockSpec((tm, K), lambda i: (i, 0))] + c_out,
        out_shape=[jax.ShapeDtypeStruct((M, n_cols), BF16),
                   jax.ShapeDtypeStruct((M, K), BF16)] + c_shape,
        compiler_params=_params(("parallel",)),
        name="norm_in_proj_qk",
    )(x, g.reshape(1, K), w, cos2, sin2, *c_args)
    return outs[0], outs[1], outs[2:]


def _mm_kernel(*refs, epilogue, cast_splits):
    n_cast_in = len(cast_splits)
    x_ref, w_ref = refs[:2]
    cast_in = refs[2:2 + n_cast_in]
    o_ref = refs[2 + n_cast_in]
    cast_out = refs[3 + n_cast_in:]
    _side_cast(cast_in, cast_out, cast_splits)
    acc = jnp.dot(x_ref[...], w_ref[...], preferred_element_type=F32)
    if epilogue == "sigmoid":
        acc = jax.nn.sigmoid(acc)
        o_ref[...] = acc.astype(o_ref.dtype)
    elif epilogue == "silu_upper_half":
        half = acc.shape[1] // 2
        gate = acc[:, half:]
        o_ref[:, :half] = acc[:, :half].astype(o_ref.dtype)
        o_ref[:, half:] = (gate * jax.nn.sigmoid(gate)).astype(o_ref.dtype)
    else:
        o_ref[...] = acc.astype(o_ref.dtype)


def _matmul(x, w, *, name, epilogue="none", col0=0, n_cols=None, casts=(), out_dtype, tm, tn):
    M, K = x.shape
    n_cols = w.shape[1] - col0 if n_cols is None else n_cols
    c0 = col0 // tn
    nj, ni = n_cols // tn, M // tm
    c_in, c_args, c_out, c_shape, cast_splits = _side_cast_specs(
        casts, ni * nj, lambda j, i: j * ni + i)
    outs = pl.pallas_call(
        functools.partial(_mm_kernel, epilogue=epilogue, cast_splits=cast_splits),
        grid=(nj, ni),
        in_specs=[pl.BlockSpec((tm, K), lambda j, i: (i, 0)),
                  pl.BlockSpec((K, tn), lambda j, i: (0, c0 + j))] + c_in,
        out_specs=[pl.BlockSpec((tm, tn), lambda j, i: (i, j))] + c_out,
        out_shape=[jax.ShapeDtypeStruct((M, n_cols), out_dtype)] + c_shape,
        compiler_params=_params(("parallel", "arbitrary")),
        name=name,
    )(x, w, *c_args)
    return outs[0], outs[1:]


def _mm_shift_kernel(*refs, tiles_per_seq, cast_splits):
    n_cast_in = len(cast_splits)
    x_ref, w_ref, mu_ref = refs[:3]
    cast_in = refs[3:3 + n_cast_in]
    o_ref = refs[3 + n_cast_in]
    cast_out = refs[4 + n_cast_in:-1]
    carry_ref = refs[-1]
    i = pl.program_id(1)

    @pl.when(i % tiles_per_seq == 0)
    def _():
        carry_ref[...] = jnp.zeros_like(carry_ref)

    _side_cast(cast_in, cast_out, cast_splits)
    z = jnp.dot(x_ref[...], w_ref[...], preferred_element_type=F32)
    n = z.shape[0]
    first = lax.broadcasted_iota(jnp.int32, z.shape, 0) == 0
    z_prev = jnp.where(first, carry_ref[...], pltpu.roll(z, 1, axis=0))
    carry_ref[...] = z[n - 1:n, :]
    o_ref[...] = (z + mu_ref[...] * (z_prev - z)).astype(o_ref.dtype)


def _matmul_shift(x, w, mu, *, seq_len, casts=(), name, tm, tn):
    M, K = x.shape
    N = w.shape[1]
    assert seq_len % tm == 0
    nj, ni = N // tn, M // tm
    c_in, c_args, c_out, c_shape, cast_splits = _side_cast_specs(
        casts, ni * nj, lambda j, i: j * ni + i)
    w_mode = dict(pipeline_mode=pl.Buffered(1)) if nj == 1 else {}
    outs = pl.pallas_call(
        functools.partial(_mm_shift_kernel, tiles_per_seq=seq_len // tm, cast_splits=cast_splits),
        grid=(nj, ni),
        in_specs=[pl.BlockSpec((tm, K), lambda j, i: (i, 0)),
                  pl.BlockSpec((K, tn), lambda j, i: (0, j), **w_mode),
                  pl.BlockSpec((1, tn), lambda j, i: (0, j))] + c_in,
        out_specs=[pl.BlockSpec((tm, tn), lambda j, i: (i, j))] + c_out,
        out_shape=[jax.ShapeDtypeStruct((M, N), F32)] + c_shape,
        scratch_shapes=[pltpu.VMEM((1, tn), F32)],
        compiler_params=_params(("arbitrary", "arbitrary")),
        name=name,
    )(x, w, mu.reshape(1, N), *c_args)
    return outs[0], outs[1:]


def _mlp_kernel(*refs, final_norm):
    x_ref, g_ref, wu_ref, wd_ref = refs[:4]
    fg_ref = refs[4] if final_norm else None
    o_ref, xn_ref = refs[-2:]
    j = pl.program_id(1)

    @pl.when(j == 0)
    def _():
        x = x_ref[...]
        ms = jnp.mean(x * x, axis=-1, keepdims=True)
        xn_ref[...] = (x * lax.rsqrt(ms + NORM_EPS) * g_ref[...]).astype(BF16)
        o_ref[...] = x

    h = jnp.dot(xn_ref[...], wu_ref[...], preferred_element_type=F32)
    h = jnp.square(jnp.maximum(h, 0.0)).astype(BF16)
    o_ref[...] += jnp.dot(h, wd_ref[...], preferred_element_type=F32)

    if final_norm:
        @pl.when(j == pl.num_programs(1) - 1)
        def _():
            y = o_ref[...]
            ms = jnp.mean(y * y, axis=-1, keepdims=True)
            o_ref[...] = y * lax.rsqrt(ms + NORM_EPS) * fg_ref[...]


def _mlp(x, g, w_up, w_down, final_g=None, *, tm, tf):
    M, D = x.shape
    F = w_up.shape[1]
    final_norm = final_g is not None
    vec = pl.BlockSpec((1, D), lambda i, j: (0, 0))
    in_specs = [pl.BlockSpec((tm, D), lambda i, j: (i, 0)), vec,
                pl.BlockSpec((D, tf), lambda i, j: (0, j)),
                pl.BlockSpec((tf, D), lambda i, j: (j, 0))]
    args = [x, g.reshape(1, D), w_up, w_down]
    if final_norm:
        in_specs.append(vec)
        args.append(final_g.reshape(1, D))
    return pl.pallas_call(
        functools.partial(_mlp_kernel, final_norm=final_norm),
        grid=(M // tm, F // tf),
        in_specs=in_specs,
        out_specs=pl.BlockSpec((tm, D), lambda i, j: (i, 0)),
        out_shape=jax.ShapeDtypeStruct((M, D), F32),
        scratch_shapes=[pltpu.VMEM((tm, D), BF16)],
        compiler_params=_params(("parallel", "arbitrary")),
        name="mlp_final" if final_norm else "mlp",
    )(*args)


def _merge_out_kernel(ya_ref, yb_ref, wa_ref, wb_ref, ga_ref, gb_ref, wo_ref, x_ref, o_ref):
    D = o_ref.shape[1]
    n_split = 2
    width = D // n_split
    ya = ya_ref[...]
    yb = yb_ref[...]
    acc = x_ref[...]
    for c in range(n_split):
        cs = slice(c * width, (c + 1) * width)
        a = jnp.dot(ya, wa_ref[:, cs], preferred_element_type=F32)
        b = jnp.dot(yb, wb_ref[:, cs], preferred_element_type=F32)
        m = (ga_ref[:, cs].astype(F32) * a + gb_ref[:, cs].astype(F32) * b).astype(BF16)
        acc = acc + jnp.dot(m, wo_ref[cs, :], preferred_element_type=F32)
    o_ref[...] = acc


def _merge_out(ya, yb, wa, wb, gates, wo, x, *, tm):
    M, K = ya.shape
    D = wo.shape[1]
    resident = lambda a: pl.BlockSpec(a.shape, lambda i: (0, 0), pipeline_mode=pl.Buffered(1))
    return pl.pallas_call(
        _merge_out_kernel,
        grid=(M // tm,),
        in_specs=[
            pl.BlockSpec((tm, K), lambda i: (i, 0)),
            pl.BlockSpec((tm, K), lambda i: (i, 0)),
            resident(wa), resident(wb),
            pl.BlockSpec((tm, D), lambda i: (i, 0)),
            pl.BlockSpec((tm, D), lambda i: (i, 1)),
            resident(wo),
            pl.BlockSpec((tm, D), lambda i: (i, 0)),
        ],
        out_specs=pl.BlockSpec((tm, D), lambda i: (i, 0)),
        out_shape=jax.ShapeDtypeStruct((M, D), F32),
        compiler_params=_params(("parallel",)),
        name="merge_out_proj",
    )(ya, yb, wa, wb, gates, gates, wo, x)


def _rope_kernel(pos_ref, cos_ref, sin_ref):
    half = RET_HEAD_DIM // 2
    pos = pos_ref[...].astype(F32)
    lane = lax.broadcasted_iota(jnp.int32, (1, RET_HEAD_DIM), 1)
    j = jnp.where(lane < half, lane, lane - half).astype(F32)
    inv_freq = jnp.exp(j * (-math.log(ROPE_BASE) / half))
    ang = pos * inv_freq
    cos_ref[...] = jnp.cos(ang)
    sin_ref[...] = jnp.where(lane < half, -1.0, 1.0) * jnp.sin(ang)


def _rope_tables(positions, *, tm):
    M = positions.size
    pos = positions.reshape(M, 1)
    return pl.pallas_call(
        _rope_kernel,
        grid=(M // tm,),
        in_specs=[pl.BlockSpec((tm, 1), lambda i: (i, 0))],
        out_specs=[pl.BlockSpec((tm, RET_HEAD_DIM), lambda i: (i, 0))] * 2,
        out_shape=[jax.ShapeDtypeStruct((M, RET_HEAD_DIM), F32)] * 2,
        compiler_params=_params(("parallel",)),
        name="rope_tables",
    )(pos)


def _retention_kernel(q_ref, k_ref, v_ref, gate_ref, gng_ref, gnb_ref,
                      o_ref, state_ref, inner_ref, qdec_ref, kdec_ref):
    C = RET_CHUNK
    d = RET_HEAD_DIM
    chunks = [slice(c * C, (c + 1) * C) for c in range(q_ref.shape[1] // C)]
    log_gammas = [math.log1p(-(2.0 ** (-5.0 - h))) for h in range(RET_HEADS)]

    @pl.when(pl.program_id(1) == 0)
    def _():
        state_ref[...] = jnp.zeros_like(state_ref)
        row = lax.broadcasted_iota(jnp.int32, (C, C), 0)
        col = lax.broadcasted_iota(jnp.int32, (C, C), 1)
        diff = (row - col).astype(F32)
        rowd = lax.broadcasted_iota(jnp.int32, (C, d), 0).astype(F32)
        scale = d ** -0.5
        for h, lg in enumerate(log_gammas):
            inner_ref[h] = jnp.where(row >= col, jnp.exp(jnp.maximum(diff, 0.0) * lg) * scale, 0.0)
            qdec_ref[h] = jnp.exp((rowd + 1.0) * lg)
            kdec_ref[h] = jnp.exp((C - 1.0 - rowd) * lg) * scale

    heads = range(RET_HEADS)
    sls = [slice(h * d, (h + 1) * d) for h in heads]
    keys = [(c, h) for c in range(len(chunks)) for h in heads]
    qb = {(c, h): q_ref[0, chunks[c], sls[h]] for c, h in keys}
    kb = {(c, h): k_ref[0, chunks[c], sls[h]] for c, h in keys}
    vb = {(c, h): v_ref[0, chunks[c], sls[h]] for c, h in keys}
    scores = {q: lax.dot_general(qb[q], kb[q], _NT, preferred_element_type=F32) for q in keys}
    inner = {q: _bdot(scores[q] * inner_ref[q[1]], vb[q]) for q in keys}
    kv = {q: lax.dot_general((kb[q].astype(F32) * kdec_ref[q[1]]).astype(BF16), vb[q], _TN,
                             preferred_element_type=F32) for q in keys}
    state = [state_ref[h] for h in heads]
    cross = {}
    for c, h in keys:
        cross[c, h] = _bdot(qb[c, h], state[h]) * qdec_ref[h]
        state[h] = math.exp(C * log_gammas[h]) * state[h] + kv[c, h]
    for h in heads:
        state_ref[h] = state[h]
    for c, h in keys:
        rc, sl = chunks[c], sls[h]
        y = inner[c, h] + cross[c, h]
        mean = jnp.mean(y, axis=-1, keepdims=True)
        yc = y - mean
        var = jnp.mean(yc * yc, axis=-1, keepdims=True)
        yn = yc * lax.rsqrt(var + RET_GN_EPS) * gng_ref[:, sl] + gnb_ref[:, sl]
        o_ref[0, rc, sl] = (gate_ref[0, rc, sl].astype(F32) * yn).astype(o_ref.dtype)


def _retention(qk, vg, gn_g, gn_b):
    B, S, _ = qk.shape
    W = RET_HEADS * RET_HEAD_DIM
    C = RET_CHUNK
    R = RET_ROWS
    zspec = lambda off: pl.BlockSpec((1, R, W), lambda b, n: (b, n, off))
    pspec = pl.BlockSpec((1, W), lambda b, n: (0, 0))
    return pl.pallas_call(
        _retention_kernel,
        grid=(B, S // R),
        in_specs=[zspec(0), zspec(1), zspec(0), zspec(1), pspec, pspec],
        out_specs=pl.BlockSpec((1, R, W), lambda b, n: (b, n, 0)),
        out_shape=jax.ShapeDtypeStruct((B, S, W), BF16),
        scratch_shapes=[pltpu.VMEM((RET_HEADS, RET_HEAD_DIM, RET_HEAD_DIM), F32),
                        pltpu.VMEM((RET_HEADS, C, C), F32),
                        pltpu.VMEM((RET_HEADS, C, RET_HEAD_DIM), F32),
                        pltpu.VMEM((RET_HEADS, C, RET_HEAD_DIM), F32)],
        compiler_params=_params(("parallel", "arbitrary")),
        name="retention",
    )(qk, qk, vg, vg, gn_g.reshape(1, W), gn_b.reshape(1, W))


def _split2(x):
    hi = x.astype(BF16)
    return hi, (x - hi.astype(F32)).astype(BF16)


def _rwkv_staged_kernel(r_ref, k_ref, v_ref, lo_ref,
                        w0_ref, wup_ref, a0_ref, aup_ref, gup_ref, kk_ref, ka_ref, rk_ref,
                        lng_ref, lnb_ref, o_ref, state_ref):
    R = r_ref.shape[1]
    W = r_ref.shape[2]
    C = RWKV_CHUNK
    hd = RWKV_HEAD_DIM
    P = 2 * hd
    assert P == LANES and 2 * C == P and R % C == 0
    chunks = range(R // C)
    rows = [slice(c * C, (c + 1) * C) for c in chunks]
    pairs = range(W // P)
    sls = [slice(p * P, (p + 1) * P) for p in pairs]

    @pl.when(pl.program_id(1) == 0)
    def _():
        state_ref[...] = jnp.zeros_like(state_ref)

    lo = lo_ref[0]
    tw = jnp.tanh(lo[:, 0:LORA_W]).astype(BF16)
    za = lo[:, LORA_W:LORA_W + LORA_A].astype(BF16)
    sg = jax.nn.sigmoid(lo[:, LORA_W + LORA_A:]).astype(BF16)

    r = r_ref[0]
    k = k_ref[0]
    v = v_ref[0]
    logdec = -math.exp(-0.5) * jax.nn.sigmoid(
        w0_ref[...] + jnp.dot(tw, wup_ref[...], preferred_element_type=F32))
    a = jax.nn.sigmoid(a0_ref[...] + jnp.dot(za, aup_ref[...], preferred_element_type=F32))
    g = jnp.dot(sg, gup_ref[...], preferred_element_type=F32)
    kk_raw = k * kk_ref[...]
    k = k * (1.0 + (a - 1.0) * ka_ref[...])

    tri = jnp.where(lax.broadcasted_iota(jnp.int32, (C, C), 0)
                    >= lax.broadcasted_iota(jnp.int32, (C, C), 1), 1.0, 0.0).astype(BF16)
    ld_hi = logdec.astype(BF16)
    ld_mid, ld_lo = _split2(logdec - ld_hi.astype(F32))
    cum = [jnp.dot(tri, ld_hi[rc], preferred_element_type=F32)
           + jnp.dot(tri, ld_mid[rc], preferred_element_type=F32)
           + jnp.dot(tri, ld_lo[rc], preferred_element_type=F32) for rc in rows]

    row = lax.broadcasted_iota(jnp.int32, (C, P), 0)
    lane = lax.broadcasted_iota(jnp.int32, (C, P), 1)
    head0 = lane < hd
    cj = jnp.where(head0, lane, lane - hd)
    strict = row > cj
    incl = row >= cj
    eye2 = jnp.where(row == cj, 1.0, 0.0)
    r2 = lax.broadcasted_iota(jnp.int32, (P, P), 0)
    l2 = lax.broadcasted_iota(jnp.int32, (P, P), 1)
    bdmask = (r2 < hd) == (l2 < hd)
    zero = jnp.zeros((), BF16)

    def blockdiag(x):
        return jnp.where(bdmask, jnp.concatenate([x, x], axis=0), jnp.zeros((), x.dtype))

    def segsum(x):
        h0 = lax.broadcasted_iota(jnp.int32, x.shape, 1) < hd
        s0 = jnp.sum(jnp.where(h0, x, 0.0), axis=-1, keepdims=True)
        s1 = jnp.sum(jnp.where(h0, 0.0, x), axis=-1, keepdims=True)
        return jnp.where(h0, s0, s1)

    kk_sq = [segsum(jnp.square(kk_raw[:, sl])) for sl in sls]
    bonus = [segsum(r[:, sl] * k[:, sl] * rk_ref[:, sl]) for sl in sls]

    lhs, lv, m_rb, tb, bk_end, dec_chunk, vb = ({} for _ in range(7))

    def independent(group):
        keys = [(c, p) for c in group for p in pairs]
        gram = {}
        for q in keys:
            rc, sl = rows[q[0]], sls[q[1]]
            kk = kk_raw[rc, sl] * lax.rsqrt(jnp.maximum(kk_sq[q[1]][rc], 1e-24))
            cum_p = cum[q[0]][:, sl]
            cum_last = cum_p[C - 1:C, :]
            dec_to_end = jnp.exp(cum_last - cum_p)
            inv_p = jnp.exp(-cum_p)
            kka = kk * a[rc, sl]
            rt = r[rc, sl] * jnp.exp(cum_p)
            kt = (k[rc, sl] * inv_p).astype(BF16)
            at = -kk * jnp.exp(cum_p - logdec[rc, sl])
            bt = (kka * inv_p).astype(BF16)
            bk_end[q] = jnp.concatenate([kka * dec_to_end, k[rc, sl] * dec_to_end],
                                        axis=0).astype(BF16)
            dec_chunk[q] = jnp.exp(cum_last)
            vb[q] = v[rc, sl].astype(BF16)
            lhs[q] = jnp.concatenate([at, rt], axis=0).astype(BF16)
            rhs = jnp.concatenate([jnp.where(head0, bt, zero), jnp.where(head0, zero, bt),
                                   jnp.where(head0, kt, zero), jnp.where(head0, zero, kt)], axis=0)
            gram[q] = lax.dot_general(lhs[q], rhs, _NT, preferred_element_type=F32)
        yield

        l_ab = {q: jnp.where(strict, gram[q][:C, :P], 0.0) for q in keys}
        for q in keys:
            l_akrk = jnp.concatenate([jnp.where(strict, gram[q][:C, P:], 0.0),
                                      jnp.where(incl, gram[q][C:, P:], 0.0)], axis=0)
            lv[q] = jnp.dot(l_akrk.astype(BF16), blockdiag(vb[q]), preferred_element_type=F32)
            m_rb[q] = jnp.where(incl, gram[q][C:, :P], 0.0).astype(BF16)
        yield

        t = {q: eye2 + l_ab[q] for q in keys}
        m = {q: _bdot(l_ab[q], blockdiag(l_ab[q].astype(BF16))) for q in keys}
        yield
        n_sq = int(math.log2(C)) - 1
        for step in range(n_sq):
            for q in keys:
                mb = m[q].astype(BF16)
                if step < n_sq - 1:
                    both = jnp.concatenate([blockdiag(t[q].astype(BF16)), blockdiag(mb)], axis=1)
                    prod = jnp.dot(mb, both, preferred_element_type=F32)
                    t[q] = t[q] + prod[:, :P]
                    m[q] = prod[:, P:]
                else:
                    t[q] = t[q] + jnp.dot(mb, blockdiag(t[q].astype(BF16)),
                                          preferred_element_type=F32)
            yield
        for q in keys:
            tb[q] = t[q].astype(BF16)

    state = [state_ref[p] for p in pairs]

    def dependent(group):
        for c in group:
            rc = rows[c]
            ar = [lax.dot_general(lhs[c, p], state[p].astype(BF16), _NT,
                                  preferred_element_type=F32) for p in pairs]
            yield
            ub = [jnp.dot(tb[c, p], blockdiag((ar[p][:C] + lv[c, p][:C]).astype(BF16)),
                          preferred_element_type=F32).astype(BF16) for p in pairs]
            yield
            y = [ar[p][C:] + lv[c, p][C:]
                 + jnp.dot(m_rb[c, p], blockdiag(ub[p]), preferred_element_type=F32) for p in pairs]
            for p in pairs:
                upd = lax.dot_general(jnp.concatenate([ub[p], vb[c, p]], axis=0), bk_end[c, p],
                                      _TN, preferred_element_type=F32)
                state[p] = state[p] * dec_chunk[c, p] + jnp.where(bdmask, upd, 0.0)
            yield
            yc = [y[p] - segsum(y[p]) * (1.0 / hd) for p in pairs]
            var = [segsum(jnp.square(yc[p])) * (1.0 / hd) for p in pairs]
            for p, sl in zip(pairs, sls):
                yn = yc[p] * lax.rsqrt(var[p] + RWKV_LN_EPS) * lng_ref[:, sl] + lnb_ref[:, sl]
                o_ref[0, rc, sl] = ((yn + bonus[p][rc] * v[rc, sl]) * g[rc, sl]).astype(o_ref.dtype)
            yield

    def run(*stages):
        stages = list(stages)
        while stages:
            for stage in tuple(stages):
                if next(stage, stages) is stages:
                    stages.remove(stage)

    groups = [list(chunks[i:i + RWKV_GROUP]) for i in range(0, len(chunks), RWKV_GROUP)]
    run(independent(groups[0]))
    for gi, group in enumerate(groups):
        ahead = [independent(groups[gi + 1])] if gi + 1 < len(groups) else []
        run(dependent(group), *ahead)
    for p in pairs:
        state_ref[p] = state[p]


def _rwkv(z, w0, w_up, a0, a_up, g_up, k_k, k_a, r_k, ln_g, ln_b):
    B, S, _ = z.shape
    W = w0.shape[-1]
    C = RWKV_ROWS
    LO = LORA_W + LORA_A + LORA_G
    P = 2 * RWKV_HEAD_DIM
    lo0 = 3 * W // LO
    zspec = lambda off: pl.BlockSpec((1, C, W), lambda b, n: (b, n, off))
    full = lambda a: pl.BlockSpec(a.shape, lambda b, n: (0,) * a.ndim)
    row = lambda a: a.reshape(1, -1)
    params = [row(w0), w_up.astype(BF16), row(a0), a_up.astype(BF16), g_up.astype(BF16),
              row(k_k), row(k_a), row(r_k), row(ln_g), row(ln_b)]
    return pl.pallas_call(
        _rwkv_staged_kernel,
        grid=(B, S // C),
        in_specs=[zspec(0), zspec(1), zspec(2),
                  pl.BlockSpec((1, C, LO), lambda b, n: (b, n, lo0))] + [full(a) for a in params],
        out_specs=pl.BlockSpec((1, C, W), lambda b, n: (b, n, 0)),
        out_shape=jax.ShapeDtypeStruct((B, S, W), BF16),
        scratch_shapes=[pltpu.VMEM((W // P, P, P), F32)],
        compiler_params=_params(("parallel", "arbitrary")),
        name="rwkv7",
    )(z, z, z, z, *params)


def kernel(x, positions, norm1_g, w_in, ret_gn_g, ret_gn_b, rwkv_mu, rwkv_w0, rwkv_w_up, rwkv_a0, rwkv_a_up, rwkv_g_up, rwkv_k_k, rwkv_k_a, rwkv_r_k, rwkv_ln_g, rwkv_ln_b, w_branch_a, w_branch_b, w_out, norm2_g, mlp_up, mlp_down, final_g):
    B, S, D = x.shape
    M = B * S
    depth = w_in.shape[0]
    ret_w = ret_gn_g.shape[-1]
    rwkv_w = rwkv_w0.shape[-1]
    ret_cols = 4 * ret_w
    rwkv_cols = rwkv_mu.shape[-1]
    assert ret_w == RET_HEADS * RET_HEAD_DIM and rwkv_w % (2 * RWKV_HEAD_DIM) == 0
    assert rwkv_cols == 3 * rwkv_w + LORA_W + LORA_A + LORA_G
    assert S % PROJ_ROWS == 0 and S % RWKV_PROJ_ROWS == 0
    assert S % RET_ROWS == 0 and S % RWKV_ROWS == 0
    assert M % ROPE_ROWS == 0 and M % MLP_ROWS == 0 and mlp_up.shape[-1] % MLP_HIDDEN == 0
    cos2, sin2 = _rope_tables(positions, tm=ROPE_ROWS)

    in_splits = (0, ret_cols, ret_cols + rwkv_cols, w_in.shape[-1])
    w_ret = w_in[0][:, :ret_cols].astype(BF16)
    full = lambda t, layer: (t, layer, (0, t.shape[-1]))
    mix_weights = (w_branch_a, w_branch_b, w_out)
    mlp_weights = (mlp_up, mlp_down)

    xf = x.reshape(M, D)
    for l in range(depth):
        more = l + 1 < depth
        next_mix = ([(w_in, l + 1, in_splits)] + [full(t, l + 1) for t in mix_weights]) if more else []
        next_mlp = [full(t, l + 1) for t in mlp_weights] if more else []
        z_qk, xn, cast_qk = _norm_qk_proj(
            xf, norm1_g[l], w_ret, cos2, sin2, n_cols=2 * ret_w,
            casts=[(w_in, 0, in_splits[1:])] if l == 0 else (), tm=QK_PROJ_ROWS)
        if l == 0:
            w_rwkv, w_gate = cast_qk
        z_vg, cast_vg = _matmul(xn, w_ret, name="in_proj_vg", epilogue="silu_upper_half",
                                col0=2 * ret_w, casts=next_mix if l == 0 else (),
                                out_dtype=BF16, tm=PROJ_ROWS, tn=2 * ret_w)
        gate_casts = [full(t, 0) for t in mix_weights + mlp_weights] if l == 0 else next_mix
        z_gate, cast_gate = _matmul(xn, w_gate, name="in_proj_gate", epilogue="sigmoid",
                                    casts=gate_casts, out_dtype=BF16, tm=PROJ_ROWS, tn=PROJ_COLS)
        if l == 0:
            w_a, w_b, w_o, w_up, w_down = cast_gate
        z_rwkv, cast_rwkv = _matmul_shift(xn, w_rwkv, rwkv_mu[l], seq_len=S, casts=next_mlp,
                                          name="in_proj_rwkv", tm=RWKV_PROJ_ROWS, tn=rwkv_cols)
        y_ret = _retention(z_qk.reshape(B, S, -1), z_vg.reshape(B, S, -1),
                           ret_gn_g[l], ret_gn_b[l])
        y_rwkv = _rwkv(z_rwkv.reshape(B, S, -1), rwkv_w0[l], rwkv_w_up[l],
                       rwkv_a0[l], rwkv_a_up[l], rwkv_g_up[l], rwkv_k_k[l], rwkv_k_a[l],
                       rwkv_r_k[l], rwkv_ln_g[l], rwkv_ln_b[l])
        xf = _merge_out(y_ret.reshape(M, ret_w), y_rwkv.reshape(M, rwkv_w),
                        w_a, w_b, z_gate, w_o, xf, tm=MERGE_ROWS)
        xf = _mlp(xf, norm2_g[l], w_up, w_down, None if more else final_g,
                  tm=MLP_ROWS, tf=MLP_HIDDEN)
        if more:
            w_ret, w_rwkv, w_gate, w_a, w_b, w_o = cast_vg if l == 0 else cast_gate
            w_up, w_down = cast_rwkv
    return xf.reshape(B, S, D)
```
